```python
import jax, jax.numpy as jnp
from jax import lax
import numpy as np

D_MODEL = 1024
BATCH = 4
SEQ = 4096
DEPTH = 1

ATT_HEADS = 8
ATT_HEAD_DIM = 64
ATT_WIDTH = ATT_HEADS * ATT_HEAD_DIM
ROPE_FRACTION_DIV = 4
ATT_ROPE_DIM = ATT_HEAD_DIM // ROPE_FRACTION_DIV
ROPE_THETA = 500000.0
IDX_HEADS = 4
IDX_DIM = 64
IDX_ROPE_DIM = IDX_DIM // ROPE_FRACTION_DIV
TOPK_MAX = 256
Q_BLOCK = 128

RET_HEADS = 4
RET_QK_DIM = 64
RET_V_DIM = 128
RET_QK_WIDTH = RET_HEADS * RET_QK_DIM
RET_V_WIDTH = RET_HEADS * RET_V_DIM
RET_CHUNK = 128
RET_THETA = 10000.0

MIX_WIDTH = ATT_WIDTH + RET_V_WIDTH
SPLITS = (ATT_WIDTH, ATT_WIDTH, ATT_WIDTH, ATT_WIDTH,
          IDX_HEADS * IDX_DIM, IDX_DIM, IDX_HEADS,
          RET_QK_WIDTH, RET_QK_WIDTH, RET_V_WIDTH, RET_V_WIDTH)
IN_WIDTH = 4 * ATT_WIDTH + IDX_HEADS * IDX_DIM + IDX_DIM + IDX_HEADS + 2 * RET_QK_WIDTH + 2 * RET_V_WIDTH
EPS = 1e-6

kernel_name = "hymba_dsa_retention_gated_layer"


def rms_norm(x, g):
    xf = x.astype(jnp.float32)
    y = xf * lax.rsqrt(jnp.mean(xf * xf, axis=-1, keepdims=True) + EPS)
    return (y * g.astype(jnp.float32)).astype(x.dtype)


def rotary(x, pos, rot_dim, theta):
    half = rot_dim // 2
    inv = 1.0 / (theta ** (jnp.arange(half, dtype=jnp.float32) / half))
    ang = pos.astype(jnp.float32)[:, None] * inv[None, :]
    cos = jnp.cos(ang)[:, None, :]
    sin = jnp.sin(ang)[:, None, :]
    xr = x[..., :rot_dim].astype(jnp.float32)
    x1, x2 = xr[..., :half], xr[..., half:]
    rot = jnp.concatenate([x1 * cos - x2 * sin, x2 * cos + x1 * sin], axis=-1).astype(x.dtype)
    return jnp.concatenate([rot, x[..., rot_dim:]], axis=-1)


def dsa_attention(q, k, v, q_idx, k_idx, w_idx, topk):
    B, S, H, hd = q.shape
    n_blocks = S // Q_BLOCK
    s_pos = jnp.arange(S)
    k_idx_f = k_idx.astype(jnp.float32)

    def block(i):
        start = i * Q_BLOCK
        qb = lax.dynamic_slice_in_dim(q, start, Q_BLOCK, axis=1)
        qib = lax.dynamic_slice_in_dim(q_idx, start, Q_BLOCK, axis=1)
        wb = lax.dynamic_slice_in_dim(w_idx, start, Q_BLOCK, axis=1)
        t = start + jnp.arange(Q_BLOCK)
        logits = jnp.einsum('bqhd,bsd->bqhs', qib.astype(jnp.float32), k_idx_f) * (IDX_DIM ** -0.5)
        score = jnp.einsum('bqh,bqhs->bqs', wb.astype(jnp.float32) * (IDX_HEADS ** -0.5),
                           jax.nn.relu(logits))
        causal = s_pos[None, :] <= t[:, None]
        score = jnp.where(causal[None], score, -jnp.inf)
        _, sel = lax.top_k(score, topk)
        valid = sel <= t[None, :, None]
        kg = jax.vmap(lambda kb, ib: kb[ib])(k, sel)
        vg = jax.vmap(lambda vb, ib: vb[ib])(v, sel)
        att = jnp.einsum('bqhd,bqkhd->bhqk', qb.astype(jnp.float32),
                         kg.astype(jnp.float32)) * (hd ** -0.5)
        att = jnp.where(valid[:, None], att, -jnp.inf)
        p = jax.nn.softmax(att, axis=-1)
        out = jnp.einsum('bhqk,bqkhd->bqhd', p, vg.astype(jnp.float32))
        return out.astype(q.dtype)

    outs = lax.map(block, jnp.arange(n_blocks))
    return outs.transpose(1, 0, 2, 3, 4).reshape(B, S, H, hd)


def retention(q, k, v):
    B, S, H, dk = q.shape
    dv = v.shape[-1]
    C = RET_CHUNK
    N = S // C
    gamma = 1.0 - 2.0 ** (-5.0 - jnp.arange(H, dtype=jnp.float32))
    log_g = jnp.log(gamma)
    qf = q.astype(jnp.float32).reshape(B, N, C, H, dk)
    kf = (k.astype(jnp.float32) * (dk ** -0.5)).reshape(B, N, C, H, dk)
    vf = v.astype(jnp.float32).reshape(B, N, C, H, dv)
    idx = jnp.arange(C, dtype=jnp.float32)
    diff = idx[:, None] - idx[None, :]
    decay = jnp.where(diff[None] >= 0,
                      jnp.exp(log_g[:, None, None] * jnp.maximum(diff, 0.0)[None]), 0.0)
    scores = jnp.einsum('bnihd,bnjhd->bnhij', qf, kf) * decay[None, None]
    inner = jnp.einsum('bnhij,bnjhe->bnihe', scores, vf)
    zeta = jnp.exp(log_g[:, None] * (C - 1.0 - idx)[None, :])
    kv = jnp.einsum('bnjhd,hj,bnjhe->bnhde', kf, zeta, vf)
    g_chunk = jnp.exp(log_g * C)[None, :, None, None]

    def step(state, kv_n):
        return state * g_chunk + kv_n, state

    _, r_prev = lax.scan(step, jnp.zeros((B, H, dk, dv), jnp.float32), kv.transpose(1, 0, 2, 3, 4))
    r_prev = r_prev.transpose(1, 0, 2, 3, 4)
    xi = jnp.exp(log_g[:, None] * (idx + 1.0)[None, :])
    cross = jnp.einsum('bnihd,hi,bnhde->bnihe', qf, xi, r_prev)
    out = (inner + cross).reshape(B, S, H, dv)
    out = out * lax.rsqrt(jnp.mean(out * out, axis=-1, keepdims=True) + EPS)
    return out.astype(q.dtype)


def setup_inputs(seed: int = 0) -> dict:
    key = jax.random.key(seed)
    kx, kg, kin, kout, kf = jax.random.split(key, 5)
    x = jax.random.normal(kx, (BATCH, SEQ, D_MODEL), jnp.float32)
    norm_gain = 1.0 + 0.02 * jax.random.normal(kg, (DEPTH, D_MODEL), jnp.float32)
    w_in = jax.random.normal(kin, (DEPTH, D_MODEL, IN_WIDTH), jnp.float32) * (D_MODEL ** -0.5)
    w_out = jax.random.normal(kout, (DEPTH, MIX_WIDTH, D_MODEL), jnp.float32) * (MIX_WIDTH ** -0.5)
    final_gain = 1.0 + 0.02 * jax.random.normal(kf, (D_MODEL,), jnp.float32)
    return {"x": x, "norm_gain": norm_gain, "w_in": w_in, "w_out": w_out, "final_gain": final_gain}


def reference(x, norm_gain, w_in, w_out, final_gain):
    B, S, _ = x.shape
    pos = jnp.arange(S)
    topk = min(TOPK_MAX, S // 4)
    split_points = np.cumsum(SPLITS)[:-1].tolist()
    h = x
    for layer in range(DEPTH):
        u = rms_norm(h, norm_gain[layer])
        z = jnp.einsum('bsd,de->bse', u, w_in[layer])
        (q_a, k_a, v_a, g_a, q_i, k_i, w_i, q_r, k_r, v_r, g_r) = jnp.split(z, split_points, axis=-1)
        q_a = rotary(q_a.reshape(B, S, ATT_HEADS, ATT_HEAD_DIM), pos, ATT_ROPE_DIM, ROPE_THETA)
        k_a = rotary(k_a.reshape(B, S, ATT_HEADS, ATT_HEAD_DIM), pos, ATT_ROPE_DIM, ROPE_THETA)
        v_a = v_a.reshape(B, S, ATT_HEADS, ATT_HEAD_DIM)
        q_i = rotary(q_i.reshape(B, S, IDX_HEADS, IDX_DIM), pos, IDX_ROPE_DIM, ROPE_THETA)
        k_i = rotary(k_i.reshape(B, S, 1, IDX_DIM), pos, IDX_ROPE_DIM, ROPE_THETA)[:, :, 0, :]
        y_a = dsa_attention(q_a, k_a, v_a, q_i, k_i, w_i, topk).reshape(B, S, ATT_WIDTH)
        y_a = y_a * jax.nn.silu(g_a)
        q_r = rotary(q_r.reshape(B, S, RET_HEADS, RET_QK_DIM), pos, RET_QK_DIM, RET_THETA)
        k_r = rotary(k_r.reshape(B, S, RET_HEADS, RET_QK_DIM), pos, RET_QK_DIM, RET_THETA)
        v_r = v_r.reshape(B, S, RET_HEADS, RET_V_DIM)
        y_r = retention(q_r, k_r, v_r).reshape(B, S, RET_V_WIDTH)
        y_r = y_r * jax.nn.silu(g_r)
        mix = jnp.concatenate([y_a, y_r], axis=-1)
        h = h + jnp.einsum('bse,ed->bsd', mix, w_out[layer])
    return rms_norm(h, final_gain)
```

```python
import functools

import jax
import jax.numpy as jnp
import numpy as np
from jax import lax
from jax.experimental import pallas as pl
from jax.experimental.pallas import tpu as pltpu

D_MODEL = 1024
ATT_HEADS = 8
ATT_HEAD_DIM = 64
ATT_WIDTH = ATT_HEADS * ATT_HEAD_DIM
ATT_ROPE_DIM = ATT_HEAD_DIM // 4
ROPE_THETA = 500000.0
IDX_HEADS = 4
IDX_DIM = 64
IDX_WIDTH = IDX_HEADS * IDX_DIM
IDX_ROPE_DIM = IDX_DIM // 4
TOPK_MAX = 256
Q_BLOCK = 128
RET_HEADS = 4
RET_QK_DIM = 64
RET_V_DIM = 128
RET_QK_WIDTH = RET_HEADS * RET_QK_DIM
RET_V_WIDTH = RET_HEADS * RET_V_DIM
RET_CHUNK = 128
RET_THETA = 10000.0
MIX_WIDTH = ATT_WIDTH + RET_V_WIDTH
SPLITS = (ATT_WIDTH, ATT_WIDTH, ATT_WIDTH, ATT_WIDTH, IDX_WIDTH, IDX_DIM, IDX_HEADS,
          RET_QK_WIDTH, RET_QK_WIDTH, RET_V_WIDTH, RET_V_WIDTH)
EPS = 1e-6

LANES = 128
MASK_VALUE = -1e30
VMEM_LIMIT_BYTES = 48 * 1024 * 1024

OFF_QA = 0
OFF_KA = OFF_QA + ATT_WIDTH
OFF_VA = OFF_KA + ATT_WIDTH
OFF_GA = OFF_VA + ATT_WIDTH
OFF_QI = OFF_GA + ATT_WIDTH
OFF_QR = OFF_QI + IDX_WIDTH
OFF_KR = OFF_QR + RET_QK_WIDTH
OFF_VR = OFF_KR + RET_QK_WIDTH
OFF_GR = OFF_VR + RET_V_WIDTH
OFF_KIW = OFF_GR + RET_V_WIDTH
PROJ_WIDTH = OFF_KIW + LANES
W_IDX_LANE = IDX_DIM


def _rotary_tables(seq, head_dim, rot_dim, theta):
    half = rot_dim // 2
    inv = 1.0 / (theta ** (jnp.arange(half, dtype=jnp.float32) / half))
    ang = jnp.arange(seq).astype(jnp.float32)[:, None] * inv[None, :]
    cos = jnp.cos(ang)
    sin = jnp.sin(ang)
    pad = head_dim - rot_dim
    ones = jnp.ones((seq, pad), jnp.float32)
    zeros = jnp.zeros((seq, pad), jnp.float32)
    zh = jnp.zeros((seq, half), jnp.float32)
    c = jnp.concatenate([cos, cos, ones], axis=1)
    s_prev = jnp.concatenate([zh, sin, zeros], axis=1)
    s_next = jnp.concatenate([-sin, zh, zeros], axis=1)
    reps = LANES // head_dim
    return jnp.stack([jnp.tile(c, (1, reps)), jnp.tile(s_prev, (1, reps)),
                      jnp.tile(s_next, (1, reps))])


def _retention_tables():
    c = RET_CHUNK
    gamma = 1.0 - 2.0 ** (-5.0 - jnp.arange(RET_HEADS, dtype=jnp.float32))
    log_g = jnp.log(gamma)
    idx = jnp.arange(c, dtype=jnp.float32)
    diff = idx[:, None] - idx[None, :]
    decay = jnp.where(diff[None] >= 0,
                      jnp.exp(log_g[:, None, None] * jnp.maximum(diff, 0.0)[None]), 0.0)
    zeta = jnp.exp(log_g[:, None] * (c - 1.0 - idx)[None, :])
    xi = jnp.exp(log_g[:, None] * (idx + 1.0)[None, :])
    g_chunk = jnp.exp(log_g * c)
    zeta_b = jnp.broadcast_to(zeta[:, :, None], (RET_HEADS, c, LANES))
    xi_b = jnp.broadcast_to(xi[:, :, None], (RET_HEADS, c, LANES))
    g_rows = jnp.repeat(g_chunk, RET_QK_DIM)[:, None]
    g_rows = jnp.broadcast_to(g_rows, (RET_HEADS * RET_QK_DIM, RET_V_DIM))
    return decay, zeta_b, xi_b, g_rows


def _rope(z, tab_ref, half):
    return (z * tab_ref[0] + pltpu.roll(z, half, 1) * tab_ref[1]
            + pltpu.roll(z, LANES - half, 1) * tab_ref[2])


def _silu(g):
    return g * (1.0 / (1.0 + jnp.exp(-g)))


def _in_proj_kernel(x_ref, gain_ref, w_ref, rope_a_ref, rope_r_ref,
                    qa_ref, ka_ref, va_ref, ga_ref, qi_ref, qr_ref, kr_ref, vr_ref, gr_ref, kiw_ref,
                    u_ref):
    x = x_ref[0]
    ms = jnp.mean(x * x, axis=-1, keepdims=True)
    u_ref[...] = ((x * lax.rsqrt(ms + EPS)) * gain_ref[...]).astype(jnp.bfloat16)

    def proj(off):
        return jnp.dot(u_ref[...], w_ref[:, off:off + LANES], preferred_element_type=jnp.float32)

    a_half = ATT_ROPE_DIM // 2
    r_half = RET_QK_DIM // 2
    att_scale = ATT_HEAD_DIM ** -0.5
    ret_scale = RET_QK_DIM ** -0.5
    for j in range(ATT_WIDTH // LANES):
        sl = slice(j * LANES, (j + 1) * LANES)
        qa_ref[0, :, sl] = (_rope(proj(OFF_QA + j * LANES), rope_a_ref, a_half) * att_scale).astype(qa_ref.dtype)
        ka_ref[0, :, sl] = _rope(proj(OFF_KA + j * LANES), rope_a_ref, a_half).astype(ka_ref.dtype)
        va_ref[0, :, sl] = proj(OFF_VA + j * LANES).astype(va_ref.dtype)
        ga_ref[0, :, sl] = _silu(proj(OFF_GA + j * LANES)).astype(ga_ref.dtype)
        vr_ref[0, :, sl] = proj(OFF_VR + j * LANES).astype(vr_ref.dtype)
        gr_ref[0, :, sl] = _silu(proj(OFF_GR + j * LANES)).astype(gr_ref.dtype)
    for j in range(IDX_WIDTH // LANES):
        sl = slice(j * LANES, (j + 1) * LANES)
        qi_ref[0, :, sl] = _rope(proj(OFF_QI + j * LANES), rope_a_ref, a_half).astype(qi_ref.dtype)
        qr_ref[0, :, sl] = _rope(proj(OFF_QR + j * LANES), rope_r_ref, r_half).astype(qr_ref.dtype)
        kr_ref[0, :, sl] = (_rope(proj(OFF_KR + j * LANES), rope_r_ref, r_half) * ret_scale).astype(kr_ref.dtype)
    z = proj(OFF_KIW)
    lane = lax.broadcasted_iota(jnp.int32, z.shape, 1)
    w_scale = (IDX_DIM ** -0.5) * (IDX_HEADS ** -0.5)
    kiw_ref[0] = jnp.where(lane < W_IDX_LANE, _rope(z, rope_a_ref, a_half), z * w_scale)


def _in_proj(x, gain, w, rope_a, rope_r, tm):
    b, s, d = x.shape
    row = lambda width: pl.BlockSpec((1, tm, width), lambda si, bi: (bi, si, 0))
    tab = pl.BlockSpec((3, tm, LANES), lambda si, bi: (0, si, 0))
    bf = jnp.bfloat16
    out_shapes = [
        jax.ShapeDtypeStruct((b, s, ATT_WIDTH), bf),
        jax.ShapeDtypeStruct((b, s, ATT_WIDTH), bf),
        jax.ShapeDtypeStruct((b, s, ATT_WIDTH), bf),
        jax.ShapeDtypeStruct((b, s, ATT_WIDTH), bf),
        jax.ShapeDtypeStruct((b, s, IDX_WIDTH), bf),
        jax.ShapeDtypeStruct((b, s, RET_QK_WIDTH), bf),
        jax.ShapeDtypeStruct((b, s, RET_QK_WIDTH), bf),
        jax.ShapeDtypeStruct((b, s, RET_V_WIDTH), bf),
        jax.ShapeDtypeStruct((b, s, RET_V_WIDTH), bf),
        jax.ShapeDtypeStruct((b, s, LANES), jnp.float32),
    ]
    out_specs = [row(ATT_WIDTH)] * 4 + [row(IDX_WIDTH), row(RET_QK_WIDTH), row(RET_QK_WIDTH),
                                         row(RET_V_WIDTH), row(RET_V_WIDTH), row(LANES)]
    return pl.pallas_call(
        _in_proj_kernel,
        grid=(s // tm, b),
        in_specs=[row(d),
                  pl.BlockSpec((1, d), lambda si, bi: (0, 0)),
                  pl.BlockSpec((d, PROJ_WIDTH), lambda si, bi: (0, 0)),
                  tab, tab],
        out_specs=out_specs,
        out_shape=out_shapes,
        scratch_shapes=[pltpu.VMEM((tm, d), jnp.bfloat16)],
        compiler_params=pltpu.CompilerParams(
            dimension_semantics=("arbitrary", "arbitrary"), vmem_limit_bytes=VMEM_LIMIT_BYTES),
        name="in_proj",
    )(x, gain, w, rope_a, rope_r)


_NT = (((1,), (1,)), ((), ()))


def _key_to_float(key):
    k = key ^ jnp.int32(-2 ** 31)
    bits = jnp.where(k >= 0, k, k ^ jnp.int32(2 ** 31 - 1))
    return lax.bitcast_convert_type(bits, jnp.float32)


def _dsa_kernel(qi_ref, wq_ref, kiw_ref, qa_ref, ka_ref, va_ref, ga_ref, tri_ref,
                out_ref, sc_ref, bias_ref, *, topk):
    i = pl.program_id(1)
    n_chunks = i + 1
    row = i * Q_BLOCK + lax.broadcasted_iota(jnp.int32, (Q_BLOCK, 1), 0)
    col0 = lax.broadcasted_iota(jnp.int32, (1, LANES), 1)
    lane = lax.broadcasted_iota(jnp.int32, (Q_BLOCK, LANES), 1)

    qi = qi_ref[0]
    wq = wq_ref[0]
    w_heads = [wq[:, W_IDX_LANE + h:W_IDX_LANE + h + 1] for h in range(IDX_HEADS)]

    def score_chunk(c, carry):
        start = pl.multiple_of(c * LANES, LANES)
        ki = kiw_ref[0, pl.ds(start, LANES), :][:, :IDX_DIM].astype(jnp.bfloat16)
        acc = jnp.zeros((Q_BLOCK, LANES), jnp.float32)
        for h in range(IDX_HEADS):
            logit = lax.dot_general(qi[:, h * IDX_DIM:(h + 1) * IDX_DIM], ki, _NT,
                                    preferred_element_type=jnp.float32)
            acc = acc + w_heads[h] * jnp.maximum(logit, 0.0)
        sc_ref[c] = jnp.where(start + col0 <= row, acc, -jnp.inf)
        return carry

    lax.fori_loop(0, n_chunks, score_chunk, 0)

    def count(pred):
        def body(c, acc):
            return acc + jnp.where(pred(sc_ref[c]), 1.0, 0.0)
        acc = lax.fori_loop(0, n_chunks, body, jnp.zeros((Q_BLOCK, LANES), jnp.float32))
        return jnp.sum(acc, axis=1, keepdims=True)

    @pl.when(n_chunks * LANES <= topk)
    def _():
        def body(c, carry):
            start = c * LANES
            bias_ref[c] = jnp.where(start + col0 <= row, 0.0, MASK_VALUE)
            return carry
        lax.fori_loop(0, n_chunks, body, 0)

    @pl.when(n_chunks * LANES > topk)
    def _():
        def bit_step(b, key):
            cand = key | jnp.left_shift(jnp.int32(1), 31 - b)
            f = _key_to_float(cand)
            total = count(lambda s: s >= f)
            return jnp.where(total >= topk, cand, key)

        key = lax.fori_loop(0, 32, bit_step, jnp.zeros((Q_BLOCK, 1), jnp.int32))
        thr = _key_to_float(key)
        need = topk - count(lambda s: s > thr)

        def body(c, run):
            s = sc_ref[c]
            eq = s == thr
            eq_f = jnp.where(eq, 1.0, 0.0)
            incl = jnp.dot(eq_f.astype(jnp.bfloat16), tri_ref[...],
                           preferred_element_type=jnp.float32) + run
            sel = (s > thr) | (eq & (incl <= need))
            bias_ref[c] = jnp.where(sel, 0.0, MASK_VALUE)
            return run + jnp.sum(eq_f, axis=1, keepdims=True)

        lax.fori_loop(0, n_chunks, body, jnp.zeros((Q_BLOCK, 1), jnp.float32))

    for pair in range(ATT_HEADS // 2):
        psl = slice(pair * LANES, (pair + 1) * LANES)
        q_pair = qa_ref[0, :, psl]
        accs = []
        for sub in range(2):
            in_head = (lane >= sub * ATT_HEAD_DIM) & (lane < (sub + 1) * ATT_HEAD_DIM)
            q_h = jnp.where(in_head, q_pair, jnp.zeros_like(q_pair))

            def att_chunk(c, carry, q_h=q_h, psl=psl):
                m, l, acc = carry
                start = pl.multiple_of(c * LANES, LANES)
                k = ka_ref[0, pl.ds(start, LANES), psl]
                v = va_ref[0, pl.ds(start, LANES), psl]
                s = lax.dot_general(q_h, k, _NT, preferred_element_type=jnp.float32) + bias_ref[c]
                m_new = jnp.maximum(m, jnp.max(s, axis=1, keepdims=True))
                alpha = jnp.exp(m - m_new)
                p = jnp.exp(s - m_new)
                l = alpha * l + jnp.sum(p, axis=1, keepdims=True)
                acc = alpha * acc + jnp.dot(p.astype(jnp.bfloat16), v,
                                            preferred_element_type=jnp.float32)
                return m_new, l, acc

            m0 = jnp.full((Q_BLOCK, 1), MASK_VALUE, jnp.float32)
            l0 = jnp.zeros((Q_BLOCK, 1), jnp.float32)
            a0 = jnp.zeros((Q_BLOCK, LANES), jnp.float32)
            _, l, acc = lax.fori_loop(0, n_chunks, att_chunk, (m0, l0, a0))
            accs.append(acc / l)
        y = jnp.where(lane < ATT_HEAD_DIM, accs[0], accs[1])
        out_ref[0, :, psl] = (y * ga_ref[0, :, psl].astype(jnp.float32)).astype(out_ref.dtype)


def _dsa(qi, kiw, qa, ka, va, ga, tri, topk):
    b, s, _ = qa.shape
    nq = s // Q_BLOCK
    qblk = lambda width: pl.BlockSpec((1, Q_BLOCK, width), lambda bi, i: (bi, i, 0))
    full = lambda width: pl.BlockSpec((1, s, width), lambda bi, i: (bi, 0, 0))
    return pl.pallas_call(
        functools.partial(_dsa_kernel, topk=topk),
        grid=(b, nq),
        in_specs=[qblk(IDX_WIDTH), qblk(LANES), full(LANES), qblk(ATT_WIDTH),
                  full(ATT_WIDTH), full(ATT_WIDTH), qblk(ATT_WIDTH),
                  pl.BlockSpec((LANES, LANES), lambda bi, i: (0, 0))],
        out_specs=qblk(ATT_WIDTH),
        out_shape=jax.ShapeDtypeStruct((b, s, ATT_WIDTH), jnp.bfloat16),
        scratch_shapes=[pltpu.VMEM((nq, Q_BLOCK, LANES), jnp.float32),
                        pltpu.VMEM((nq, Q_BLOCK, LANES), jnp.float32)],
        compiler_params=pltpu.CompilerParams(
            dimension_semantics=("arbitrary", "arbitrary"), vmem_limit_bytes=VMEM_LIMIT_BYTES),
        name="sparse_attention",
    )(qi, kiw, kiw, qa, ka, va, ga, tri)


_TN = (((0,), (0,)), ((), ()))


def _retention_kernel(qr_ref, kr_ref, vr_ref, gr_ref, decay_ref, zeta_ref, xi_ref, grow_ref,
                      out_ref, state_ref):
    @pl.when(pl.program_id(1) == 0)
    def _():
        state_ref[...] = jnp.zeros_like(state_ref)

    lane = lax.broadcasted_iota(jnp.int32, (RET_CHUNK, LANES), 1)
    for pair in range(RET_HEADS // 2):
        psl = slice(pair * LANES, (pair + 1) * LANES)
        q_pair = qr_ref[0, :, psl].astype(jnp.float32)
        k_pair = kr_ref[0, :, psl].astype(jnp.float32)
        state = state_ref[psl, :]
        state_b = state.astype(jnp.bfloat16)
        kv = jnp.zeros((LANES, RET_V_DIM), jnp.float32)
        for sub in range(2):
            h = 2 * pair + sub
            vsl = slice(h * RET_V_DIM, (h + 1) * RET_V_DIM)
            in_head = (lane >= sub * RET_QK_DIM) & (lane < (sub + 1) * RET_QK_DIM)
            q_h = jnp.where(in_head, q_pair, 0.0)
            k_h = jnp.where(in_head, k_pair, 0.0)
            v = vr_ref[0, :, vsl]
            scores = lax.dot_general(q_h.astype(jnp.bfloat16), k_h.astype(jnp.bfloat16), _NT,
                                     preferred_element_type=jnp.float32) * decay_ref[h]
            inner = jnp.dot(scores.astype(jnp.bfloat16), v, preferred_element_type=jnp.float32)
            cross = jnp.dot((q_h * xi_ref[h]).astype(jnp.bfloat16), state_b,
                            preferred_element_type=jnp.float32)
            o = inner + cross
            o = o * lax.rsqrt(jnp.mean(o * o, axis=-1, keepdims=True) + EPS)
            out_ref[0, :, vsl] = (o * gr_ref[0, :, vsl].astype(jnp.float32)).astype(out_ref.dtype)
            kv = kv + lax.dot_general((k_h * zeta_ref[h]).astype(jnp.bfloat16), v, _TN,
                                      preferred_element_type=jnp.float32)
        state_ref[psl, :] = state * grow_ref[psl, :] + kv


def _retention(qr, kr, vr, gr, tables):
    b, s, _ = vr.shape
    decay, zeta_b, xi_b, g_rows = tables
    blk = lambda width: pl.BlockSpec((1, RET_CHUNK, width), lambda bi, i: (bi, i, 0))
    const3 = lambda a: pl.BlockSpec(a.shape, lambda bi, i: (0, 0, 0))
    return pl.pallas_call(
        _retention_kernel,
        grid=(b, s // RET_CHUNK),
        in_specs=[blk(RET_QK_WIDTH), blk(RET_QK_WIDTH), blk(RET_V_WIDTH), blk(RET_V_WIDTH),
                  const3(decay), const3(zeta_b), const3(xi_b),
                  pl.BlockSpec(g_rows.shape, lambda bi, i: (0, 0))],
        out_specs=blk(RET_V_WIDTH),
        out_shape=jax.ShapeDtypeStruct((b, s, RET_V_WIDTH), jnp.bfloat16),
        scratch_shapes=[pltpu.VMEM((RET_HEADS * RET_QK_DIM, RET_V_DIM), jnp.float32)],
        compiler_params=pltpu.CompilerParams(
            dimension_semantics=("arbitrary", "arbitrary"), vmem_limit_bytes=VMEM_LIMIT_BYTES),
        name="retention",
    )(qr, kr, vr, gr, decay, zeta_b, xi_b, g_rows)


def _out_proj_kernel(x_ref, ya_ref, yr_ref, wa_ref, wr_ref, gain_ref, out_ref):
    h = (x_ref[...]
         + jnp.dot(ya_ref[...], wa_ref[...], preferred_element_type=jnp.float32)
         + jnp.dot(yr_ref[...], wr_ref[...], preferred_element_type=jnp.float32))
    ms = jnp.mean(h * h, axis=-1, keepdims=True)
    out_ref[...] = (h * lax.rsqrt(ms + EPS)) * gain_ref[...]


def _out_proj(x2, ya2, yr2, wa, wr, gain, tm):
    n, d = x2.shape
    row = lambda width: pl.BlockSpec((tm, width), lambda r: (r, 0))
    const = lambda a: pl.BlockSpec(a.shape, lambda r: (0, 0))
    return pl.pallas_call(
        _out_proj_kernel,
        grid=(n // tm,),
        in_specs=[row(d), row(ATT_WIDTH), row(RET_V_WIDTH), const(wa), const(wr), const(gain)],
        out_specs=row(d),
        out_shape=jax.ShapeDtypeStruct((n, d), jnp.float32),
        compiler_params=pltpu.CompilerParams(
            dimension_semantics=("arbitrary",), vmem_limit_bytes=VMEM_LIMIT_BYTES),
        name="out_proj",
    )(x2, ya2, yr2, wa, wr, gain)


def _reorder_weight(w):
    pts = np.cumsum(SPLITS)[:-1].tolist()
    q_a, k_a, v_a, g_a, q_i, k_i, w_i, q_r, k_r, v_r, g_r = jnp.split(w, pts, axis=1)
    pad = jnp.zeros((w.shape[0], LANES - IDX_DIM - IDX_HEADS), w.dtype)
    cols = jnp.concatenate([q_a, k_a, v_a, g_a, q_i, q_r, k_r, v_r, g_r, k_i, w_i, pad], axis=1)
    return cols.astype(jnp.bfloat16)


def kernel(x, norm_gain, w_in, w_out, final_gain):
    b, s, d = x.shape
    depth = norm_gain.shape[0]
    assert d == D_MODEL and s % Q_BLOCK == 0 and w_in.shape[2] == sum(SPLITS)
    topk = min(TOPK_MAX, s // 4)
    rope_a = _rotary_tables(s, ATT_HEAD_DIM, ATT_ROPE_DIM, ROPE_THETA)
    rope_r = _rotary_tables(s, RET_QK_DIM, RET_QK_DIM, RET_THETA)
    ret_tables = _retention_tables()
    tri = (jnp.arange(LANES)[:, None] <= jnp.arange(LANES)[None, :]).astype(jnp.bfloat16)
    tm = 256

    assert depth == 1, "the final norm is fused into the single layer's output projection"
    w = _reorder_weight(w_in[0])
    qa, ka, va, ga, qi, qr, kr, vr, gr, kiw = _in_proj(
        x, norm_gain[0][None, :], w, rope_a, rope_r, tm)
    ya = _dsa(qi, kiw, qa, ka, va, ga, tri, topk)
    yr = _retention(qr, kr, vr, gr, ret_tables)
    wo = w_out[0].astype(jnp.bfloat16)
    out = _out_proj(x.reshape(b * s, d), ya.reshape(b * s, ATT_WIDTH),
                    yr.reshape(b * s, RET_V_WIDTH), wo[:ATT_WIDTH], wo[ATT_WIDTH:],
                    final_gain[None, :], 512)
    return out.reshape(b, s, d)
```

```python
import functools

import jax
import jax.numpy as jnp
import numpy as np
from jax import lax
from jax.experimental import pallas as pl
from jax.experimental.pallas import tpu as pltpu

D_MODEL = 1024
ATT_HEADS = 8
ATT_HEAD_DIM = 64
ATT_WIDTH = ATT_HEADS * ATT_HEAD_DIM
ATT_ROPE_DIM = ATT_HEAD_DIM // 4
ROPE_THETA = 500000.0
IDX_HEADS = 4
IDX_DIM = 64
IDX_WIDTH = IDX_HEADS * IDX_DIM
IDX_ROPE_DIM = IDX_DIM // 4
TOPK_MAX = 256
Q_BLOCK = 128
RET_HEADS = 4
RET_QK_DIM = 64
RET_V_DIM = 128
RET_QK_WIDTH = RET_HEADS * RET_QK_DIM
RET_V_WIDTH = RET_HEADS * RET_V_DIM
RET_CHUNK = 128
RET_THETA = 10000.0
MIX_WIDTH = ATT_WIDTH + RET_V_WIDTH
SPLITS = (ATT_WIDTH, ATT_WIDTH, ATT_WIDTH, ATT_WIDTH, IDX_WIDTH, IDX_DIM, IDX_HEADS,
          RET_QK_WIDTH, RET_QK_WIDTH, RET_V_WIDTH, RET_V_WIDTH)
EPS = 1e-6

LANES = 128
MASK_VALUE = -1e30
VMEM_LIMIT_BYTES = 48 * 1024 * 1024

OFF_QA = 0
OFF_KA = OFF_QA + ATT_WIDTH
OFF_VA = OFF_KA + ATT_WIDTH
OFF_GA = OFF_VA + ATT_WIDTH
OFF_QI = OFF_GA + ATT_WIDTH
OFF_QR = OFF_QI + IDX_WIDTH
OFF_KR = OFF_QR + RET_QK_WIDTH
OFF_VR = OFF_KR + RET_QK_WIDTH
OFF_GR = OFF_VR + RET_V_WIDTH
OFF_KI = OFF_GR + RET_V_WIDTH
OFF_WI = OFF_KI + LANES
PROJ_WIDTH = OFF_WI + LANES
KEY_CHUNK = 256


def _rotary_tables(seq, head_dim, rot_dim, theta):
    half = rot_dim // 2
    inv = 1.0 / (theta ** (jnp.arange(half, dtype=jnp.float32) / half))
    ang = jnp.arange(seq).astype(jnp.float32)[:, None] * inv[None, :]
    cos = jnp.cos(ang)
    sin = jnp.sin(ang)
    pad = head_dim - rot_dim
    ones = jnp.ones((seq, pad), jnp.float32)
    zeros = jnp.zeros((seq, pad), jnp.float32)
    zh = jnp.zeros((seq, half), jnp.float32)
    c = jnp.concatenate([cos, cos, ones], axis=1)
    s_prev = jnp.concatenate([zh, sin, zeros], axis=1)
    s_next = jnp.concatenate([-sin, zh, zeros], axis=1)
    reps = LANES // head_dim
    return jnp.stack([jnp.tile(c, (1, reps)), jnp.tile(s_prev, (1, reps)),
                      jnp.tile(s_next, (1, reps))])


def _retention_tables():
    c = RET_CHUNK
    gamma = 1.0 - 2.0 ** (-5.0 - jnp.arange(RET_HEADS, dtype=jnp.float32))
    log_g = jnp.log(gamma)
    idx = jnp.arange(c, dtype=jnp.float32)
    diff = idx[:, None] - idx[None, :]
    decay = jnp.where(diff[None] >= 0,
                      jnp.exp(log_g[:, None, None] * jnp.maximum(diff, 0.0)[None]), 0.0)
    zeta = jnp.exp(log_g[:, None] * (c - 1.0 - idx)[None, :])
    xi = jnp.exp(log_g[:, None] * (idx + 1.0)[None, :])
    g_chunk = jnp.exp(log_g * c)
    zeta_b = jnp.broadcast_to(zeta[:, :, None], (RET_HEADS, c, LANES))
    xi_b = jnp.broadcast_to(xi[:, :, None], (RET_HEADS, c, LANES))
    g_rows = jnp.repeat(g_chunk, RET_QK_DIM)[:, None]
    g_rows = jnp.broadcast_to(g_rows, (RET_HEADS * RET_QK_DIM, RET_V_DIM))
    return decay, zeta_b, xi_b, g_rows


def _rope(z, tab_ref, half):
    return (z * tab_ref[0] + pltpu.roll(z, half, 1) * tab_ref[1]
            + pltpu.roll(z, LANES - half, 1) * tab_ref[2])


def _silu(g):
    return g * (1.0 / (1.0 + jnp.exp(-g)))


def _in_proj_kernel(x_ref, gain_ref, w_ref, rope_a_ref, rope_r_ref,
                    qa_ref, ka_ref, vt_ref, ga_ref, qi_ref, qr_ref, kr_ref, vr_ref, gr_ref,
                    ki_ref, wi_ref, u_ref):
    x = x_ref[0]
    ms = jnp.mean(x * x, axis=-1, keepdims=True)
    u_ref[...] = ((x * lax.rsqrt(ms + EPS)) * gain_ref[...]).astype(jnp.bfloat16)

    def proj(off):
        return jnp.dot(u_ref[...], w_ref[:, off:off + LANES], preferred_element_type=jnp.float32)

    a_half = ATT_ROPE_DIM // 2
    r_half = RET_QK_DIM // 2
    att_scale = ATT_HEAD_DIM ** -0.5
    ret_scale = RET_QK_DIM ** -0.5
    for j in range(ATT_WIDTH // LANES):
        sl = slice(j * LANES, (j + 1) * LANES)
        qa_ref[0, :, sl] = (_rope(proj(OFF_QA + j * LANES), rope_a_ref, a_half) * att_scale).astype(qa_ref.dtype)
        ka_ref[0, :, sl] = _rope(proj(OFF_KA + j * LANES), rope_a_ref, a_half).astype(ka_ref.dtype)
        vt_ref[0, 0, sl, :] = proj(OFF_VA + j * LANES).T.astype(vt_ref.dtype)
        ga_ref[0, :, sl] = _silu(proj(OFF_GA + j * LANES)).astype(ga_ref.dtype)
        vr_ref[0, :, sl] = proj(OFF_VR + j * LANES).astype(vr_ref.dtype)
        gr_ref[0, :, sl] = _silu(proj(OFF_GR + j * LANES)).astype(gr_ref.dtype)
    for j in range(IDX_WIDTH // LANES):
        sl = slice(j * LANES, (j + 1) * LANES)
        qi_ref[0, :, sl] = _rope(proj(OFF_QI + j * LANES), rope_a_ref, a_half).astype(qi_ref.dtype)
        qr_ref[0, :, sl] = _rope(proj(OFF_QR + j * LANES), rope_r_ref, r_half).astype(qr_ref.dtype)
        kr_ref[0, :, sl] = (_rope(proj(OFF_KR + j * LANES), rope_r_ref, r_half) * ret_scale).astype(kr_ref.dtype)
    ki_ref[0] = _rope(proj(OFF_KI), rope_a_ref, a_half).astype(ki_ref.dtype)
    wi_ref[0] = proj(OFF_WI) * ((IDX_DIM ** -0.5) * (IDX_HEADS ** -0.5))


def _in_proj(x, gain, w, rope_a, rope_r):
    b, s, d = x.shape
    tm = KEY_CHUNK
    row = lambda width: pl.BlockSpec((1, tm, width), lambda si, bi: (bi, si, 0))
    tab = pl.BlockSpec((3, tm, LANES), lambda si, bi: (0, si, 0))
    vt_spec = pl.BlockSpec((1, 1, ATT_WIDTH, tm), lambda si, bi: (bi, si, 0, 0))
    bf = jnp.bfloat16
    out_shapes = [
        jax.ShapeDtypeStruct((b, s, ATT_WIDTH), bf),
        jax.ShapeDtypeStruct((b, s, ATT_WIDTH), bf),
        jax.ShapeDtypeStruct((b, s // tm, ATT_WIDTH, tm), bf),
        jax.ShapeDtypeStruct((b, s, ATT_WIDTH), bf),
        jax.ShapeDtypeStruct((b, s, IDX_WIDTH), bf),
        jax.ShapeDtypeStruct((b, s, RET_QK_WIDTH), bf),
        jax.ShapeDtypeStruct((b, s, RET_QK_WIDTH), bf),
        jax.ShapeDtypeStruct((b, s, RET_V_WIDTH), bf),
        jax.ShapeDtypeStruct((b, s, RET_V_WIDTH), bf),
        jax.ShapeDtypeStruct((b, s, LANES), bf),
        jax.ShapeDtypeStruct((b, s, LANES), jnp.float32),
    ]
    out_specs = [row(ATT_WIDTH), row(ATT_WIDTH), vt_spec, row(ATT_WIDTH),
                 row(IDX_WIDTH), row(RET_QK_WIDTH), row(RET_QK_WIDTH),
                 row(RET_V_WIDTH), row(RET_V_WIDTH), row(LANES), row(LANES)]
    return pl.pallas_call(
        _in_proj_kernel,
        grid=(s // tm, b),
        in_specs=[row(d),
                  pl.BlockSpec((1, d), lambda si, bi: (0, 0)),
                  pl.BlockSpec((d, PROJ_WIDTH), lambda si, bi: (0, 0)),
                  tab, tab],
        out_specs=out_specs,
        out_shape=out_shapes,
        scratch_shapes=[pltpu.VMEM((tm, d), jnp.bfloat16)],
        compiler_params=pltpu.CompilerParams(
            dimension_semantics=("arbitrary", "arbitrary"), vmem_limit_bytes=VMEM_LIMIT_BYTES),
        name="in_proj",
    )(x, gain, w, rope_a, rope_r)


_NT = (((1,), (1,)), ((), ()))


def _key_to_float(key):
    k = key ^ jnp.int32(-2 ** 31)
    bits = jnp.where(k >= 0, k, k ^ jnp.int32(2 ** 31 - 1))
    return lax.bitcast_convert_type(bits, jnp.float32)


def _split_heads_t(blk):
    blk_t = blk.astype(jnp.float32).T
    feat = lax.broadcasted_iota(jnp.int32, blk_t.shape, 0)
    zero = jnp.zeros_like(blk_t)
    both = jnp.concatenate([jnp.where(feat < LANES // 2, blk_t, zero),
                            jnp.where(feat >= LANES // 2, blk_t, zero)], axis=1)
    return both.astype(jnp.bfloat16)


def _dsa_kernel(qi_ref, wi_ref, ki_ref, qa_ref, ka_ref, vt_ref, ga_ref, tri_ref,
                out_ref, sc_ref, bias_ref, *, topk):
    i = pl.program_id(1)
    n_chunks = (i + 2) // 2
    q_pos = i * Q_BLOCK + lax.broadcasted_iota(jnp.int32, (1, Q_BLOCK), 1)
    k_off = lax.broadcasted_iota(jnp.int32, (KEY_CHUNK, 1), 0)

    qi = qi_ref[0]
    qi_pairs = [_split_heads_t(qi[:, p * LANES:(p + 1) * LANES]) for p in range(IDX_HEADS // 2)]
    w_t = wi_ref[0].T
    w_rows = [w_t[h:h + 1, :] for h in range(IDX_HEADS)]

    def score_chunk(c, carry):
        start = pl.multiple_of(c * KEY_CHUNK, KEY_CHUNK)
        ki = ki_ref[0, pl.ds(start, KEY_CHUNK), :]
        acc = jnp.zeros((KEY_CHUNK, Q_BLOCK), jnp.float32)
        for p in range(IDX_HEADS // 2):
            logit = jnp.dot(ki, qi_pairs[p], preferred_element_type=jnp.float32)
            acc = (acc + w_rows[2 * p] * jnp.maximum(logit[:, :Q_BLOCK], 0.0)
                   + w_rows[2 * p + 1] * jnp.maximum(logit[:, Q_BLOCK:], 0.0))
        sc_ref[c] = jnp.where(start + k_off <= q_pos, acc, -jnp.inf)
        return carry

    lax.fori_loop(0, n_chunks, score_chunk, 0)

    def count(pred):
        def body(c, acc):
            return acc + jnp.where(pred(sc_ref[c]), 1.0, 0.0)
        acc = lax.fori_loop(0, n_chunks, body, jnp.zeros((KEY_CHUNK, Q_BLOCK), jnp.float32))
        return jnp.sum(acc, axis=0, keepdims=True)

    @pl.when((i + 1) * Q_BLOCK <= topk)
    def _():
        bias_ref[0] = jnp.where(k_off <= q_pos, 0.0, MASK_VALUE)

    @pl.when((i + 1) * Q_BLOCK > topk)
    def _():
        def bit_step(b, key):
            cand = key | jnp.left_shift(jnp.int32(1), 31 - b)
            f = _key_to_float(cand)
            total = count(lambda s: s >= f)
            return jnp.where(total >= topk, cand, key)

        key = lax.fori_loop(0, 32, bit_step, jnp.zeros((1, Q_BLOCK), jnp.int32))
        thr = _key_to_float(key)
        need = topk - count(lambda s: s > thr)

        def body(c, run):
            s = sc_ref[c]
            eq = s == thr
            eq_f = jnp.where(eq, 1.0, 0.0)
            incl = jnp.dot(tri_ref[...], eq_f.astype(jnp.bfloat16),
                           preferred_element_type=jnp.float32) + run
            sel = (s > thr) | (eq & (incl <= need))
            bias_ref[c] = jnp.where(sel, 0.0, MASK_VALUE)
            return run + jnp.sum(eq_f, axis=0, keepdims=True)

        lax.fori_loop(0, n_chunks, body, jnp.zeros((1, Q_BLOCK), jnp.float32))

    n_pairs = ATT_HEADS // 2
    q_pairs = [_split_heads_t(qa_ref[0, :, p * LANES:(p + 1) * LANES]) for p in range(n_pairs)]
    half = ATT_HEAD_DIM

    def att_chunk(c, carry):
        start = pl.multiple_of(c * KEY_CHUNK, KEY_CHUNK)
        bias = bias_ref[c]
        bias2 = jnp.concatenate([bias, bias], axis=1)
        pairs = range(n_pairs)
        s_all = [jnp.dot(ka_ref[0, pl.ds(start, KEY_CHUNK), p * LANES:(p + 1) * LANES], q_pairs[p],
                         preferred_element_type=jnp.float32) + bias2 for p in pairs]
        m_new = [jnp.maximum(carry[p][0], jnp.max(s_all[p], axis=0, keepdims=True)) for p in pairs]
        alpha = [jnp.exp(carry[p][0] - m_new[p]) for p in pairs]
        prob = [jnp.exp(s_all[p] - m_new[p]) for p in pairs]
        l_new = [alpha[p] * carry[p][1] + jnp.sum(prob[p], axis=0, keepdims=True) for p in pairs]
        o = [jnp.dot(vt_ref[0, c, p * LANES:(p + 1) * LANES, :], prob[p].astype(jnp.bfloat16),
                     preferred_element_type=jnp.float32) for p in pairs]
        return tuple((m_new[p], l_new[p],
                      alpha[p][:, :Q_BLOCK] * carry[p][2] + o[p][:half, :Q_BLOCK],
                      alpha[p][:, Q_BLOCK:] * carry[p][3] + o[p][half:, Q_BLOCK:]) for p in pairs)

    init = tuple((jnp.full((1, 2 * Q_BLOCK), MASK_VALUE, jnp.float32),
                  jnp.zeros((1, 2 * Q_BLOCK), jnp.float32),
                  jnp.zeros((half, Q_BLOCK), jnp.float32),
                  jnp.zeros((half, Q_BLOCK), jnp.float32)) for _ in range(n_pairs))
    final = lax.fori_loop(0, n_chunks, att_chunk, init)
    for p in range(n_pairs):
        _, l, a0, a1 = final[p]
        psl = slice(p * LANES, (p + 1) * LANES)
        y_t = jnp.concatenate([a0 / l[:, :Q_BLOCK], a1 / l[:, Q_BLOCK:]], axis=0)
        out_ref[0, :, psl] = (y_t.T * ga_ref[0, :, psl].astype(jnp.float32)).astype(out_ref.dtype)


def _dsa(qi, wi, ki, qa, ka, vt, ga, tri, topk):
    b, s, _ = qa.shape
    n_kc = s // KEY_CHUNK
    qblk = lambda width: pl.BlockSpec((1, Q_BLOCK, width), lambda bi, i: (bi, i, 0))
    full = lambda width: pl.BlockSpec((1, s, width), lambda bi, i: (bi, 0, 0))
    return pl.pallas_call(
        functools.partial(_dsa_kernel, topk=topk),
        grid=(b, s // Q_BLOCK),
        in_specs=[qblk(IDX_WIDTH), qblk(LANES), full(LANES), qblk(ATT_WIDTH), full(ATT_WIDTH),
                  pl.BlockSpec((1, n_kc, ATT_WIDTH, KEY_CHUNK), lambda bi, i: (bi, 0, 0, 0)),
                  qblk(ATT_WIDTH),
                  pl.BlockSpec((KEY_CHUNK, KEY_CHUNK), lambda bi, i: (0, 0))],
        out_specs=qblk(ATT_WIDTH),
        out_shape=jax.ShapeDtypeStruct((b, s, ATT_WIDTH), jnp.bfloat16),
        scratch_shapes=[pltpu.VMEM((n_kc, KEY_CHUNK, Q_BLOCK), jnp.float32),
                        pltpu.VMEM((n_kc, KEY_CHUNK, Q_BLOCK), jnp.float32)],
        compiler_params=pltpu.CompilerParams(
            dimension_semantics=("arbitrary", "arbitrary"), vmem_limit_bytes=VMEM_LIMIT_BYTES),
        name="sparse_attention",
    )(qi, wi, ki, qa, ka, vt, ga, tri)


_TN = (((0,), (0,)), ((), ()))


def _retention_kernel(qr_ref, kr_ref, vr_ref, gr_ref, decay_ref, zeta_ref, xi_ref, grow_ref,
                      out_ref, state_ref):
    @pl.when(pl.program_id(1) == 0)
    def _():
        state_ref[...] = jnp.zeros_like(state_ref)

    lane = lax.broadcasted_iota(jnp.int32, (RET_CHUNK, LANES), 1)
    for pair in range(RET_HEADS // 2):
        psl = slice(pair * LANES, (pair + 1) * LANES)
        q_pair = qr_ref[0, :, psl].astype(jnp.float32)
        k_pair = kr_ref[0, :, psl].astype(jnp.float32)
        state = state_ref[psl, :]
        state_b = state.astype(jnp.bfloat16)
        kv = jnp.zeros((LANES, RET_V_DIM), jnp.float32)
        for sub in range(2):
            h = 2 * pair + sub
            vsl = slice(h * RET_V_DIM, (h + 1) * RET_V_DIM)
            in_head = (lane >= sub * RET_QK_DIM) & (lane < (sub + 1) * RET_QK_DIM)
            q_h = jnp.where(in_head, q_pair, 0.0)
            k_h = jnp.where(in_head, k_pair, 0.0)
            v = vr_ref[0, :, vsl]
            scores = lax.dot_general(q_h.astype(jnp.bfloat16), k_h.astype(jnp.bfloat16), _NT,
                                     preferred_element_type=jnp.float32) * decay_ref[h]
            inner = jnp.dot(scores.astype(jnp.bfloat16), v, preferred_element_type=jnp.float32)
            cross = jnp.dot((q_h * xi_ref[h]).astype(jnp.bfloat16), state_b,
                            preferred_element_type=jnp.float32)
            o = inner + cross
            o = o * lax.rsqrt(jnp.mean(o * o, axis=-1, keepdims=True) + EPS)
            out_ref[0, :, vsl] = (o * gr_ref[0, :, vsl].astype(jnp.float32)).astype(out_ref.dtype)
            kv = kv + lax.dot_general((k_h * zeta_ref[h]).astype(jnp.bfloat16), v, _TN,
                                      preferred_element_type=jnp.float32)
        state_ref[psl, :] = state * grow_ref[psl, :] + kv


def _retention(qr, kr, vr, gr, tables):
    b, s, _ = vr.shape
    decay, zeta_b, xi_b, g_rows = tables
    blk = lambda width: pl.BlockSpec((1, RET_CHUNK, width), lambda bi, i: (bi, i, 0))
    const3 = lambda a: pl.BlockSpec(a.shape, lambda bi, i: (0, 0, 0))
    return pl.pallas_call(
        _retention_kernel,
        grid=(b, s // RET_CHUNK),
        in_specs=[blk(RET_QK_WIDTH), blk(RET_QK_WIDTH), blk(RET_V_WIDTH), blk(RET_V_WIDTH),
                  const3(decay), const3(zeta_b), const3(xi_b),
                  pl.BlockSpec(g_rows.shape, lambda bi, i: (0, 0))],
        out_specs=blk(RET_V_WIDTH),
        out_shape=jax.ShapeDtypeStruct((b, s, RET_V_WIDTH), jnp.bfloat16),
        scratch_shapes=[pltpu.VMEM((RET_HEADS * RET_QK_DIM, RET_V_DIM), jnp.float32)],
        compiler_params=pltpu.CompilerParams(
            dimension_semantics=("arbitrary", "arbitrary"), vmem_limit_bytes=VMEM_LIMIT_BYTES),
        name="retention",
    )(qr, kr, vr, gr, decay, zeta_b, xi_b, g_rows)


def _out_proj_kernel(x_ref, ya_ref, yr_ref, wa_ref, wr_ref, gain_ref, out_ref):
    h = (x_ref[...]
         + jnp.dot(ya_ref[...], wa_ref[...], preferred_element_type=jnp.float32)
         + jnp.dot(yr_ref[...], wr_ref[...], preferred_element_type=jnp.float32))
    ms = jnp.mean(h * h, axis=-1, keepdims=True)
    out_ref[...] = (h * lax.rsqrt(ms + EPS)) * gain_ref[...]


def _out_proj(x2, ya2, yr2, wa, wr, gain, tm):
    n, d = x2.shape
    row = lambda width: pl.BlockSpec((tm, width), lambda r: (r, 0))
    const = lambda a: pl.BlockSpec(a.shape, lambda r: (0, 0))
    return pl.pallas_call(
        _out_proj_kernel,
        grid=(n // tm,),
        in_specs=[row(d), row(ATT_WIDTH), row(RET_V_WIDTH), const(wa), const(wr), const(gain)],
        out_specs=row(d),
        out_shape=jax.ShapeDtypeStruct((n, d), jnp.float32),
        compiler_params=pltpu.CompilerParams(
            dimension_semantics=("arbitrary",), vmem_limit_bytes=VMEM_LIMIT_BYTES),
        name="out_proj",
    )(x2, ya2, yr2, wa, wr, gain)


def _reorder_weight(w):
    pts = np.cumsum(SPLITS)[:-1].tolist()
    q_a, k_a, v_a, g_a, q_i, k_i, w_i, q_r, k_r, v_r, g_r = jnp.split(w, pts, axis=1)
    pad = jnp.zeros((w.shape[0], LANES - IDX_HEADS), w.dtype)
    cols = jnp.concatenate([q_a, k_a, v_a, g_a, q_i, q_r, k_r, v_r, g_r, k_i, k_i, w_i, pad], axis=1)
    return cols.astype(jnp.bfloat16)


def kernel(x, norm_gain, w_in, w_out, final_gain):
    b, s, d = x.shape
    depth = norm_gain.shape[0]
    assert d == D_MODEL and s % Q_BLOCK == 0 and w_in.shape[2] == sum(SPLITS)
    topk = min(TOPK_MAX, s // 4)
    rope_a = _rotary_tables(s, ATT_HEAD_DIM, ATT_ROPE_DIM, ROPE_THETA)
    rope_r = _rotary_tables(s, RET_QK_DIM, RET_QK_DIM, RET_THETA)
    ret_tables = _retention_tables()
    tri = (jnp.arange(KEY_CHUNK)[None, :] <= jnp.arange(KEY_CHUNK)[:, None]).astype(jnp.bfloat16)

    assert depth == 1, "the final norm is fused into the single layer's output projection"
    w = _reorder_weight(w_in[0])
    qa, ka, vt, ga, qi, qr, kr, vr, gr, ki, wi = _in_proj(
        x, norm_gain[0][None, :], w, rope_a, rope_r)
    ya = _dsa(qi, wi, ki, qa, ka, vt, ga, tri, topk)
    yr = _retention(qr, kr, vr, gr, ret_tables)
    wo = w_out[0].astype(jnp.bfloat16)
    out = _out_proj(x.reshape(b * s, d), ya.reshape(b * s, ATT_WIDTH),
                    yr.reshape(b * s, RET_V_WIDTH), wo[:ATT_WIDTH], wo[ATT_WIDTH:],
                    final_gain[None, :], 512)
    return out.reshape(b, s, d)
```

```python
import functools

import jax
import jax.numpy as jnp
import numpy as np
from jax import lax
from jax.experimental import pallas as pl
from jax.experimental.pallas import tpu as pltpu

D_MODEL = 1024
ATT_HEADS = 8
ATT_HEAD_DIM = 64
ATT_WIDTH = ATT_HEADS * ATT_HEAD_DIM
ATT_ROPE_DIM = ATT_HEAD_DIM // 4
ROPE_THETA = 500000.0
IDX_HEADS = 4
IDX_DIM = 64
IDX_WIDTH = IDX_HEADS * IDX_DIM
IDX_ROPE_DIM = IDX_DIM // 4
TOPK_MAX = 256
Q_BLOCK = 128
RET_HEADS = 4
RET_QK_DIM = 64
RET_V_DIM = 128
RET_QK_WIDTH = RET_HEADS * RET_QK_DIM
RET_V_WIDTH = RET_HEADS * RET_V_DIM
RET_CHUNK = 128
RET_THETA = 10000.0
MIX_WIDTH = ATT_WIDTH + RET_V_WIDTH
SPLITS = (ATT_WIDTH, ATT_WIDTH, ATT_WIDTH, ATT_WIDTH, IDX_WIDTH, IDX_DIM, IDX_HEADS,
          RET_QK_WIDTH, RET_QK_WIDTH, RET_V_WIDTH, RET_V_WIDTH)
EPS = 1e-6

LANES = 128
SUBLANES = 8
BF16_ROWS = 16
LOG2_E = float(np.log2(np.e))
MASK_VALUE = -1e30
VMEM_LIMIT_BYTES = 48 * 1024 * 1024

OFF_QA = 0
OFF_KA = OFF_QA + ATT_WIDTH
OFF_VA = OFF_KA + ATT_WIDTH
OFF_GA = OFF_VA + ATT_WIDTH
OFF_QI = OFF_GA + ATT_WIDTH
OFF_QR = OFF_QI + IDX_WIDTH
OFF_KR = OFF_QR + RET_QK_WIDTH
OFF_VR = OFF_KR + RET_QK_WIDTH
OFF_GR = OFF_VR + RET_V_WIDTH
OFF_KI = OFF_GR + RET_V_WIDTH
OFF_WI = OFF_KI + LANES
PROJ_WIDTH = OFF_WI + LANES
KEY_CHUNK = 256
COUNT_ROWS = 64


def _rotary_tables(seq, head_dim, rot_dim, theta):
    half = rot_dim // 2
    inv = 1.0 / (theta ** (jnp.arange(half, dtype=jnp.float32) / half))
    ang = jnp.arange(seq).astype(jnp.float32)[:, None] * inv[None, :]
    cos = jnp.cos(ang)
    sin = jnp.sin(ang)
    pad = head_dim - rot_dim
    ones = jnp.ones((seq, pad), jnp.float32)
    zeros = jnp.zeros((seq, pad), jnp.float32)
    zh = jnp.zeros((seq, half), jnp.float32)
    c = jnp.concatenate([cos, cos, ones], axis=1)
    s_prev = jnp.concatenate([zh, sin, zeros], axis=1)
    s_next = jnp.concatenate([-sin, zh, zeros], axis=1)
    reps = LANES // head_dim
    return jnp.stack([jnp.tile(c, (1, reps)), jnp.tile(s_prev, (1, reps)),
                      jnp.tile(s_next, (1, reps))])


def _retention_tables():
    c = RET_CHUNK
    gamma = 1.0 - 2.0 ** (-5.0 - jnp.arange(RET_HEADS, dtype=jnp.float32))
    log_g = jnp.log(gamma)
    idx = jnp.arange(c, dtype=jnp.float32)
    diff = idx[:, None] - idx[None, :]
    decay = jnp.where(diff[None] >= 0,
                      jnp.exp(log_g[:, None, None] * jnp.maximum(diff, 0.0)[None]), 0.0)
    zeta = jnp.exp(log_g[:, None] * (c - 1.0 - idx)[None, :])
    xi = jnp.exp(log_g[:, None] * (idx + 1.0)[None, :])
    g_chunk = jnp.exp(log_g * c)
    zeta_b = jnp.broadcast_to(zeta[:, :, None], (RET_HEADS, c, LANES))
    xi_b = jnp.broadcast_to(xi[:, :, None], (RET_HEADS, c, LANES))
    g_rows = jnp.repeat(g_chunk, RET_QK_DIM)[:, None]
    g_rows = jnp.broadcast_to(g_rows, (RET_HEADS * RET_QK_DIM, RET_V_DIM))
    return decay, zeta_b, xi_b, g_rows


def _rope(z, tab_ref, half):
    return (z * tab_ref[0] + pltpu.roll(z, half, 1) * tab_ref[1]
            + pltpu.roll(z, LANES - half, 1) * tab_ref[2])


def _silu(g):
    return g * (1.0 / (1.0 + jnp.exp(-g)))


def _in_proj_kernel(x_ref, gain_ref, w_ref, rope_a_ref, rope_r_ref,
                    qa_ref, ka_ref, vt_ref, ga_ref, qi_ref, qr_ref, kr_ref, vr_ref, gr_ref,
                    ki_ref, wi_ref, u_ref):
    x = x_ref[0]
    ms = jnp.mean(x * x, axis=-1, keepdims=True)
    u_ref[...] = ((x * lax.rsqrt(ms + EPS)) * gain_ref[...]).astype(jnp.bfloat16)

    def proj(off):
        return jnp.dot(u_ref[...], w_ref[:, off:off + LANES], preferred_element_type=jnp.float32)

    a_half = ATT_ROPE_DIM // 2
    r_half = RET_QK_DIM // 2
    att_scale = ATT_HEAD_DIM ** -0.5 * LOG2_E
    ret_scale = RET_QK_DIM ** -0.5
    for j in range(ATT_WIDTH // LANES):
        sl = slice(j * LANES, (j + 1) * LANES)
        qa_ref[0, :, sl] = (_rope(proj(OFF_QA + j * LANES), rope_a_ref, a_half) * att_scale).astype(qa_ref.dtype)
        ka_ref[0, :, sl] = _rope(proj(OFF_KA + j * LANES), rope_a_ref, a_half).astype(ka_ref.dtype)
        vt_ref[0, 0, sl, :] = proj(OFF_VA + j * LANES).T.astype(vt_ref.dtype)
        ga_ref[0, :, sl] = _silu(proj(OFF_GA + j * LANES)).astype(ga_ref.dtype)
        vr_ref[0, :, sl] = proj(OFF_VR + j * LANES).astype(vr_ref.dtype)
        gr_ref[0, :, sl] = _silu(proj(OFF_GR + j * LANES)).astype(gr_ref.dtype)
    for j in range(IDX_WIDTH // LANES):
        sl = slice(j * LANES, (j + 1) * LANES)
        qi_ref[0, :, sl] = _rope(proj(OFF_QI + j * LANES), rope_a_ref, a_half).astype(qi_ref.dtype)
        qr_ref[0, :, sl] = _rope(proj(OFF_QR + j * LANES), rope_r_ref, r_half).astype(qr_ref.dtype)
        kr_ref[0, :, sl] = (_rope(proj(OFF_KR + j * LANES), rope_r_ref, r_half) * ret_scale).astype(kr_ref.dtype)
    ki_ref[0] = _rope(proj(OFF_KI), rope_a_ref, a_half).astype(ki_ref.dtype)
    wi_ref[0] = proj(OFF_WI) * ((IDX_DIM ** -0.5) * (IDX_HEADS ** -0.5))


def _in_proj(x, gain, w, rope_a, rope_r):
    b, s, d = x.shape
    tm = KEY_CHUNK
    row = lambda width: pl.BlockSpec((1, tm, width), lambda si, bi: (bi, si, 0))
    tab = pl.BlockSpec((3, tm, LANES), lambda si, bi: (0, si, 0))
    vt_spec = pl.BlockSpec((1, 1, ATT_WIDTH, tm), lambda si, bi: (bi, si, 0, 0))
    bf = jnp.bfloat16
    out_shapes = [
        jax.ShapeDtypeStruct((b, s, ATT_WIDTH), bf),
        jax.ShapeDtypeStruct((b, s, ATT_WIDTH), bf),
        jax.ShapeDtypeStruct((b, s // tm, ATT_WIDTH, tm), bf),
        jax.ShapeDtypeStruct((b, s, ATT_WIDTH), bf),
        jax.ShapeDtypeStruct((b, s, IDX_WIDTH), bf),
        jax.ShapeDtypeStruct((b, s, RET_QK_WIDTH), bf),
        jax.ShapeDtypeStruct((b, s, RET_QK_WIDTH), bf),
        jax.ShapeDtypeStruct((b, s, RET_V_WIDTH), bf),
        jax.ShapeDtypeStruct((b, s, RET_V_WIDTH), bf),
        jax.ShapeDtypeStruct((b, s, LANES), bf),
        jax.ShapeDtypeStruct((b, s, LANES), jnp.float32),
    ]
    out_specs = [row(ATT_WIDTH), row(ATT_WIDTH), vt_spec, row(ATT_WIDTH),
                 row(IDX_WIDTH), row(RET_QK_WIDTH), row(RET_QK_WIDTH),
                 row(RET_V_WIDTH), row(RET_V_WIDTH), row(LANES), row(LANES)]
    return pl.pallas_call(
        _in_proj_kernel,
        grid=(s // tm, b),
        in_specs=[row(d),
                  pl.BlockSpec((1, d), lambda si, bi: (0, 0)),
                  pl.BlockSpec((d, PROJ_WIDTH), lambda si, bi: (0, 0)),
                  tab, tab],
        out_specs=out_specs,
        out_shape=out_shapes,
        scratch_shapes=[pltpu.VMEM((tm, d), jnp.bfloat16)],
        compiler_params=pltpu.CompilerParams(
            dimension_semantics=("arbitrary", "arbitrary"), vmem_limit_bytes=VMEM_LIMIT_BYTES),
        name="in_proj",
    )(x, gain, w, rope_a, rope_r)


_NT = (((1,), (1,)), ((), ()))


def _key_to_float(key):
    k = key ^ jnp.int32(-2 ** 31)
    bits = jnp.where(k >= 0, k, k ^ jnp.int32(2 ** 31 - 1))
    return lax.bitcast_convert_type(bits, jnp.float32)


def _split_heads_t(blk):
    blk_t = blk.astype(jnp.float32).T
    feat = lax.broadcasted_iota(jnp.int32, blk_t.shape, 0)
    zero = jnp.zeros_like(blk_t)
    both = jnp.concatenate([jnp.where(feat < LANES // 2, blk_t, zero),
                            jnp.where(feat >= LANES // 2, blk_t, zero)], axis=1)
    return both.astype(jnp.bfloat16)


def _dsa_kernel(qi_ref, wi_ref, ki_ref, qa_ref, ka_ref, vt_ref, ga_ref,
                out_ref, sc_ref, bias_ref, s_a, s_b, p_a, p_b, acc_ref, *, topk):
    i = pl.program_id(1)
    n_kc = sc_ref.shape[0]
    n_chunks = (i + 2) // 2
    q_pos = i * Q_BLOCK + lax.broadcasted_iota(jnp.int32, (1, Q_BLOCK), 1)
    k_off = lax.broadcasted_iota(jnp.int32, (KEY_CHUNK, 1), 0)

    qi = qi_ref[0]
    qi_pairs = [_split_heads_t(qi[:, p * LANES:(p + 1) * LANES]) for p in range(IDX_HEADS // 2)]
    w_t = wi_ref[0].T
    w_rows = [w_t[h:h + 1, :] for h in range(IDX_HEADS)]

    def score_chunk(c, carry):
        start = pl.multiple_of(c * KEY_CHUNK, KEY_CHUNK)
        ki = ki_ref[0, pl.ds(start, KEY_CHUNK), :]
        logits = [jnp.dot(ki, qi_pairs[p], preferred_element_type=jnp.float32)
                  for p in range(IDX_HEADS // 2)]
        acc = w_rows[0] * jnp.maximum(logits[0][:, :Q_BLOCK], 0.0)
        acc = acc + w_rows[1] * jnp.maximum(logits[0][:, Q_BLOCK:], 0.0)
        acc = acc + w_rows[2] * jnp.maximum(logits[1][:, :Q_BLOCK], 0.0)
        acc = acc + w_rows[3] * jnp.maximum(logits[1][:, Q_BLOCK:], 0.0)
        sc_ref[c] = jnp.where(start + k_off <= q_pos, acc, -jnp.inf)
        return carry

    lax.fori_loop(0, n_chunks, score_chunk, 0)

    def count(pred):
        rows = COUNT_ROWS

        def body(c, acc):
            hit = jnp.where(pred(sc_ref[c]), 1.0, 0.0)
            for j in range(KEY_CHUNK // rows):
                acc = acc + hit[j * rows:(j + 1) * rows]
            return acc
        acc = lax.fori_loop(0, n_chunks, body, jnp.zeros((rows, Q_BLOCK), jnp.float32))
        return jnp.sum(acc, axis=0, keepdims=True)

    @pl.when((i + 1) * Q_BLOCK <= topk)
    def _():
        bias_ref[0] = jnp.where(k_off <= q_pos, 0.0, MASK_VALUE)

    @pl.when((i + 1) * Q_BLOCK > topk)
    def _():
        def bit_step(b, key):
            cand = key | jnp.left_shift(jnp.int32(1), 31 - b)
            f = _key_to_float(cand)
            total = count(lambda s: s >= f)
            return jnp.where(total >= topk, cand, key)

        key = lax.fori_loop(0, 32, bit_step, jnp.zeros((1, Q_BLOCK), jnp.int32))
        thr = _key_to_float(key)
        need = topk - count(lambda s: s > thr)

        sub = lax.broadcasted_iota(jnp.int32, (KEY_CHUNK, Q_BLOCK), 0) % SUBLANES
        n_groups = KEY_CHUNK // SUBLANES

        def body(c, run):
            s = sc_ref[c]
            eq = s == thr
            cnt = jnp.where(eq, 1.0, 0.0)
            for k in (1, 2, 4):
                cnt = cnt + jnp.where(sub >= k, pltpu.roll(cnt, k, 0), 0.0)
            cnt3 = cnt.reshape(n_groups, SUBLANES, Q_BLOCK)
            groups = []
            for g in range(n_groups):
                groups.append(cnt3[g] + run)
                run = run + cnt3[g, SUBLANES - 1:SUBLANES, :]
            incl = jnp.concatenate(groups, axis=0)
            sel = (s > thr) | (eq & (incl <= need))
            bias_ref[c] = jnp.where(sel, 0.0, MASK_VALUE)
            return run

        lax.fori_loop(0, n_chunks, body, jnp.zeros((1, Q_BLOCK), jnp.float32))

    n_pairs = ATT_HEADS // 2
    q_pairs = [_split_heads_t(qa_ref[0, :, p * LANES:(p + 1) * LANES]) for p in range(n_pairs)]
    half = ATT_HEAD_DIM

    pairs = range(n_pairs)
    s_bufs = (s_a, s_b)
    p_bufs = (p_a, p_b)
    n_steps = (n_chunks + 1) // 2
    bias_ref[n_chunks] = jnp.full((KEY_CHUNK, Q_BLOCK), MASK_VALUE, jnp.float32)

    def scores_to(c, s_buf):
        start = pl.multiple_of(c * KEY_CHUNK, KEY_CHUNK)
        bias = bias_ref[c]
        bias2 = jnp.concatenate([bias, bias], axis=1)
        cmax = []
        for p in pairs:
            s = jnp.dot(ka_ref[0, pl.ds(start, KEY_CHUNK), p * LANES:(p + 1) * LANES], q_pairs[p],
                        preferred_element_type=jnp.float32) + bias2
            s_buf[p] = s
            cmax.append(jnp.max(s, axis=0, keepdims=True))
        return cmax

    ones_rows = jnp.ones((BF16_ROWS, KEY_CHUNK), jnp.bfloat16)

    def weighted_values(c, p_buf):
        return [jnp.dot(jnp.concatenate([vt_ref[0, c, p * LANES:(p + 1) * LANES, :], ones_rows], axis=0),
                        p_buf[p], preferred_element_type=jnp.float32) for p in pairs]

    def accumulate(l, alpha, o):
        l_new = []
        for p in pairs:
            acc_ref[p, 0] = alpha[p][:, :Q_BLOCK] * acc_ref[p, 0] + o[p][:half, :Q_BLOCK]
            acc_ref[p, 1] = alpha[p][:, Q_BLOCK:] * acc_ref[p, 1] + o[p][half:2 * half, Q_BLOCK:]
            l_new.append(alpha[p] * l[p] + o[p][2 * half:2 * half + 1, :])
        return l_new

    def softmax_to(s_buf, p_buf, cmax, m):
        m_new = [jnp.maximum(m[p], cmax[p]) for p in pairs]
        alpha = [jnp.exp2(m[p] - m_new[p]) for p in pairs]
        for p in pairs:
            p_buf[p] = jnp.exp2(s_buf[p] - m_new[p]).astype(jnp.bfloat16)
        return alpha, m_new

    def stage(c, cur, carry):
        cmax, alpha_prev, m, l = carry
        other = 1 - cur
        o_prev = weighted_values(jnp.maximum(c - 1, 0), p_bufs[other])
        cmax_next = scores_to(jnp.minimum(c + 1, n_kc - 1), s_bufs[other])
        alpha, m = softmax_to(s_bufs[cur], p_bufs[cur], cmax, m)
        return cmax_next, alpha, m, accumulate(l, alpha_prev, o_prev)

    def att_step(j, carry):
        return stage(2 * j + 1, 1, stage(2 * j, 0, carry))

    zeros_row = jnp.zeros((1, 2 * Q_BLOCK), jnp.float32)
    p_b[...] = jnp.zeros(p_b.shape, p_b.dtype)
    acc_ref[...] = jnp.zeros(acc_ref.shape, acc_ref.dtype)
    carry = (scores_to(0, s_a), [zeros_row + 1.0 for _ in pairs], [zeros_row + MASK_VALUE for _ in pairs],
             [zeros_row for _ in pairs])
    _, alpha_last, _, l = lax.fori_loop(0, n_steps, att_step, carry)
    l = accumulate(l, alpha_last, weighted_values(2 * n_steps - 1, p_b))
    for p in pairs:
        a0, a1 = acc_ref[p, 0], acc_ref[p, 1]
        psl = slice(p * LANES, (p + 1) * LANES)
        y_t = jnp.concatenate([a0 / l[p][:, :Q_BLOCK], a1 / l[p][:, Q_BLOCK:]], axis=0)
        out_ref[0, :, psl] = (y_t.T * ga_ref[0, :, psl].astype(jnp.float32)).astype(out_ref.dtype)


def _dsa(qi, wi, ki, qa, ka, vt, ga, topk):
    b, s, _ = qa.shape
    n_kc = s // KEY_CHUNK
    att_buf = (ATT_HEADS // 2, KEY_CHUNK, 2 * Q_BLOCK)
    qblk = lambda width: pl.BlockSpec((1, Q_BLOCK, width), lambda bi, i: (bi, i, 0))
    full = lambda width: pl.BlockSpec((1, s, width), lambda bi, i: (bi, 0, 0))
    return pl.pallas_call(
        functools.partial(_dsa_kernel, topk=topk),
        grid=(b, s // Q_BLOCK),
        in_specs=[qblk(IDX_WIDTH), qblk(LANES), full(LANES), qblk(ATT_WIDTH), full(ATT_WIDTH),
                  pl.BlockSpec((1, n_kc, ATT_WIDTH, KEY_CHUNK), lambda bi, i: (bi, 0, 0, 0)),
                  qblk(ATT_WIDTH)],
        out_specs=qblk(ATT_WIDTH),
        out_shape=jax.ShapeDtypeStruct((b, s, ATT_WIDTH), jnp.bfloat16),
        scratch_shapes=[pltpu.VMEM((n_kc, KEY_CHUNK, Q_BLOCK), jnp.float32),
                        pltpu.VMEM((n_kc + 1, KEY_CHUNK, Q_BLOCK), jnp.float32),
                        pltpu.VMEM(att_buf, jnp.float32), pltpu.VMEM(att_buf, jnp.float32),
                        pltpu.VMEM(att_buf, jnp.bfloat16), pltpu.VMEM(att_buf, jnp.bfloat16),
                        pltpu.VMEM((ATT_HEADS // 2, 2, ATT_HEAD_DIM, Q_BLOCK), jnp.float32)],
        compiler_params=pltpu.CompilerParams(
            dimension_semantics=("arbitrary", "arbitrary"), vmem_limit_bytes=VMEM_LIMIT_BYTES),
        name="sparse_attention",
    )(qi, wi, ki, qa, ka, vt, ga)


_TN = (((0,), (0,)), ((), ()))


def _retention_kernel(qr_ref, kr_ref, vr_ref, gr_ref, decay_ref, zeta_ref, xi_ref, grow_ref,
                      out_ref, state_ref):
    @pl.when(pl.program_id(1) == 0)
    def _():
        state_ref[...] = jnp.zeros_like(state_ref)

    lane = lax.broadcasted_iota(jnp.int32, (RET_CHUNK, LANES), 1)
    for pair in range(RET_HEADS // 2):
        psl = slice(pair * LANES, (pair + 1) * LANES)
        q_pair = qr_ref[0, :, psl].astype(jnp.float32)
        k_pair = kr_ref[0, :, psl].astype(jnp.float32)
        state = state_ref[psl, :]
        state_b = state.astype(jnp.bfloat16)
        kv = jnp.zeros((LANES, RET_V_DIM), jnp.float32)
        for sub in range(2):
            h = 2 * pair + sub
            vsl = slice(h * RET_V_DIM, (h + 1) * RET_V_DIM)
            in_head = (lane >= sub * RET_QK_DIM) & (lane < (sub + 1) * RET_QK_DIM)
            q_h = jnp.where(in_head, q_pair, 0.0)
            k_h = jnp.where(in_head, k_pair, 0.0)
            v = vr_ref[0, :, vsl]
            scores = lax.dot_general(q_h.astype(jnp.bfloat16), k_h.astype(jnp.bfloat16), _NT,
                                     preferred_element_type=jnp.float32) * decay_ref[h]
            inner = jnp.dot(scores.astype(jnp.bfloat16), v, preferred_element_type=jnp.float32)
            cross = jnp.dot((q_h * xi_ref[h]).astype(jnp.bfloat16), state_b,
                            preferred_element_type=jnp.float32)
            o = inner + cross
            o = o * lax.rsqrt(jnp.mean(o * o, axis=-1, keepdims=True) + EPS)
            out_ref[0, :, vsl] = (o * gr_ref[0, :, vsl].astype(jnp.float32)).astype(out_ref.dtype)
            kv = kv + lax.dot_general((k_h * zeta_ref[h]).astype(jnp.bfloat16), v, _TN,
                                      preferred_element_type=jnp.float32)
        state_ref[psl, :] = state * grow_ref[psl, :] + kv


def _retention(qr, kr, vr, gr, tables):
    b, s, _ = vr.shape
    decay, zeta_b, xi_b, g_rows = tables
    blk = lambda width: pl.BlockSpec((1, RET_CHUNK, width), lambda bi, i: (bi, i, 0))
    const3 = lambda a: pl.BlockSpec(a.shape, lambda bi, i: (0, 0, 0))
    return pl.pallas_call(
        _retention_kernel,
        grid=(b, s // RET_CHUNK),
        in_specs=[blk(RET_QK_WIDTH), blk(RET_QK_WIDTH), blk(RET_V_WIDTH), blk(RET_V_WIDTH),
                  const3(decay), const3(zeta_b), const3(xi_b),
                  pl.BlockSpec(g_rows.shape, lambda bi, i: (0, 0))],
        out_specs=blk(RET_V_WIDTH),
        out_shape=jax.ShapeDtypeStruct((b, s, RET_V_WIDTH), jnp.bfloat16),
        scratch_shapes=[pltpu.VMEM((RET_HEADS * RET_QK_DIM, RET_V_DIM), jnp.float32)],
        compiler_params=pltpu.CompilerParams(
            dimension_semantics=("arbitrary", "arbitrary"), vmem_limit_bytes=VMEM_LIMIT_BYTES),
        name="retention",
    )(qr, kr, vr, gr, decay, zeta_b, xi_b, g_rows)


def _out_proj_kernel(x_ref, ya_ref, yr_ref, wa_ref, wr_ref, gain_ref, out_ref):
    h = (x_ref[...]
         + jnp.dot(ya_ref[...], wa_ref[...], preferred_element_type=jnp.float32)
         + jnp.dot(yr_ref[...], wr_ref[...], preferred_element_type=jnp.float32))
    ms = jnp.mean(h * h, axis=-1, keepdims=True)
    out_ref[...] = (h * lax.rsqrt(ms + EPS)) * gain_ref[...]


def _out_proj(x2, ya2, yr2, wa, wr, gain, tm):
    n, d = x2.shape
    row = lambda width: pl.BlockSpec((tm, width), lambda r: (r, 0))
    const = lambda a: pl.BlockSpec(a.shape, lambda r: (0, 0))
    return pl.pallas_call(
        _out_proj_kernel,
        grid=(n // tm,),
        in_specs=[row(d), row(ATT_WIDTH), row(RET_V_WIDTH), const(wa), const(wr), const(gain)],
        out_specs=row(d),
        out_shape=jax.ShapeDtypeStruct((n, d), jnp.float32),
        compiler_params=pltpu.CompilerParams(
            dimension_semantics=("arbitrary",), vmem_limit_bytes=VMEM_LIMIT_BYTES),
        name="out_proj",
    )(x2, ya2, yr2, wa, wr, gain)


def _reorder_weight(w):
    pts = np.cumsum(SPLITS)[:-1].tolist()
    q_a, k_a, v_a, g_a, q_i, k_i, w_i, q_r, k_r, v_r, g_r = jnp.split(w, pts, axis=1)
    pad = jnp.zeros((w.shape[0], LANES - IDX_HEADS), w.dtype)
    cols = jnp.concatenate([q_a, k_a, v_a, g_a, q_i, q_r, k_r, v_r, g_r, k_i, k_i, w_i, pad], axis=1)
    return cols.astype(jnp.bfloat16)


def kernel(x, norm_gain, w_in, w_out, final_gain):
    b, s, d = x.shape
    depth = norm_gain.shape[0]
    assert d == D_MODEL and s % Q_BLOCK == 0 and w_in.shape[2] == sum(SPLITS)
    topk = min(TOPK_MAX, s // 4)
    rope_a = _rotary_tables(s, ATT_HEAD_DIM, ATT_ROPE_DIM, ROPE_THETA)
    rope_r = _rotary_tables(s, RET_QK_DIM, RET_QK_DIM, RET_THETA)
    ret_tables = _retention_tables()
    assert depth == 1, "the final norm is fused into the single layer's output projection"
    w = _reorder_weight(w_in[0])
    qa, ka, vt, ga, qi, qr, kr, vr, gr, ki, wi = _in_proj(
        x, norm_gain[0][None, :], w, rope_a, rope_r)
    ya = _dsa(qi, wi, ki, qa, ka, vt, ga, topk)
    yr = _retention(qr, kr, vr, gr, ret_tables)
    wo = w_out[0].astype(jnp.bfloat16)
    out = _out_proj(x.reshape(b * s, d), ya.reshape(b * s, ATT_WIDTH),
                    yr.reshape(b * s, RET_V_WIDTH), wo[:ATT_WIDTH], wo[ATT_WIDTH:],
                    final_gain[None, :], 512)
    return out.reshape(b, s, d)
```

```python
import functools

import jax
import jax.numpy as jnp
import numpy as np
from jax import lax
from jax.experimental import pallas as pl
from jax.experimental.pallas import tpu as pltpu

D_MODEL = 1024
ATT_HEADS = 8
ATT_HEAD_DIM = 64
ATT_WIDTH = ATT_HEADS * ATT_HEAD_DIM
ATT_ROPE_DIM = ATT_HEAD_DIM // 4
ROPE_THETA = 500000.0
IDX_HEADS = 4
IDX_DIM = 64
IDX_WIDTH = IDX_HEADS * IDX_DIM
IDX_ROPE_DIM = IDX_DIM // 4
TOPK_MAX = 256
Q_BLOCK = 128
RET_HEADS = 4
RET_QK_DIM = 64
RET_V_DIM = 128
RET_QK_WIDTH = RET_HEADS * RET_QK_DIM
RET_V_WIDTH = RET_HEADS * RET_V_DIM
RET_CHUNK = 128
RET_THETA = 10000.0
MIX_WIDTH = ATT_WIDTH + RET_V_WIDTH
SPLITS = (ATT_WIDTH, ATT_WIDTH, ATT_WIDTH, ATT_WIDTH, IDX_WIDTH, IDX_DIM, IDX_HEADS,
          RET_QK_WIDTH, RET_QK_WIDTH, RET_V_WIDTH, RET_V_WIDTH)
EPS = 1e-6

LANES = 128
SUBLANES = 8
BF16_ROWS = 16
MXU_COLS = 256
LOG2_E = float(np.log2(np.e))
MASK_VALUE = -1e30
VMEM_LIMIT_BYTES = 48 * 1024 * 1024

OFF_QA = 0
OFF_KA = OFF_QA + ATT_WIDTH
OFF_VA = OFF_KA + ATT_WIDTH
OFF_GA = OFF_VA + ATT_WIDTH
OFF_QI = OFF_GA + ATT_WIDTH
OFF_QR = OFF_QI + IDX_WIDTH
OFF_KR = OFF_QR + RET_QK_WIDTH
OFF_VR = OFF_KR + RET_QK_WIDTH
OFF_GR = OFF_VR + RET_V_WIDTH
OFF_KI = OFF_GR + RET_V_WIDTH
OFF_WI = OFF_KI + LANES
PROJ_WIDTH = OFF_WI + LANES
KEY_CHUNK = 256
COUNT_ROWS = 64


def _rotary_tables(seq, head_dim, rot_dim, theta):
    half = rot_dim // 2
    inv = 1.0 / (theta ** (jnp.arange(half, dtype=jnp.float32) / half))
    ang = jnp.arange(seq).astype(jnp.float32)[:, None] * inv[None, :]
    cos = jnp.cos(ang)
    sin = jnp.sin(ang)
    pad = head_dim - rot_dim
    ones = jnp.ones((seq, pad), jnp.float32)
    zeros = jnp.zeros((seq, pad), jnp.float32)
    zh = jnp.zeros((seq, half), jnp.float32)
    c = jnp.concatenate([cos, cos, ones], axis=1)
    s_prev = jnp.concatenate([zh, sin, zeros], axis=1)
    s_next = jnp.concatenate([-sin, zh, zeros], axis=1)
    reps = LANES // head_dim
    return jnp.stack([jnp.tile(c, (1, reps)), jnp.tile(s_prev, (1, reps)),
                      jnp.tile(s_next, (1, reps))])


def _retention_tables():
    c = RET_CHUNK
    gamma = 1.0 - 2.0 ** (-5.0 - jnp.arange(RET_HEADS, dtype=jnp.float32))
    log_g = jnp.log(gamma)
    idx = jnp.arange(c, dtype=jnp.float32)
    diff = idx[:, None] - idx[None, :]
    decay = jnp.where(diff[None] >= 0,
                      jnp.exp(log_g[:, None, None] * jnp.maximum(diff, 0.0)[None]), 0.0)
    zeta = jnp.exp(log_g[:, None] * (c - 1.0 - idx)[None, :])
    xi = jnp.exp(log_g[:, None] * (idx + 1.0)[None, :])
    g_chunk = jnp.exp(log_g * c)
    zeta_b = jnp.broadcast_to(zeta[:, :, None], (RET_HEADS, c, LANES))
    xi_b = jnp.broadcast_to(xi[:, :, None], (RET_HEADS, c, LANES))
    g_rows = jnp.repeat(g_chunk, RET_QK_DIM)[:, None]
    g_rows = jnp.broadcast_to(g_rows, (RET_HEADS * RET_QK_DIM, RET_V_DIM))
    return decay, zeta_b, xi_b, g_rows


def _rope(z, tab_ref, half):
    return (z * tab_ref[0] + pltpu.roll(z, half, 1) * tab_ref[1]
            + pltpu.roll(z, LANES - half, 1) * tab_ref[2])


def _silu(g):
    return g * (1.0 / (1.0 + jnp.exp(-g)))


def _in_proj_kernel(x_ref, gain_ref, w_ref, rope_a_ref, rope_r_ref,
                    qa_ref, ka_ref, vt_ref, ga_ref, qi_ref, qr_ref, kr_ref, vr_ref, gr_ref,
                    ki_ref, wi_ref, u_ref):
    x = x_ref[0]
    ms = jnp.mean(x * x, axis=-1, keepdims=True)
    u_ref[...] = ((x * lax.rsqrt(ms + EPS)) * gain_ref[...]).astype(jnp.bfloat16)

    def proj(off, width):
        out = []
        for j in range(width // MXU_COLS):
            c0 = off + j * MXU_COLS
            z = jnp.dot(u_ref[...], w_ref[:, c0:c0 + MXU_COLS], preferred_element_type=jnp.float32)
            for h in range(MXU_COLS // LANES):
                lo = j * MXU_COLS + h * LANES
                out.append((slice(lo, lo + LANES), z[:, h * LANES:(h + 1) * LANES]))
        return out

    a_half = ATT_ROPE_DIM // 2
    r_half = RET_QK_DIM // 2
    att_scale = ATT_HEAD_DIM ** -0.5 * LOG2_E
    ret_scale = RET_QK_DIM ** -0.5
    for sl, z in proj(OFF_QA, ATT_WIDTH):
        qa_ref[0, :, sl] = (_rope(z, rope_a_ref, a_half) * att_scale).astype(qa_ref.dtype)
    for sl, z in proj(OFF_KA, ATT_WIDTH):
        ka_ref[0, :, sl] = _rope(z, rope_a_ref, a_half).astype(ka_ref.dtype)
    for sl, z in proj(OFF_VA, ATT_WIDTH):
        vt_ref[0, 0, sl, :] = z.T.astype(vt_ref.dtype)
    for sl, z in proj(OFF_GA, ATT_WIDTH):
        ga_ref[0, :, sl] = _silu(z).astype(ga_ref.dtype)
    for sl, z in proj(OFF_QI, IDX_WIDTH):
        qi_ref[0, :, sl] = _rope(z, rope_a_ref, a_half).astype(qi_ref.dtype)
    for sl, z in proj(OFF_QR, RET_QK_WIDTH):
        qr_ref[0, :, sl] = _rope(z, rope_r_ref, r_half).astype(qr_ref.dtype)
    for sl, z in proj(OFF_KR, RET_QK_WIDTH):
        kr_ref[0, :, sl] = (_rope(z, rope_r_ref, r_half) * ret_scale).astype(kr_ref.dtype)
    for sl, z in proj(OFF_VR, RET_V_WIDTH):
        vr_ref[0, :, sl] = z.astype(vr_ref.dtype)
    for sl, z in proj(OFF_GR, RET_V_WIDTH):
        gr_ref[0, :, sl] = _silu(z).astype(gr_ref.dtype)
    (_, ki), (_, wi) = proj(OFF_KI, 2 * LANES)
    ki_ref[0] = _rope(ki, rope_a_ref, a_half).astype(ki_ref.dtype)
    wi_ref[0] = wi * ((IDX_DIM ** -0.5) * (IDX_HEADS ** -0.5))


def _in_proj(x, gain, w, rope_a, rope_r):
    b, s, d = x.shape
    tm = KEY_CHUNK
    row = lambda width: pl.BlockSpec((1, tm, width), lambda si, bi: (bi, si, 0))
    tab = pl.BlockSpec((3, tm, LANES), lambda si, bi: (0, si, 0))
    vt_spec = pl.BlockSpec((1, 1, ATT_WIDTH, tm), lambda si, bi: (bi, si, 0, 0))
    bf = jnp.bfloat16
    out_shapes = [
        jax.ShapeDtypeStruct((b, s, ATT_WIDTH), bf),
        jax.ShapeDtypeStruct((b, s, ATT_WIDTH), bf),
        jax.ShapeDtypeStruct((b, s // tm, ATT_WIDTH, tm), bf),
        jax.ShapeDtypeStruct((b, s, ATT_WIDTH), bf),
        jax.ShapeDtypeStruct((b, s, IDX_WIDTH), bf),
        jax.ShapeDtypeStruct((b, s, RET_QK_WIDTH), bf),
        jax.ShapeDtypeStruct((b, s, RET_QK_WIDTH), bf),
        jax.ShapeDtypeStruct((b, s, RET_V_WIDTH), bf),
        jax.ShapeDtypeStruct((b, s, RET_V_WIDTH), bf),
        jax.ShapeDtypeStruct((b, s, LANES), bf),
        jax.ShapeDtypeStruct((b, s, LANES), jnp.float32),
    ]
    out_specs = [row(ATT_WIDTH), row(ATT_WIDTH), vt_spec, row(ATT_WIDTH),
                 row(IDX_WIDTH), row(RET_QK_WIDTH), row(RET_QK_WIDTH),
                 row(RET_V_WIDTH), row(RET_V_WIDTH), row(LANES), row(LANES)]
    return pl.pallas_call(
        _in_proj_kernel,
        grid=(s // tm, b),
        in_specs=[row(d),
                  pl.BlockSpec((1, d), lambda si, bi: (0, 0)),
                  pl.BlockSpec((d, PROJ_WIDTH), lambda si, bi: (0, 0)),
                  tab, tab],
        out_specs=out_specs,
        out_shape=out_shapes,
        scratch_shapes=[pltpu.VMEM((tm, d), jnp.bfloat16)],
        compiler_params=pltpu.CompilerParams(
            dimension_semantics=("arbitrary", "arbitrary"), vmem_limit_bytes=VMEM_LIMIT_BYTES),
        name="in_proj",
    )(x, gain, w, rope_a, rope_r)


_NT = (((1,), (1,)), ((), ()))


def _key_to_float(key):
    k = key ^ jnp.int32(-2 ** 31)
    bits = jnp.where(k >= 0, k, k ^ jnp.int32(2 ** 31 - 1))
    return lax.bitcast_convert_type(bits, jnp.float32)


def _split_heads_t(blk):
    blk_t = blk.astype(jnp.float32).T
    feat = lax.broadcasted_iota(jnp.int32, blk_t.shape, 0)
    zero = jnp.zeros_like(blk_t)
    both = jnp.concatenate([jnp.where(feat < LANES // 2, blk_t, zero),
                            jnp.where(feat >= LANES // 2, blk_t, zero)], axis=1)
    return both.astype(jnp.bfloat16)


def _dsa_kernel(qi_ref, wi_ref, ki_ref, qa_ref, ka_ref, vt_ref, ga_ref,
                out_ref, sc_ref, bias_ref, s_a, s_b, p_a, p_b, acc_ref, *, topk):
    i = pl.program_id(1)
    n_kc = sc_ref.shape[0]
    n_chunks = (i + 2) // 2
    q_pos = i * Q_BLOCK + lax.broadcasted_iota(jnp.int32, (1, Q_BLOCK), 1)
    k_off = lax.broadcasted_iota(jnp.int32, (KEY_CHUNK, 1), 0)

    qi = qi_ref[0]
    qi_pairs = [_split_heads_t(qi[:, p * LANES:(p + 1) * LANES]) for p in range(IDX_HEADS // 2)]
    w_t = wi_ref[0].T
    w_rows = [w_t[h:h + 1, :] for h in range(IDX_HEADS)]

    s_bufs = (s_a, s_b)
    n_steps = (n_chunks + 1) // 2

    def logits_to(c, buf):
        start = pl.multiple_of(c * KEY_CHUNK, KEY_CHUNK)
        ki = ki_ref[0, pl.ds(start, KEY_CHUNK), :]
        for p in range(IDX_HEADS // 2):
            buf[p] = jnp.dot(ki, qi_pairs[p], preferred_element_type=jnp.float32)

    def score_stage(c, cur):
        logits_to(jnp.minimum(c + 1, n_kc - 1), s_bufs[1 - cur])
        buf = s_bufs[cur]
        acc = None
        for h in range(IDX_HEADS):
            lanes = slice((h % 2) * Q_BLOCK, (h % 2 + 1) * Q_BLOCK)
            term = w_rows[h] * jnp.maximum(buf[h // 2, :, lanes], 0.0)
            acc = term if acc is None else acc + term
        sc_ref[c] = jnp.where(c * KEY_CHUNK + k_off <= q_pos, acc, -jnp.inf)

    def score_step(j, carry):
        score_stage(2 * j, 0)
        score_stage(2 * j + 1, 1)
        return carry

    logits_to(0, s_a)
    lax.fori_loop(0, n_steps, score_step, 0)

    def count(pred):
        rows = COUNT_ROWS

        def body(c, acc):
            hit = jnp.where(pred(sc_ref[c]), 1.0, 0.0)
            for j in range(KEY_CHUNK // rows):
                acc = acc + hit[j * rows:(j + 1) * rows]
            return acc
        acc = lax.fori_loop(0, n_chunks, body, jnp.zeros((rows, Q_BLOCK), jnp.float32))
        return jnp.sum(acc, axis=0, keepdims=True)

    @pl.when((i + 1) * Q_BLOCK <= topk)
    def _():
        bias_ref[0] = jnp.where(k_off <= q_pos, 0.0, MASK_VALUE)

    @pl.when((i + 1) * Q_BLOCK > topk)
    def _():
        def bit_step(b, key):
            cand = key | jnp.left_shift(jnp.int32(1), 31 - b)
            f = _key_to_float(cand)
            total = count(lambda s: s >= f)
            return jnp.where(total >= topk, cand, key)

        key = lax.fori_loop(0, 32, bit_step, jnp.zeros((1, Q_BLOCK), jnp.int32))
        thr = _key_to_float(key)
        need = topk - count(lambda s: s > thr)

        sub = lax.broadcasted_iota(jnp.int32, (KEY_CHUNK, Q_BLOCK), 0) % SUBLANES
        n_groups = KEY_CHUNK // SUBLANES

        def body(c, run):
            s = sc_ref[c]
            eq = s == thr
            cnt = jnp.where(eq, 1.0, 0.0)
            for k in (1, 2, 4):
                cnt = cnt + jnp.where(sub >= k, pltpu.roll(cnt, k, 0), 0.0)
            cnt3 = cnt.reshape(n_groups, SUBLANES, Q_BLOCK)
            groups = []
            for g in range(n_groups):
                groups.append(cnt3[g] + run)
                run = run + cnt3[g, SUBLANES - 1:SUBLANES, :]
            incl = jnp.concatenate(groups, axis=0)
            sel = (s > thr) | (eq & (incl <= need))
            bias_ref[c] = jnp.where(sel, 0.0, MASK_VALUE)
            return run

        lax.fori_loop(0, n_chunks, body, jnp.zeros((1, Q_BLOCK), jnp.float32))

    n_pairs = ATT_HEADS // 2
    q_pairs = [_split_heads_t(qa_ref[0, :, p * LANES:(p + 1) * LANES]) for p in range(n_pairs)]
    half = ATT_HEAD_DIM

    pairs = range(n_pairs)
    p_bufs = (p_a, p_b)
    bias_ref[n_chunks] = jnp.full((KEY_CHUNK, Q_BLOCK), MASK_VALUE, jnp.float32)

    def scores_to(c, s_buf):
        start = pl.multiple_of(c * KEY_CHUNK, KEY_CHUNK)
        bias = bias_ref[c]
        bias2 = jnp.concatenate([bias, bias], axis=1)
        cmax = []
        for p in pairs:
            s = jnp.dot(ka_ref[0, pl.ds(start, KEY_CHUNK), p * LANES:(p + 1) * LANES], q_pairs[p],
                        preferred_element_type=jnp.float32) + bias2
            s_buf[p] = s
            cmax.append(jnp.max(s, axis=0, keepdims=True))
        return cmax

    ones_rows = jnp.ones((BF16_ROWS, KEY_CHUNK), jnp.bfloat16)

    def weighted_values(c, p_buf):
        return [jnp.dot(jnp.concatenate([vt_ref[0, c, p * LANES:(p + 1) * LANES, :], ones_rows], axis=0),
                        p_buf[p], preferred_element_type=jnp.float32) for p in pairs]

    def accumulate(l, alpha, o):
        l_new = []
        for p in pairs:
            acc_ref[p, 0] = alpha[p][:, :Q_BLOCK] * acc_ref[p, 0] + o[p][:half, :Q_BLOCK]
            acc_ref[p, 1] = alpha[p][:, Q_BLOCK:] * acc_ref[p, 1] + o[p][half:2 * half, Q_BLOCK:]
            l_new.append(alpha[p] * l[p] + o[p][2 * half:2 * half + 1, :])
        return l_new

    def softmax_to(s_buf, p_buf, cmax, m):
        m_new = [jnp.maximum(m[p], cmax[p]) for p in pairs]
        alpha = [jnp.exp2(m[p] - m_new[p]) for p in pairs]
        for p in pairs:
            p_buf[p] = jnp.exp2(s_buf[p] - m_new[p]).astype(jnp.bfloat16)
        return alpha, m_new

    def stage(c, cur, carry):
        cmax, alpha_prev, m, l = carry
        other = 1 - cur
        o_prev = weighted_values(jnp.maximum(c - 1, 0), p_bufs[other])
        cmax_next = scores_to(jnp.minimum(c + 1, n_kc - 1), s_bufs[other])
        alpha, m = softmax_to(s_bufs[cur], p_bufs[cur], cmax, m)
        return cmax_next, alpha, m, accumulate(l, alpha_prev, o_prev)

    def att_step(j, carry):
        return stage(2 * j + 1, 1, stage(2 * j, 0, carry))

    zeros_row = jnp.zeros((1, 2 * Q_BLOCK), jnp.float32)
    p_b[...] = jnp.zeros(p_b.shape, p_b.dtype)
    acc_ref[...] = jnp.zeros(acc_ref.shape, acc_ref.dtype)
    carry = (scores_to(0, s_a), [zeros_row + 1.0 for _ in pairs], [zeros_row + MASK_VALUE for _ in pairs],
             [zeros_row for _ in pairs])
    _, alpha_last, _, l = lax.fori_loop(0, n_steps, att_step, carry)
    l = accumulate(l, alpha_last, weighted_values(2 * n_steps - 1, p_b))
    for p in pairs:
        a0, a1 = acc_ref[p, 0], acc_ref[p, 1]
        psl = slice(p * LANES, (p + 1) * LANES)
        y_t = jnp.concatenate([a0 / l[p][:, :Q_BLOCK], a1 / l[p][:, Q_BLOCK:]], axis=0)
        out_ref[0, :, psl] = (y_t.T * ga_ref[0, :, psl].astype(jnp.float32)).astype(out_ref.dtype)


def _dsa(qi, wi, ki, qa, ka, vt, ga, topk):
    b, s, _ = qa.shape
    n_kc = s // KEY_CHUNK
    att_buf = (ATT_HEADS // 2, KEY_CHUNK, 2 * Q_BLOCK)
    qblk = lambda width: pl.BlockSpec((1, Q_BLOCK, width), lambda bi, i: (bi, i, 0))
    full = lambda width: pl.BlockSpec((1, s, width), lambda bi, i: (bi, 0, 0))
    return pl.pallas_call(
        functools.partial(_dsa_kernel, topk=topk),
        grid=(b, s // Q_BLOCK),
        in_specs=[qblk(IDX_WIDTH), qblk(LANES), full(LANES), qblk(ATT_WIDTH), full(ATT_WIDTH),
                  pl.BlockSpec((1, n_kc, ATT_WIDTH, KEY_CHUNK), lambda bi, i: (bi, 0, 0, 0)),
                  qblk(ATT_WIDTH)],
        out_specs=qblk(ATT_WIDTH),
        out_shape=jax.ShapeDtypeStruct((b, s, ATT_WIDTH), jnp.bfloat16),
        scratch_shapes=[pltpu.VMEM((n_kc, KEY_CHUNK, Q_BLOCK), jnp.float32),
                        pltpu.VMEM((n_kc + 1, KEY_CHUNK, Q_BLOCK), jnp.float32),
                        pltpu.VMEM(att_buf, jnp.float32), pltpu.VMEM(att_buf, jnp.float32),
                        pltpu.VMEM(att_buf, jnp.bfloat16), pltpu.VMEM(att_buf, jnp.bfloat16),
                        pltpu.VMEM((ATT_HEADS // 2, 2, ATT_HEAD_DIM, Q_BLOCK), jnp.float32)],
        compiler_params=pltpu.CompilerParams(
            dimension_semantics=("arbitrary", "arbitrary"), vmem_limit_bytes=VMEM_LIMIT_BYTES),
        name="sparse_attention",
    )(qi, wi, ki, qa, ka, vt, ga)


_TN = (((0,), (0,)), ((), ()))


def _retention_kernel(qr_ref, kr_ref, vr_ref, gr_ref, decay_ref, zeta_ref, xi_ref, grow_ref,
                      out_ref, state_ref):
    @pl.when(pl.program_id(1) == 0)
    def _():
        state_ref[...] = jnp.zeros_like(state_ref)

    lane = lax.broadcasted_iota(jnp.int32, (RET_CHUNK, LANES), 1)
    for pair in range(RET_HEADS // 2):
        psl = slice(pair * LANES, (pair + 1) * LANES)
        q_pair = qr_ref[0, :, psl].astype(jnp.float32)
        k_pair = kr_ref[0, :, psl].astype(jnp.float32)
        state = state_ref[psl, :]
        state_b = state.astype(jnp.bfloat16)
        kv = jnp.zeros((LANES, RET_V_DIM), jnp.float32)
        for sub in range(2):
            h = 2 * pair + sub
            vsl = slice(h * RET_V_DIM, (h + 1) * RET_V_DIM)
            in_head = (lane >= sub * RET_QK_DIM) & (lane < (sub + 1) * RET_QK_DIM)
            q_h = jnp.where(in_head, q_pair, 0.0)
            k_h = jnp.where(in_head, k_pair, 0.0)
            v = vr_ref[0, :, vsl]
            scores = lax.dot_general(q_h.astype(jnp.bfloat16), k_h.astype(jnp.bfloat16), _NT,
                                     preferred_element_type=jnp.float32) * decay_ref[h]
            inner = jnp.dot(scores.astype(jnp.bfloat16), v, preferred_element_type=jnp.float32)
            cross = jnp.dot((q_h * xi_ref[h]).astype(jnp.bfloat16), state_b,
                            preferred_element_type=jnp.float32)
            o = inner + cross
            o = o * lax.rsqrt(jnp.mean(o * o, axis=-1, keepdims=True) + EPS)
            out_ref[0, :, vsl] = (o * gr_ref[0, :, vsl].astype(jnp.float32)).astype(out_ref.dtype)
            kv = kv + lax.dot_general((k_h * zeta_ref[h]).astype(jnp.bfloat16), v, _TN,
                                      preferred_element_type=jnp.float32)
        state_ref[psl, :] = state * grow_ref[psl, :] + kv


def _retention(qr, kr, vr, gr, tables):
    b, s, _ = vr.shape
    decay, zeta_b, xi_b, g_rows = tables
    blk = lambda width: pl.BlockSpec((1, RET_CHUNK, width), lambda bi, i: (bi, i, 0))
    const3 = lambda a: pl.BlockSpec(a.shape, lambda bi, i: (0, 0, 0))
    return pl.pallas_call(
        _retention_kernel,
        grid=(b, s // RET_CHUNK),
        in_specs=[blk(RET_QK_WIDTH), blk(RET_QK_WIDTH), blk(RET_V_WIDTH), blk(RET_V_WIDTH),
                  const3(decay), const3(zeta_b), const3(xi_b),
                  pl.BlockSpec(g_rows.shape, lambda bi, i: (0, 0))],
        out_specs=blk(RET_V_WIDTH),
        out_shape=jax.ShapeDtypeStruct((b, s, RET_V_WIDTH), jnp.bfloat16),
        scratch_shapes=[pltpu.VMEM((RET_HEADS * RET_QK_DIM, RET_V_DIM), jnp.float32)],
        compiler_params=pltpu.CompilerParams(
            dimension_semantics=("arbitrary", "arbitrary"), vmem_limit_bytes=VMEM_LIMIT_BYTES),
        name="retention",
    )(qr, kr, vr, gr, decay, zeta_b, xi_b, g_rows)


def _out_proj_kernel(x_ref, ya_ref, yr_ref, wa_ref, wr_ref, gain_ref, out_ref):
    h = (x_ref[...]
         + jnp.dot(ya_ref[...], wa_ref[...], preferred_element_type=jnp.float32)
         + jnp.dot(yr_ref[...], wr_ref[...], preferred_element_type=jnp.float32))
    ms = jnp.mean(h * h, axis=-1, keepdims=True)
    out_ref[...] = (h * lax.rsqrt(ms + EPS)) * gain_ref[...]


def _out_proj(x2, ya2, yr2, wa, wr, gain, tm):
    n, d = x2.shape
    row = lambda width: pl.BlockSpec((tm, width), lambda r: (r, 0))
    const = lambda a: pl.BlockSpec(a.shape, lambda r: (0, 0))
    return pl.pallas_call(
        _out_proj_kernel,
        grid=(n // tm,),
        in_specs=[row(d), row(ATT_WIDTH), row(RET_V_WIDTH), const(wa), const(wr), const(gain)],
        out_specs=row(d),
        out_shape=jax.ShapeDtypeStruct((n, d), jnp.float32),
        compiler_params=pltpu.CompilerParams(
            dimension_semantics=("arbitrary",), vmem_limit_bytes=VMEM_LIMIT_BYTES),
        name="out_proj",
    )(x2, ya2, yr2, wa, wr, gain)


def _reorder_weight(w):
    pts = np.cumsum(SPLITS)[:-1].tolist()
    q_a, k_a, v_a, g_a, q_i, k_i, w_i, q_r, k_r, v_r, g_r = jnp.split(w, pts, axis=1)
    pad = jnp.zeros((w.shape[0], LANES - IDX_HEADS), w.dtype)
    cols = jnp.concatenate([q_a, k_a, v_a, g_a, q_i, q_r, k_r, v_r, g_r, k_i, k_i, w_i, pad], axis=1)
    return cols.astype(jnp.bfloat16)


def kernel(x, norm_gain, w_in, w_out, final_gain):
    b, s, d = x.shape
    depth = norm_gain.shape[0]
    assert d == D_MODEL and s % Q_BLOCK == 0 and w_in.shape[2] == sum(SPLITS)
    topk = min(TOPK_MAX, s // 4)
    rope_a = _rotary_tables(s, ATT_HEAD_DIM, ATT_ROPE_DIM, ROPE_THETA)
    rope_r = _rotary_tables(s, RET_QK_DIM, RET_QK_DIM, RET_THETA)
    ret_tables = _retention_tables()
    assert depth == 1, "the final norm is fused into the single layer's output projection"
    w = _reorder_weight(w_in[0])
    qa, ka, vt, ga, qi, qr, kr, vr, gr, ki, wi = _in_proj(
        x, norm_gain[0][None, :], w, rope_a, rope_r)
    ya = _dsa(qi, wi, ki, qa, ka, vt, ga, topk)
    yr = _retention(qr, kr, vr, gr, ret_tables)
    wo = w_out[0].astype(jnp.bfloat16)
    out = _out_proj(x.reshape(b * s, d), ya.reshape(b * s, ATT_WIDTH),
                    yr.reshape(b * s, RET_V_WIDTH), wo[:ATT_WIDTH], wo[ATT_WIDTH:],
                    final_gain[None, :], 512)
    return out.reshape(b, s, d)
```

```python
import functools

import jax
import jax.numpy as jnp
import numpy as np
from jax import lax
from jax.experimental import pallas as pl
from jax.experimental.pallas import tpu as pltpu

D_MODEL = 1024
ATT_HEADS = 8
ATT_HEAD_DIM = 64
ATT_WIDTH = ATT_HEADS * ATT_HEAD_DIM
ATT_ROPE_DIM = ATT_HEAD_DIM // 4
ROPE_THETA = 500000.0
IDX_HEADS = 4
IDX_DIM = 64
IDX_WIDTH = IDX_HEADS * IDX_DIM
IDX_ROPE_DIM = IDX_DIM // 4
TOPK_MAX = 256
Q_BLOCK = 128
RET_HEADS = 4
RET_QK_DIM = 64
RET_V_DIM = 128
RET_QK_WIDTH = RET_HEADS * RET_QK_DIM
RET_V_WIDTH = RET_HEADS * RET_V_DIM
RET_CHUNK = 128
RET_THETA = 10000.0
MIX_WIDTH = ATT_WIDTH + RET_V_WIDTH
SPLITS = (ATT_WIDTH, ATT_WIDTH, ATT_WIDTH, ATT_WIDTH, IDX_WIDTH, IDX_DIM, IDX_HEADS,
          RET_QK_WIDTH, RET_QK_WIDTH, RET_V_WIDTH, RET_V_WIDTH)
EPS = 1e-6

LANES = 128
SUBLANES = 8
BF16_ROWS = 16
MXU_COLS = 256
LOG2_E = float(np.log2(np.e))
MASK_VALUE = -1e30
VMEM_LIMIT_BYTES = 48 * 1024 * 1024

OFF_QA = 0
OFF_KA = OFF_QA + ATT_WIDTH
OFF_VA = OFF_KA + ATT_WIDTH
OFF_GA = OFF_VA + ATT_WIDTH
OFF_QI = OFF_GA + ATT_WIDTH
OFF_QR = OFF_QI + IDX_WIDTH
OFF_KR = OFF_QR + RET_QK_WIDTH
OFF_VR = OFF_KR + RET_QK_WIDTH
OFF_GR = OFF_VR + RET_V_WIDTH
OFF_KI = OFF_GR + RET_V_WIDTH
OFF_WI = OFF_KI + LANES
PROJ_WIDTH = OFF_WI + LANES
KEY_CHUNK = 256
RET_STEP_CHUNKS = 4
COUNT_ROWS = 64


def _rotary_tables(seq, head_dim, rot_dim, theta):
    half = rot_dim // 2
    inv = 1.0 / (theta ** (jnp.arange(half, dtype=jnp.float32) / half))
    ang = jnp.arange(seq).astype(jnp.float32)[:, None] * inv[None, :]
    cos = jnp.cos(ang)
    sin = jnp.sin(ang)
    pad = head_dim - rot_dim
    ones = jnp.ones((seq, pad), jnp.float32)
    zeros = jnp.zeros((seq, pad), jnp.float32)
    zh = jnp.zeros((seq, half), jnp.float32)
    c = jnp.concatenate([cos, cos, ones], axis=1)
    s_prev = jnp.concatenate([zh, sin, zeros], axis=1)
    s_next = jnp.concatenate([-sin, zh, zeros], axis=1)
    reps = LANES // head_dim
    return jnp.stack([jnp.tile(c, (1, reps)), jnp.tile(s_prev, (1, reps)),
                      jnp.tile(s_next, (1, reps))])


def _retention_tables():
    c = RET_CHUNK
    gamma = 1.0 - 2.0 ** (-5.0 - jnp.arange(RET_HEADS, dtype=jnp.float32))
    log_g = jnp.log(gamma)
    idx = jnp.arange(c, dtype=jnp.float32)
    diff = idx[:, None] - idx[None, :]
    decay = jnp.where(diff[None] >= 0,
                      jnp.exp(log_g[:, None, None] * jnp.maximum(diff, 0.0)[None]), 0.0)
    zeta = jnp.exp(log_g[:, None] * (c - 1.0 - idx)[None, :])
    xi = jnp.exp(log_g[:, None] * (idx + 1.0)[None, :])
    g_chunk = jnp.exp(log_g * c)
    zeta_b = jnp.broadcast_to(zeta[:, :, None], (RET_HEADS, c, LANES))
    xi_b = jnp.broadcast_to(xi[:, :, None], (RET_HEADS, c, LANES))
    g_rows = jnp.repeat(g_chunk, RET_QK_DIM)[:, None]
    g_rows = jnp.broadcast_to(g_rows, (RET_HEADS * RET_QK_DIM, RET_V_DIM))
    return decay, zeta_b, xi_b, g_rows


def _rope(z, tab_ref, half):
    return (z * tab_ref[0] + pltpu.roll(z, half, 1) * tab_ref[1]
            + pltpu.roll(z, LANES - half, 1) * tab_ref[2])


def _silu(g):
    return g * (1.0 / (1.0 + jnp.exp(-g)))


def _in_proj_kernel(x_ref, gain_ref, w_ref, rope_a_ref, rope_r_ref,
                    qa_ref, ka_ref, vt_ref, ga_ref, qi_ref, qr_ref, kr_ref, vr_ref, gr_ref,
                    ki_ref, wi_ref, u_ref):
    x = x_ref[0]
    ms = jnp.mean(x * x, axis=-1, keepdims=True)
    u_ref[...] = ((x * lax.rsqrt(ms + EPS)) * gain_ref[...]).astype(jnp.bfloat16)

    def proj(off, width):
        out = []
        for j in range(width // MXU_COLS):
            c0 = off + j * MXU_COLS
            z = jnp.dot(u_ref[...], w_ref[:, c0:c0 + MXU_COLS], preferred_element_type=jnp.float32)
            for h in range(MXU_COLS // LANES):
                lo = j * MXU_COLS + h * LANES
                out.append((slice(lo, lo + LANES), z[:, h * LANES:(h + 1) * LANES]))
        return out

    a_half = ATT_ROPE_DIM // 2
    r_half = RET_QK_DIM // 2
    att_scale = ATT_HEAD_DIM ** -0.5 * LOG2_E
    ret_scale = RET_QK_DIM ** -0.5
    for sl, z in proj(OFF_QA, ATT_WIDTH):
        qa_ref[0, :, sl] = (_rope(z, rope_a_ref, a_half) * att_scale).astype(qa_ref.dtype)
    for sl, z in proj(OFF_KA, ATT_WIDTH):
        ka_ref[0, :, sl] = _rope(z, rope_a_ref, a_half).astype(ka_ref.dtype)
    for sl, z in proj(OFF_VA, ATT_WIDTH):
        vt_ref[0, 0, sl, :] = z.T.astype(vt_ref.dtype)
    for sl, z in proj(OFF_GA, ATT_WIDTH):
        ga_ref[0, :, sl] = _silu(z).astype(ga_ref.dtype)
    for sl, z in proj(OFF_QI, IDX_WIDTH):
        qi_ref[0, :, sl] = _rope(z, rope_a_ref, a_half).astype(qi_ref.dtype)
    for sl, z in proj(OFF_QR, RET_QK_WIDTH):
        qr_ref[0, :, sl] = _rope(z, rope_r_ref, r_half).astype(qr_ref.dtype)
    for sl, z in proj(OFF_KR, RET_QK_WIDTH):
        kr_ref[0, :, sl] = (_rope(z, rope_r_ref, r_half) * ret_scale).astype(kr_ref.dtype)
    for sl, z in proj(OFF_VR, RET_V_WIDTH):
        vr_ref[0, :, sl] = z.astype(vr_ref.dtype)
    for sl, z in proj(OFF_GR, RET_V_WIDTH):
        gr_ref[0, :, sl] = _silu(z).astype(gr_ref.dtype)
    (_, ki), (_, wi) = proj(OFF_KI, 2 * LANES)
    ki_ref[0] = _rope(ki, rope_a_ref, a_half).astype(ki_ref.dtype)
    wi_ref[0] = wi * ((IDX_DIM ** -0.5) * (IDX_HEADS ** -0.5))


def _in_proj(x, gain, w, rope_a, rope_r):
    b, s, d = x.shape
    tm = KEY_CHUNK
    row = lambda width: pl.BlockSpec((1, tm, width), lambda si, bi: (bi, si, 0))
    tab = pl.BlockSpec((3, tm, LANES), lambda si, bi: (0, si, 0))
    vt_spec = pl.BlockSpec((1, 1, ATT_WIDTH, tm), lambda si, bi: (bi, si, 0, 0))
    bf = jnp.bfloat16
    out_shapes = [
        jax.ShapeDtypeStruct((b, s, ATT_WIDTH), bf),
        jax.ShapeDtypeStruct((b, s, ATT_WIDTH), bf),
        jax.ShapeDtypeStruct((b, s // tm, ATT_WIDTH, tm), bf),
        jax.ShapeDtypeStruct((b, s, ATT_WIDTH), bf),
        jax.ShapeDtypeStruct((b, s, IDX_WIDTH), bf),
        jax.ShapeDtypeStruct((b, s, RET_QK_WIDTH), bf),
        jax.ShapeDtypeStruct((b, s, RET_QK_WIDTH), bf),
        jax.ShapeDtypeStruct((b, s, RET_V_WIDTH), bf),
        jax.ShapeDtypeStruct((b, s, RET_V_WIDTH), bf),
        jax.ShapeDtypeStruct((b, s, LANES), bf),
        jax.ShapeDtypeStruct((b, s, LANES), jnp.float32),
    ]
    out_specs = [row(ATT_WIDTH), row(ATT_WIDTH), vt_spec, row(ATT_WIDTH),
                 row(IDX_WIDTH), row(RET_QK_WIDTH), row(RET_QK_WIDTH),
                 row(RET_V_WIDTH), row(RET_V_WIDTH), row(LANES), row(LANES)]
    return pl.pallas_call(
        _in_proj_kernel,
        grid=(s // tm, b),
        in_specs=[row(d),
                  pl.BlockSpec((1, d), lambda si, bi: (0, 0)),
                  pl.BlockSpec((d, PROJ_WIDTH), lambda si, bi: (0, 0)),
                  tab, tab],
        out_specs=out_specs,
        out_shape=out_shapes,
        scratch_shapes=[pltpu.VMEM((tm, d), jnp.bfloat16)],
        compiler_params=pltpu.CompilerParams(
            dimension_semantics=("arbitrary", "arbitrary"), vmem_limit_bytes=VMEM_LIMIT_BYTES),
        name="in_proj",
    )(x, gain, w, rope_a, rope_r)


_NT = (((1,), (1,)), ((), ()))


def _key_to_float(key):
    k = key ^ jnp.int32(-2 ** 31)
    bits = jnp.where(k >= 0, k, k ^ jnp.int32(2 ** 31 - 1))
    return lax.bitcast_convert_type(bits, jnp.float32)


def _split_heads_t(blk):
    blk_t = blk.astype(jnp.float32).T
    feat = lax.broadcasted_iota(jnp.int32, blk_t.shape, 0)
    zero = jnp.zeros_like(blk_t)
    both = jnp.concatenate([jnp.where(feat < LANES // 2, blk_t, zero),
                            jnp.where(feat >= LANES // 2, blk_t, zero)], axis=1)
    return both.astype(jnp.bfloat16)


def _dsa_kernel(qi_ref, wi_ref, ki_ref, qa_ref, ka_ref, vt_ref, ga_ref,
                out_ref, sc_ref, bias_ref, s_a, s_b, p_a, p_b, acc_ref, *, topk):
    i = pl.program_id(1)
    n_kc = sc_ref.shape[0]
    n_chunks = (i + 2) // 2
    q_pos = i * Q_BLOCK + lax.broadcasted_iota(jnp.int32, (1, Q_BLOCK), 1)
    k_off = lax.broadcasted_iota(jnp.int32, (KEY_CHUNK, 1), 0)

    qi = qi_ref[0]
    qi_pairs = [_split_heads_t(qi[:, p * LANES:(p + 1) * LANES]) for p in range(IDX_HEADS // 2)]
    w_t = wi_ref[0].T
    w_rows = [w_t[h:h + 1, :] for h in range(IDX_HEADS)]

    s_bufs = (s_a, s_b)
    n_steps = (n_chunks + 1) // 2

    def logits_to(c, buf):
        start = pl.multiple_of(c * KEY_CHUNK, KEY_CHUNK)
        ki = ki_ref[0, pl.ds(start, KEY_CHUNK), :]
        for p in range(IDX_HEADS // 2):
            buf[p] = jnp.dot(ki, qi_pairs[p], preferred_element_type=jnp.float32)

    def score_stage(c, cur):
        logits_to(jnp.minimum(c + 1, n_kc - 1), s_bufs[1 - cur])
        buf = s_bufs[cur]
        acc = None
        for h in range(IDX_HEADS):
            lanes = slice((h % 2) * Q_BLOCK, (h % 2 + 1) * Q_BLOCK)
            term = w_rows[h] * jnp.maximum(buf[h // 2, :, lanes], 0.0)
            acc = term if acc is None else acc + term
        sc_ref[c] = jnp.where(c * KEY_CHUNK + k_off <= q_pos, acc, -jnp.inf)

    def score_step(j, carry):
        score_stage(2 * j, 0)
        score_stage(2 * j + 1, 1)
        return carry

    logits_to(0, s_a)
    lax.fori_loop(0, n_steps, score_step, 0)

    def count(pred):
        rows = COUNT_ROWS

        def body(c, acc):
            hit = jnp.where(pred(sc_ref[c]), 1.0, 0.0)
            for j in range(KEY_CHUNK // rows):
                acc = acc + hit[j * rows:(j + 1) * rows]
            return acc
        acc = lax.fori_loop(0, n_chunks, body, jnp.zeros((rows, Q_BLOCK), jnp.float32))
        return jnp.sum(acc, axis=0, keepdims=True)

    @pl.when((i + 1) * Q_BLOCK <= topk)
    def _():
        bias_ref[0] = jnp.where(k_off <= q_pos, 0.0, MASK_VALUE)

    @pl.when((i + 1) * Q_BLOCK > topk)
    def _():
        def bit_step(b, key):
            cand = key | jnp.left_shift(jnp.int32(1), 31 - b)
            f = _key_to_float(cand)
            total = count(lambda s: s >= f)
            return jnp.where(total >= topk, cand, key)

        key = lax.fori_loop(0, 32, bit_step, jnp.zeros((1, Q_BLOCK), jnp.int32))
        thr = _key_to_float(key)
        need = topk - count(lambda s: s > thr)

        sub = lax.broadcasted_iota(jnp.int32, (KEY_CHUNK, Q_BLOCK), 0) % SUBLANES
        n_groups = KEY_CHUNK // SUBLANES

        def body(c, run):
            s = sc_ref[c]
            eq = s == thr
            cnt = jnp.where(eq, 1.0, 0.0)
            for k in (1, 2, 4):
                cnt = cnt + jnp.where(sub >= k, pltpu.roll(cnt, k, 0), 0.0)
            cnt3 = cnt.reshape(n_groups, SUBLANES, Q_BLOCK)
            groups = []
            for g in range(n_groups):
                groups.append(cnt3[g] + run)
                run = run + cnt3[g, SUBLANES - 1:SUBLANES, :]
            incl = jnp.concatenate(groups, axis=0)
            sel = (s > thr) | (eq & (incl <= need))
            bias_ref[c] = jnp.where(sel, 0.0, MASK_VALUE)
            return run

        lax.fori_loop(0, n_chunks, body, jnp.zeros((1, Q_BLOCK), jnp.float32))

    n_pairs = ATT_HEADS // 2
    q_pairs = [_split_heads_t(qa_ref[0, :, p * LANES:(p + 1) * LANES]) for p in range(n_pairs)]
    half = ATT_HEAD_DIM

    pairs = range(n_pairs)
    p_bufs = (p_a, p_b)
    bias_ref[n_chunks] = jnp.full((KEY_CHUNK, Q_BLOCK), MASK_VALUE, jnp.float32)

    def scores_to(c, s_buf):
        start = pl.multiple_of(c * KEY_CHUNK, KEY_CHUNK)
        bias = bias_ref[c]
        bias2 = jnp.concatenate([bias, bias], axis=1)
        cmax = []
        for p in pairs:
            s = jnp.dot(ka_ref[0, pl.ds(start, KEY_CHUNK), p * LANES:(p + 1) * LANES], q_pairs[p],
                        preferred_element_type=jnp.float32) + bias2
            s_buf[p] = s
            cmax.append(jnp.max(s, axis=0, keepdims=True))
        return cmax

    ones_rows = jnp.ones((BF16_ROWS, KEY_CHUNK), jnp.bfloat16)

    def weighted_values(c, p_buf):
        return [jnp.dot(jnp.concatenate([vt_ref[0, c, p * LANES:(p + 1) * LANES, :], ones_rows], axis=0),
                        p_buf[p], preferred_element_type=jnp.float32) for p in pairs]

    def accumulate(l, alpha, o):
        l_new = []
        for p in pairs:
            acc_ref[p, 0] = alpha[p][:, :Q_BLOCK] * acc_ref[p, 0] + o[p][:half, :Q_BLOCK]
            acc_ref[p, 1] = alpha[p][:, Q_BLOCK:] * acc_ref[p, 1] + o[p][half:2 * half, Q_BLOCK:]
            l_new.append(alpha[p] * l[p] + o[p][2 * half:2 * half + 1, :])
        return l_new

    def softmax_to(s_buf, p_buf, cmax, m):
        m_new = [jnp.maximum(m[p], cmax[p]) for p in pairs]
        alpha = [jnp.exp2(m[p] - m_new[p]) for p in pairs]
        for p in pairs:
            p_buf[p] = jnp.exp2(s_buf[p] - m_new[p]).astype(jnp.bfloat16)
        return alpha, m_new

    def stage(c, cur, carry):
        cmax, alpha_prev, m, l = carry
        other = 1 - cur
        o_prev = weighted_values(jnp.maximum(c - 1, 0), p_bufs[other])
        cmax_next = scores_to(jnp.minimum(c + 1, n_kc - 1), s_bufs[other])
        alpha, m = softmax_to(s_bufs[cur], p_bufs[cur], cmax, m)
        return cmax_next, alpha, m, accumulate(l, alpha_prev, o_prev)

    def att_step(j, carry):
        return stage(2 * j + 1, 1, stage(2 * j, 0, carry))

    zeros_row = jnp.zeros((1, 2 * Q_BLOCK), jnp.float32)
    p_b[...] = jnp.zeros(p_b.shape, p_b.dtype)
    acc_ref[...] = jnp.zeros(acc_ref.shape, acc_ref.dtype)
    carry = (scores_to(0, s_a), [zeros_row + 1.0 for _ in pairs], [zeros_row + MASK_VALUE for _ in pairs],
             [zeros_row for _ in pairs])
    _, alpha_last, _, l = lax.fori_loop(0, n_steps, att_step, carry)
    l = accumulate(l, alpha_last, weighted_values(2 * n_steps - 1, p_b))
    for p in pairs:
        a0, a1 = acc_ref[p, 0], acc_ref[p, 1]
        psl = slice(p * LANES, (p + 1) * LANES)
        y_t = jnp.concatenate([a0 / l[p][:, :Q_BLOCK], a1 / l[p][:, Q_BLOCK:]], axis=0)
        out_ref[0, :, psl] = (y_t.T * ga_ref[0, :, psl].astype(jnp.float32)).astype(out_ref.dtype)


def _dsa(qi, wi, ki, qa, ka, vt, ga, topk):
    b, s, _ = qa.shape
    n_kc = s // KEY_CHUNK
    att_buf = (ATT_HEADS // 2, KEY_CHUNK, 2 * Q_BLOCK)
    qblk = lambda width: pl.BlockSpec((1, Q_BLOCK, width), lambda bi, i: (bi, i, 0))
    full = lambda width: pl.BlockSpec((1, s, width), lambda bi, i: (bi, 0, 0))
    return pl.pallas_call(
        functools.partial(_dsa_kernel, topk=topk),
        grid=(b, s // Q_BLOCK),
        in_specs=[qblk(IDX_WIDTH), qblk(LANES), full(LANES), qblk(ATT_WIDTH), full(ATT_WIDTH),
                  pl.BlockSpec((1, n_kc, ATT_WIDTH, KEY_CHUNK), lambda bi, i: (bi, 0, 0, 0)),
                  qblk(ATT_WIDTH)],
        out_specs=qblk(ATT_WIDTH),
        out_shape=jax.ShapeDtypeStruct((b, s, ATT_WIDTH), jnp.bfloat16),
        scratch_shapes=[pltpu.VMEM((n_kc, KEY_CHUNK, Q_BLOCK), jnp.float32),
                        pltpu.VMEM((n_kc + 1, KEY_CHUNK, Q_BLOCK), jnp.float32),
                        pltpu.VMEM(att_buf, jnp.float32), pltpu.VMEM(att_buf, jnp.float32),
                        pltpu.VMEM(att_buf, jnp.bfloat16), pltpu.VMEM(att_buf, jnp.bfloat16),
                        pltpu.VMEM((ATT_HEADS // 2, 2, ATT_HEAD_DIM, Q_BLOCK), jnp.float32)],
        compiler_params=pltpu.CompilerParams(
            dimension_semantics=("arbitrary", "arbitrary"), vmem_limit_bytes=VMEM_LIMIT_BYTES),
        name="sparse_attention",
    )(qi, wi, ki, qa, ka, vt, ga)


_TN = (((0,), (0,)), ((), ()))


def _retention_kernel(qr_ref, kr_ref, vr_ref, gr_ref, decay_ref, zeta_ref, xi_ref, grow_ref,
                      out_ref, state_ref):
    @pl.when(pl.program_id(1) == 0)
    def _():
        state_ref[...] = jnp.zeros_like(state_ref)

    lane = lax.broadcasted_iota(jnp.int32, (RET_CHUNK, LANES), 1)
    for pair in range(RET_HEADS // 2):
        psl = slice(pair * LANES, (pair + 1) * LANES)
        state = state_ref[psl, :]
        for r in range(RET_STEP_CHUNKS):
            rows = slice(r * RET_CHUNK, (r + 1) * RET_CHUNK)
            q_pair = qr_ref[0, rows, psl].astype(jnp.float32)
            k_pair = kr_ref[0, rows, psl].astype(jnp.float32)
            state_b = state.astype(jnp.bfloat16)
            kv = jnp.zeros((LANES, RET_V_DIM), jnp.float32)
            for sub in range(2):
                h = 2 * pair + sub
                vsl = slice(h * RET_V_DIM, (h + 1) * RET_V_DIM)
                in_head = (lane >= sub * RET_QK_DIM) & (lane < (sub + 1) * RET_QK_DIM)
                q_h = jnp.where(in_head, q_pair, 0.0)
                k_h = jnp.where(in_head, k_pair, 0.0)
                v = vr_ref[0, rows, vsl]
                scores = lax.dot_general(q_h.astype(jnp.bfloat16), k_h.astype(jnp.bfloat16), _NT,
                                         preferred_element_type=jnp.float32) * decay_ref[h]
                inner = jnp.dot(scores.astype(jnp.bfloat16), v, preferred_element_type=jnp.float32)
                cross = jnp.dot((q_h * xi_ref[h]).astype(jnp.bfloat16), state_b,
                                preferred_element_type=jnp.float32)
                o = inner + cross
                o = o * lax.rsqrt(jnp.mean(o * o, axis=-1, keepdims=True) + EPS)
                out_ref[0, rows, vsl] = (o * gr_ref[0, rows, vsl].astype(jnp.float32)).astype(out_ref.dtype)
                kv = kv + lax.dot_general((k_h * zeta_ref[h]).astype(jnp.bfloat16), v, _TN,
                                          preferred_element_type=jnp.float32)
            state = state * grow_ref[psl, :] + kv
        state_ref[psl, :] = state


def _retention(qr, kr, vr, gr, tables):
    b, s, _ = vr.shape
    decay, zeta_b, xi_b, g_rows = tables
    rows = RET_CHUNK * RET_STEP_CHUNKS
    blk = lambda width: pl.BlockSpec((1, rows, width), lambda bi, i: (bi, i, 0))
    const3 = lambda a: pl.BlockSpec(a.shape, lambda bi, i: (0, 0, 0))
    return pl.pallas_call(
        _retention_kernel,
        grid=(b, s // rows),
        in_specs=[blk(RET_QK_WIDTH), blk(RET_QK_WIDTH), blk(RET_V_WIDTH), blk(RET_V_WIDTH),
                  const3(decay), const3(zeta_b), const3(xi_b),
                  pl.BlockSpec(g_rows.shape, lambda bi, i: (0, 0))],
        out_specs=blk(RET_V_WIDTH),
        out_shape=jax.ShapeDtypeStruct((b, s, RET_V_WIDTH), jnp.bfloat16),
        scratch_shapes=[pltpu.VMEM((RET_HEADS * RET_QK_DIM, RET_V_DIM), jnp.float32)],
        compiler_params=pltpu.CompilerParams(
            dimension_semantics=("arbitrary", "arbitrary"), vmem_limit_bytes=VMEM_LIMIT_BYTES),
        name="retention",
    )(qr, kr, vr, gr, decay, zeta_b, xi_b, g_rows)


def _out_proj_kernel(x_ref, ya_ref, yr_ref, wa_ref, wr_ref, gain_ref, out_ref):
    h = (x_ref[...]
         + jnp.dot(ya_ref[...], wa_ref[...], preferred_element_type=jnp.float32)
         + jnp.dot(yr_ref[...], wr_ref[...], preferred_element_type=jnp.float32))
    ms = jnp.mean(h * h, axis=-1, keepdims=True)
    out_ref[...] = (h * lax.rsqrt(ms + EPS)) * gain_ref[...]


def _out_proj(x2, ya2, yr2, wa, wr, gain, tm):
    n, d = x2.shape
    row = lambda width: pl.BlockSpec((tm, width), lambda r: (r, 0))
    const = lambda a: pl.BlockSpec(a.shape, lambda r: (0, 0))
    return pl.pallas_call(
        _out_proj_kernel,
        grid=(n // tm,),
        in_specs=[row(d), row(ATT_WIDTH), row(RET_V_WIDTH), const(wa), const(wr), const(gain)],
        out_specs=row(d),
        out_shape=jax.ShapeDtypeStruct((n, d), jnp.float32),
        compiler_params=pltpu.CompilerParams(
            dimension_semantics=("arbitrary",), vmem_limit_bytes=VMEM_LIMIT_BYTES),
        name="out_proj",
    )(x2, ya2, yr2, wa, wr, gain)


def _prep_weight_kernel(w_ref, out_ref):
    offs = np.concatenate([[0], np.cumsum(SPLITS)]).tolist()
    q_a, k_a, v_a, g_a, q_i, k_i, w_i, q_r, k_r, v_r, g_r = [
        slice(offs[j], offs[j + 1]) for j in range(len(SPLITS))]
    dst = 0
    for src_cols in (q_a, k_a, v_a, g_a, q_i, q_r, k_r, v_r, g_r, k_i, k_i, w_i):
        width = src_cols.stop - src_cols.start
        out_ref[:, dst:dst + width] = w_ref[0, :, src_cols].astype(out_ref.dtype)
        dst += width
    out_ref[:, dst:] = jnp.zeros((out_ref.shape[0], PROJ_WIDTH - dst), out_ref.dtype)


def _prep_weight(w_in):
    _, d, width = w_in.shape
    tr = 128
    return pl.pallas_call(
        _prep_weight_kernel,
        grid=(d // tr,),
        in_specs=[pl.BlockSpec((1, tr, width), lambda r: (0, r, 0))],
        out_specs=pl.BlockSpec((tr, PROJ_WIDTH), lambda r: (r, 0)),
        out_shape=jax.ShapeDtypeStruct((d, PROJ_WIDTH), jnp.bfloat16),
        compiler_params=pltpu.CompilerParams(
            dimension_semantics=("arbitrary",), vmem_limit_bytes=VMEM_LIMIT_BYTES),
        name="prep_weight",
    )(w_in)


def kernel(x, norm_gain, w_in, w_out, final_gain):
    b, s, d = x.shape
    depth = norm_gain.shape[0]
    assert d == D_MODEL and s % Q_BLOCK == 0 and w_in.shape[2] == sum(SPLITS)
    topk = min(TOPK_MAX, s // 4)
    rope_a = _rotary_tables(s, ATT_HEAD_DIM, ATT_ROPE_DIM, ROPE_THETA)
    rope_r = _rotary_tables(s, RET_QK_DIM, RET_QK_DIM, RET_THETA)
    ret_tables = _retention_tables()
    assert depth == 1, "the final norm is fused into the single layer's output projection"
    w = _prep_weight(w_in)
    qa, ka, vt, ga, qi, qr, kr, vr, gr, ki, wi = _in_proj(
        x, norm_gain[0][None, :], w, rope_a, rope_r)
    ya = _dsa(qi, wi, ki, qa, ka, vt, ga, topk)
    yr = _retention(qr, kr, vr, gr, ret_tables)
    wo = w_out[0].astype(jnp.bfloat16)
    out = _out_proj(x.reshape(b * s, d), ya.reshape(b * s, ATT_WIDTH),
                    yr.reshape(b * s, RET_V_WIDTH), wo[:ATT_WIDTH], wo[ATT_WIDTH:],
                    final_gain[None, :], 512)
    return out.reshape(b, s, d)
```

```python
import functools

import jax
import jax.numpy as jnp
import numpy as np
from jax import lax
from jax.experimental import pallas as pl
from jax.experimental.pallas import tpu as pltpu

D_MODEL = 1024
ATT_HEADS = 8
ATT_HEAD_DIM = 64
ATT_WIDTH = ATT_HEADS * ATT_HEAD_DIM
ATT_ROPE_DIM = ATT_HEAD_DIM // 4
ROPE_THETA = 500000.0
IDX_HEADS = 4
IDX_DIM = 64
IDX_WIDTH = IDX_HEADS * IDX_DIM
IDX_ROPE_DIM = IDX_DIM // 4
TOPK_MAX = 256
Q_BLOCK = 128
RET_HEADS = 4
RET_QK_DIM = 64
RET_V_DIM = 128
RET_QK_WIDTH = RET_HEADS * RET_QK_DIM
RET_V_WIDTH = RET_HEADS * RET_V_DIM
RET_CHUNK = 128
RET_THETA = 10000.0
MIX_WIDTH = ATT_WIDTH + RET_V_WIDTH
SPLITS = (ATT_WIDTH, ATT_WIDTH, ATT_WIDTH, ATT_WIDTH, IDX_WIDTH, IDX_DIM, IDX_HEADS,
          RET_QK_WIDTH, RET_QK_WIDTH, RET_V_WIDTH, RET_V_WIDTH)
EPS = 1e-6

LANES = 128
SUBLANES = 8
BF16_ROWS = 16
MXU_COLS = 256
LOG2_E = float(np.log2(np.e))
MASK_VALUE = -1e30
VMEM_LIMIT_BYTES = 48 * 1024 * 1024

OFF_QA = 0
OFF_KA = OFF_QA + ATT_WIDTH
OFF_VA = OFF_KA + ATT_WIDTH
OFF_GA = OFF_VA + ATT_WIDTH
OFF_QI = OFF_GA + ATT_WIDTH
OFF_QR = OFF_QI + IDX_WIDTH
OFF_KR = OFF_QR + RET_QK_WIDTH
OFF_VR = OFF_KR + RET_QK_WIDTH
OFF_GR = OFF_VR + RET_V_WIDTH
OFF_KI = OFF_GR + RET_V_WIDTH
OFF_WI = OFF_KI + LANES
PROJ_WIDTH = OFF_WI + LANES
KEY_CHUNK = 256
RET_STEP_CHUNKS = 4
COUNT_ROWS = 64


def _rotary_tables(seq, head_dim, rot_dim, theta):
    half = rot_dim // 2
    inv = 1.0 / (theta ** (jnp.arange(half, dtype=jnp.float32) / half))
    ang = jnp.arange(seq).astype(jnp.float32)[:, None] * inv[None, :]
    cos = jnp.cos(ang)
    sin = jnp.sin(ang)
    pad = head_dim - rot_dim
    ones = jnp.ones((seq, pad), jnp.float32)
    zeros = jnp.zeros((seq, pad), jnp.float32)
    zh = jnp.zeros((seq, half), jnp.float32)
    c = jnp.concatenate([cos, cos, ones], axis=1)
    s_prev = jnp.concatenate([zh, sin, zeros], axis=1)
    s_next = jnp.concatenate([-sin, zh, zeros], axis=1)
    reps = LANES // head_dim
    return jnp.stack([jnp.tile(c, (1, reps)), jnp.tile(s_prev, (1, reps)),
                      jnp.tile(s_next, (1, reps))])


def _retention_tables():
    c = RET_CHUNK
    gamma = 1.0 - 2.0 ** (-5.0 - jnp.arange(RET_HEADS, dtype=jnp.float32))
    log_g = jnp.log(gamma)
    idx = jnp.arange(c, dtype=jnp.float32)
    diff = idx[:, None] - idx[None, :]
    decay = jnp.where(diff[None] >= 0,
                      jnp.exp(log_g[:, None, None] * jnp.maximum(diff, 0.0)[None]), 0.0)
    zeta = jnp.exp(log_g[:, None] * (c - 1.0 - idx)[None, :])
    xi = jnp.exp(log_g[:, None] * (idx + 1.0)[None, :])
    g_chunk = jnp.exp(log_g * c)
    zeta_b = jnp.broadcast_to(zeta[:, :, None], (RET_HEADS, c, LANES))
    xi_b = jnp.broadcast_to(xi[:, :, None], (RET_HEADS, c, LANES))
    g_rows = jnp.repeat(g_chunk, RET_QK_DIM)[:, None]
    g_rows = jnp.broadcast_to(g_rows, (RET_HEADS * RET_QK_DIM, RET_V_DIM))
    return decay, zeta_b, xi_b, g_rows


def _rope(z, tab_ref, half):
    return (z * tab_ref[0] + pltpu.roll(z, half, 1) * tab_ref[1]
            + pltpu.roll(z, LANES - half, 1) * tab_ref[2])


def _silu(g):
    return g * (1.0 / (1.0 + jnp.exp(-g)))


def _in_proj_kernel(x_ref, gain_ref, w_ref, rope_a_ref, rope_r_ref,
                    qa_ref, ka_ref, vt_ref, ga_ref, qi_ref, qr_ref, kr_ref, vr_ref, gr_ref,
                    ki_ref, wi_ref, u_ref):
    x = x_ref[0]
    ms = jnp.mean(x * x, axis=-1, keepdims=True)
    u_ref[...] = ((x * lax.rsqrt(ms + EPS)) * gain_ref[...]).astype(jnp.bfloat16)

    def proj(off, width):
        out = []
        for j in range(width // MXU_COLS):
            c0 = off + j * MXU_COLS
            z = jnp.dot(u_ref[...], w_ref[:, c0:c0 + MXU_COLS], preferred_element_type=jnp.float32)
            for h in range(MXU_COLS // LANES):
                lo = j * MXU_COLS + h * LANES
                out.append((slice(lo, lo + LANES), z[:, h * LANES:(h + 1) * LANES]))
        return out

    a_half = ATT_ROPE_DIM // 2
    r_half = RET_QK_DIM // 2
    att_scale = ATT_HEAD_DIM ** -0.5 * LOG2_E
    ret_scale = RET_QK_DIM ** -0.5
    for sl, z in proj(OFF_QA, ATT_WIDTH):
        qa_ref[0, :, sl] = (_rope(z, rope_a_ref, a_half) * att_scale).astype(qa_ref.dtype)
    for sl, z in proj(OFF_KA, ATT_WIDTH):
        ka_ref[0, :, sl] = _rope(z, rope_a_ref, a_half).astype(ka_ref.dtype)
    for sl, z in proj(OFF_VA, ATT_WIDTH):
        vt_ref[0, 0, sl, :] = z.T.astype(vt_ref.dtype)
    for sl, z in proj(OFF_GA, ATT_WIDTH):
        ga_ref[0, :, sl] = _silu(z).astype(ga_ref.dtype)
    for sl, z in proj(OFF_QI, IDX_WIDTH):
        qi_ref[0, :, sl] = _rope(z, rope_a_ref, a_half).astype(qi_ref.dtype)
    for sl, z in proj(OFF_QR, RET_QK_WIDTH):
        qr_ref[0, :, sl] = _rope(z, rope_r_ref, r_half).astype(qr_ref.dtype)
    for sl, z in proj(OFF_KR, RET_QK_WIDTH):
        kr_ref[0, :, sl] = (_rope(z, rope_r_ref, r_half) * ret_scale).astype(kr_ref.dtype)
    for sl, z in proj(OFF_VR, RET_V_WIDTH):
        vr_ref[0, :, sl] = z.astype(vr_ref.dtype)
    for sl, z in proj(OFF_GR, RET_V_WIDTH):
        gr_ref[0, :, sl] = _silu(z).astype(gr_ref.dtype)
    (_, ki), (_, wi) = proj(OFF_KI, 2 * LANES)
    ki_ref[0] = _rope(ki, rope_a_ref, a_half).astype(ki_ref.dtype)
    wi_ref[0] = wi * ((IDX_DIM ** -0.5) * (IDX_HEADS ** -0.5))


def _in_proj(x, gain, w, rope_a, rope_r):
    b, s, d = x.shape
    tm = KEY_CHUNK
    row = lambda width: pl.BlockSpec((1, tm, width), lambda si, bi: (bi, si, 0))
    tab = pl.BlockSpec((3, tm, LANES), lambda si, bi: (0, si, 0))
    vt_spec = pl.BlockSpec((1, 1, ATT_WIDTH, tm), lambda si, bi: (bi, si, 0, 0))
    bf = jnp.bfloat16
    out_shapes = [
        jax.ShapeDtypeStruct((b, s, ATT_WIDTH), bf),
        jax.ShapeDtypeStruct((b, s, ATT_WIDTH), bf),
        jax.ShapeDtypeStruct((b, s // tm, ATT_WIDTH, tm), bf),
        jax.ShapeDtypeStruct((b, s, ATT_WIDTH), bf),
        jax.ShapeDtypeStruct((b, s, IDX_WIDTH), bf),
        jax.ShapeDtypeStruct((b, s, RET_QK_WIDTH), bf),
        jax.ShapeDtypeStruct((b, s, RET_QK_WIDTH), bf),
        jax.ShapeDtypeStruct((b, s, RET_V_WIDTH), bf),
        jax.ShapeDtypeStruct((b, s, RET_V_WIDTH), bf),
        jax.ShapeDtypeStruct((b, s, LANES), bf),
        jax.ShapeDtypeStruct((b, s, LANES), jnp.float32),
    ]
    out_specs = [row(ATT_WIDTH), row(ATT_WIDTH), vt_spec, row(ATT_WIDTH),
                 row(IDX_WIDTH), row(RET_QK_WIDTH), row(RET_QK_WIDTH),
                 row(RET_V_WIDTH), row(RET_V_WIDTH), row(LANES), row(LANES)]
    return pl.pallas_call(
        _in_proj_kernel,
        grid=(s // tm, b),
        in_specs=[row(d),
                  pl.BlockSpec((1, d), lambda si, bi: (0, 0)),
                  pl.BlockSpec((d, PROJ_WIDTH), lambda si, bi: (0, 0)),
                  tab, tab],
        out_specs=out_specs,
        out_shape=out_shapes,
        scratch_shapes=[pltpu.VMEM((tm, d), jnp.bfloat16)],
        compiler_params=pltpu.CompilerParams(
            dimension_semantics=("arbitrary", "arbitrary"), vmem_limit_bytes=VMEM_LIMIT_BYTES),
        name="in_proj",
    )(x, gain, w, rope_a, rope_r)


_NT = (((1,), (1,)), ((), ()))


def _key_to_float(key):
    k = key ^ jnp.int32(-2 ** 31)
    bits = jnp.where(k >= 0, k, k ^ jnp.int32(2 ** 31 - 1))
    return lax.bitcast_convert_type(bits, jnp.float32)


def _split_heads_t(blk):
    blk_t = blk.astype(jnp.float32).T
    feat = lax.broadcasted_iota(jnp.int32, blk_t.shape, 0)
    zero = jnp.zeros_like(blk_t)
    both = jnp.concatenate([jnp.where(feat < LANES // 2, blk_t, zero),
                            jnp.where(feat >= LANES // 2, blk_t, zero)], axis=1)
    return both.astype(jnp.bfloat16)


def _dsa_kernel(qi_ref, wi_ref, ki_ref, qa_ref, ka_ref, vt_ref, ga_ref,
                out_ref, sc_ref, bias_ref, s_a, s_b, p_a, p_b, acc_ref, *, topk):
    i = pl.program_id(1)
    n_kc = sc_ref.shape[0]
    n_chunks = (i + 2) // 2
    q_pos = i * Q_BLOCK + lax.broadcasted_iota(jnp.int32, (1, Q_BLOCK), 1)
    k_off = lax.broadcasted_iota(jnp.int32, (KEY_CHUNK, 1), 0)

    qi = qi_ref[0]
    qi_pairs = [_split_heads_t(qi[:, p * LANES:(p + 1) * LANES]) for p in range(IDX_HEADS // 2)]
    w_t = wi_ref[0].T
    w_rows = [w_t[h:h + 1, :] for h in range(IDX_HEADS)]

    s_bufs = (s_a, s_b)
    n_steps = (n_chunks + 1) // 2

    def logits_to(c, buf):
        start = pl.multiple_of(c * KEY_CHUNK, KEY_CHUNK)
        ki = ki_ref[0, pl.ds(start, KEY_CHUNK), :]
        for p in range(IDX_HEADS // 2):
            buf[p] = jnp.dot(ki, qi_pairs[p], preferred_element_type=jnp.float32)

    def score_stage(c, cur):
        logits_to(jnp.minimum(c + 1, n_kc - 1), s_bufs[1 - cur])
        buf = s_bufs[cur]
        acc = None
        for h in range(IDX_HEADS):
            lanes = slice((h % 2) * Q_BLOCK, (h % 2 + 1) * Q_BLOCK)
            term = w_rows[h] * jnp.maximum(buf[h // 2, :, lanes], 0.0)
            acc = term if acc is None else acc + term
        sc_ref[c] = jnp.where(c * KEY_CHUNK + k_off <= q_pos, acc, -jnp.inf)

    def score_step(j, carry):
        score_stage(2 * j, 0)
        score_stage(2 * j + 1, 1)
        return carry

    logits_to(0, s_a)
    lax.fori_loop(0, n_steps, score_step, 0)

    def count(pred):
        rows = COUNT_ROWS

        def body(j, acc):
            for c in (2 * j, 2 * j + 1):
                hit = jnp.where(pred(sc_ref[c]), 1.0, 0.0)
                for r in range(KEY_CHUNK // rows):
                    acc = acc + hit[r * rows:(r + 1) * rows]
            return acc
        acc = lax.fori_loop(0, n_steps, body, jnp.zeros((rows, Q_BLOCK), jnp.float32))
        return jnp.sum(acc, axis=0, keepdims=True)

    @pl.when((i + 1) * Q_BLOCK <= topk)
    def _():
        bias_ref[0] = jnp.where(k_off <= q_pos, 0.0, MASK_VALUE)

    @pl.when((i + 1) * Q_BLOCK > topk)
    def _():
        def bit_step(b, key):
            cand = key | jnp.left_shift(jnp.int32(1), 31 - b)
            f = _key_to_float(cand)
            total = count(lambda s: s >= f)
            return jnp.where(total >= topk, cand, key)

        key = lax.fori_loop(0, 32, bit_step, jnp.zeros((1, Q_BLOCK), jnp.int32))
        thr = _key_to_float(key)
        need = topk - count(lambda s: s > thr)

        n_groups = KEY_CHUNK // SUBLANES
        sub = lax.broadcasted_iota(jnp.int32, (1, SUBLANES, Q_BLOCK), 1)
        keep = {k: jnp.where(sub >= k, 1.0, 0.0) for k in (1, 2, 4)}

        def body(c, run):
            s = sc_ref[c]
            eq = s == thr
            cnt = jnp.where(eq, 1.0, 0.0)
            for k in (1, 2, 4):
                shifted = pltpu.roll(cnt, k, 0).reshape(n_groups, SUBLANES, Q_BLOCK) * keep[k]
                cnt = cnt + shifted.reshape(KEY_CHUNK, Q_BLOCK)
            cnt3 = cnt.reshape(n_groups, SUBLANES, Q_BLOCK)
            groups = []
            for g in range(n_groups):
                groups.append(cnt3[g] + run)
                run = run + cnt3[g, SUBLANES - 1:SUBLANES, :]
            incl = jnp.concatenate(groups, axis=0)
            sel = (s > thr) | (eq & (incl <= need))
            bias_ref[c] = jnp.where(sel, 0.0, MASK_VALUE)
            return run

        lax.fori_loop(0, n_chunks, body, jnp.zeros((1, Q_BLOCK), jnp.float32))

    n_pairs = ATT_HEADS // 2
    q_pairs = [_split_heads_t(qa_ref[0, :, p * LANES:(p + 1) * LANES]) for p in range(n_pairs)]
    half = ATT_HEAD_DIM

    pairs = range(n_pairs)
    p_bufs = (p_a, p_b)
    bias_ref[n_chunks] = jnp.full((KEY_CHUNK, Q_BLOCK), MASK_VALUE, jnp.float32)

    def scores_to(c, s_buf):
        start = pl.multiple_of(c * KEY_CHUNK, KEY_CHUNK)
        bias = bias_ref[c]
        bias2 = jnp.concatenate([bias, bias], axis=1)
        cmax = []
        for p in pairs:
            s = jnp.dot(ka_ref[0, pl.ds(start, KEY_CHUNK), p * LANES:(p + 1) * LANES], q_pairs[p],
                        preferred_element_type=jnp.float32) + bias2
            s_buf[p] = s
            cmax.append(jnp.max(s, axis=0, keepdims=True))
        return cmax

    ones_rows = jnp.ones((BF16_ROWS, KEY_CHUNK), jnp.bfloat16)

    def weighted_values(c, p_buf):
        return [jnp.dot(jnp.concatenate([vt_ref[0, c, p * LANES:(p + 1) * LANES, :], ones_rows], axis=0),
                        p_buf[p], preferred_element_type=jnp.float32) for p in pairs]

    def accumulate(l, alpha, o):
        l_new = []
        for p in pairs:
            acc_ref[p, 0] = alpha[p][:, :Q_BLOCK] * acc_ref[p, 0] + o[p][:half, :Q_BLOCK]
            acc_ref[p, 1] = alpha[p][:, Q_BLOCK:] * acc_ref[p, 1] + o[p][half:2 * half, Q_BLOCK:]
            l_new.append(alpha[p] * l[p] + o[p][2 * half:2 * half + 1, :])
        return l_new

    def softmax_to(s_buf, p_buf, cmax, m):
        m_new = [jnp.maximum(m[p], cmax[p]) for p in pairs]
        alpha = [jnp.exp2(m[p] - m_new[p]) for p in pairs]
        for p in pairs:
            p_buf[p] = jnp.exp2(s_buf[p] - m_new[p]).astype(jnp.bfloat16)
        return alpha, m_new

    def stage(c, cur, carry):
        cmax, alpha_prev, m, l = carry
        other = 1 - cur
        o_prev = weighted_values(jnp.maximum(c - 1, 0), p_bufs[other])
        cmax_next = scores_to(jnp.minimum(c + 1, n_kc - 1), s_bufs[other])
        alpha, m = softmax_to(s_bufs[cur], p_bufs[cur], cmax, m)
        return cmax_next, alpha, m, accumulate(l, alpha_prev, o_prev)

    def att_step(j, carry):
        return stage(2 * j + 1, 1, stage(2 * j, 0, carry))

    zeros_row = jnp.zeros((1, 2 * Q_BLOCK), jnp.float32)
    p_b[...] = jnp.zeros(p_b.shape, p_b.dtype)
    acc_ref[...] = jnp.zeros(acc_ref.shape, acc_ref.dtype)
    carry = (scores_to(0, s_a), [zeros_row + 1.0 for _ in pairs], [zeros_row + MASK_VALUE for _ in pairs],
             [zeros_row for _ in pairs])
    _, alpha_last, _, l = lax.fori_loop(0, n_steps, att_step, carry)
    l = accumulate(l, alpha_last, weighted_values(2 * n_steps - 1, p_b))
    for p in pairs:
        a0, a1 = acc_ref[p, 0], acc_ref[p, 1]
        psl = slice(p * LANES, (p + 1) * LANES)
        y_t = jnp.concatenate([a0 / l[p][:, :Q_BLOCK], a1 / l[p][:, Q_BLOCK:]], axis=0)
        out_ref[0, :, psl] = (y_t.T * ga_ref[0, :, psl].astype(jnp.float32)).astype(out_ref.dtype)


def _dsa(qi, wi, ki, qa, ka, vt, ga, topk):
    b, s, _ = qa.shape
    n_kc = s // KEY_CHUNK
    att_buf = (ATT_HEADS // 2, KEY_CHUNK, 2 * Q_BLOCK)
    qblk = lambda width: pl.BlockSpec((1, Q_BLOCK, width), lambda bi, i: (bi, i, 0))
    full = lambda width: pl.BlockSpec((1, s, width), lambda bi, i: (bi, 0, 0))
    return pl.pallas_call(
        functools.partial(_dsa_kernel, topk=topk),
        grid=(b, s // Q_BLOCK),
        in_specs=[qblk(IDX_WIDTH), qblk(LANES), full(LANES), qblk(ATT_WIDTH), full(ATT_WIDTH),
                  pl.BlockSpec((1, n_kc, ATT_WIDTH, KEY_CHUNK), lambda bi, i: (bi, 0, 0, 0)),
                  qblk(ATT_WIDTH)],
        out_specs=qblk(ATT_WIDTH),
        out_shape=jax.ShapeDtypeStruct((b, s, ATT_WIDTH), jnp.bfloat16),
        scratch_shapes=[pltpu.VMEM((n_kc, KEY_CHUNK, Q_BLOCK), jnp.float32),
                        pltpu.VMEM((n_kc + 1, KEY_CHUNK, Q_BLOCK), jnp.float32),
                        pltpu.VMEM(att_buf, jnp.float32), pltpu.VMEM(att_buf, jnp.float32),
                        pltpu.VMEM(att_buf, jnp.bfloat16), pltpu.VMEM(att_buf, jnp.bfloat16),
                        pltpu.VMEM((ATT_HEADS // 2, 2, ATT_HEAD_DIM, Q_BLOCK), jnp.float32)],
        compiler_params=pltpu.CompilerParams(
            dimension_semantics=("arbitrary", "arbitrary"), vmem_limit_bytes=VMEM_LIMIT_BYTES),
        name="sparse_attention",
    )(qi, wi, ki, qa, ka, vt, ga)


_TN = (((0,), (0,)), ((), ()))


def _retention_kernel(qr_ref, kr_ref, vr_ref, gr_ref, decay_ref, zeta_ref, xi_ref, grow_ref,
                      out_ref, state_ref):
    @pl.when(pl.program_id(1) == 0)
    def _():
        state_ref[...] = jnp.zeros_like(state_ref)

    lane = lax.broadcasted_iota(jnp.int32, (RET_CHUNK, LANES), 1)
    for pair in range(RET_HEADS // 2):
        psl = slice(pair * LANES, (pair + 1) * LANES)
        state = state_ref[psl, :]
        for r in range(RET_STEP_CHUNKS):
            rows = slice(r * RET_CHUNK, (r + 1) * RET_CHUNK)
            q_pair = qr_ref[0, rows, psl].astype(jnp.float32)
            k_pair = kr_ref[0, rows, psl].astype(jnp.float32)
            state_b = state.astype(jnp.bfloat16)
            kv = jnp.zeros((LANES, RET_V_DIM), jnp.float32)
            for sub in range(2):
                h = 2 * pair + sub
                vsl = slice(h * RET_V_DIM, (h + 1) * RET_V_DIM)
                in_head = (lane >= sub * RET_QK_DIM) & (lane < (sub + 1) * RET_QK_DIM)
                q_h = jnp.where(in_head, q_pair, 0.0)
                k_h = jnp.where(in_head, k_pair, 0.0)
                v = vr_ref[0, rows, vsl]
                scores = lax.dot_general(q_h.astype(jnp.bfloat16), k_h.astype(jnp.bfloat16), _NT,
                                         preferred_element_type=jnp.float32) * decay_ref[h]
                inner = jnp.dot(scores.astype(jnp.bfloat16), v, preferred_element_type=jnp.float32)
                cross = jnp.dot((q_h * xi_ref[h]).astype(jnp.bfloat16), state_b,
                                preferred_element_type=jnp.float32)
                o = inner + cross
                o = o * lax.rsqrt(jnp.mean(o * o, axis=-1, keepdims=True) + EPS)
                out_ref[0, rows, vsl] = (o * gr_ref[0, rows, vsl].astype(jnp.float32)).astype(out_ref.dtype)
                kv = kv + lax.dot_general((k_h * zeta_ref[h]).astype(jnp.bfloat16), v, _TN,
                                          preferred_element_type=jnp.float32)
            state = state * grow_ref[psl, :] + kv
        state_ref[psl, :] = state


def _retention(qr, kr, vr, gr, tables):
    b, s, _ = vr.shape
    decay, zeta_b, xi_b, g_rows = tables
    rows = RET_CHUNK * RET_STEP_CHUNKS
    blk = lambda width: pl.BlockSpec((1, rows, width), lambda bi, i: (bi, i, 0))
    const3 = lambda a: pl.BlockSpec(a.shape, lambda bi, i: (0, 0, 0))
    return pl.pallas_call(
        _retention_kernel,
        grid=(b, s // rows),
        in_specs=[blk(RET_QK_WIDTH), blk(RET_QK_WIDTH), blk(RET_V_WIDTH), blk(RET_V_WIDTH),
                  const3(decay), const3(zeta_b), const3(xi_b),
                  pl.BlockSpec(g_rows.shape, lambda bi, i: (0, 0))],
        out_specs=blk(RET_V_WIDTH),
        out_shape=jax.ShapeDtypeStruct((b, s, RET_V_WIDTH), jnp.bfloat16),
        scratch_shapes=[pltpu.VMEM((RET_HEADS * RET_QK_DIM, RET_V_DIM), jnp.float32)],
        compiler_params=pltpu.CompilerParams(
            dimension_semantics=("arbitrary", "arbitrary"), vmem_limit_bytes=VMEM_LIMIT_BYTES),
        name="retention",
    )(qr, kr, vr, gr, decay, zeta_b, xi_b, g_rows)


def _out_proj_kernel(x_ref, ya_ref, yr_ref, wa_ref, wr_ref, gain_ref, out_ref):
    h = (x_ref[...]
         + jnp.dot(ya_ref[...], wa_ref[...], preferred_element_type=jnp.float32)
         + jnp.dot(yr_ref[...], wr_ref[...], preferred_element_type=jnp.float32))
    ms = jnp.mean(h * h, axis=-1, keepdims=True)
    out_ref[...] = (h * lax.rsqrt(ms + EPS)) * gain_ref[...]


def _out_proj(x2, ya2, yr2, wa, wr, gain, tm):
    n, d = x2.shape
    row = lambda width: pl.BlockSpec((tm, width), lambda r: (r, 0))
    const = lambda a: pl.BlockSpec(a.shape, lambda r: (0, 0))
    return pl.pallas_call(
        _out_proj_kernel,
        grid=(n // tm,),
        in_specs=[row(d), row(ATT_WIDTH), row(RET_V_WIDTH), const(wa), const(wr), const(gain)],
        out_specs=row(d),
        out_shape=jax.ShapeDtypeStruct((n, d), jnp.float32),
        compiler_params=pltpu.CompilerParams(
            dimension_semantics=("arbitrary",), vmem_limit_bytes=VMEM_LIMIT_BYTES),
        name="out_proj",
    )(x2, ya2, yr2, wa, wr, gain)


def _prep_weight_kernel(w_ref, out_ref):
    offs = np.concatenate([[0], np.cumsum(SPLITS)]).tolist()
    q_a, k_a, v_a, g_a, q_i, k_i, w_i, q_r, k_r, v_r, g_r = [
        slice(offs[j], offs[j + 1]) for j in range(len(SPLITS))]
    dst = 0
    for src_cols in (q_a, k_a, v_a, g_a, q_i, q_r, k_r, v_r, g_r, k_i, k_i, w_i):
        width = src_cols.stop - src_cols.start
        out_ref[:, dst:dst + width] = w_ref[:, src_cols].astype(out_ref.dtype)
        dst += width
    out_ref[:, dst:] = jnp.zeros((out_ref.shape[0], PROJ_WIDTH - dst), out_ref.dtype)


def _prep_weight(w_in):
    d, width = w_in.shape
    tr = 128
    return pl.pallas_call(
        _prep_weight_kernel,
        grid=(d // tr,),
        in_specs=[pl.BlockSpec((tr, width), lambda r: (r, 0))],
        out_specs=pl.BlockSpec((tr, PROJ_WIDTH), lambda r: (r, 0)),
        out_shape=jax.ShapeDtypeStruct((d, PROJ_WIDTH), jnp.bfloat16),
        compiler_params=pltpu.CompilerParams(
            dimension_semantics=("arbitrary",), vmem_limit_bytes=VMEM_LIMIT_BYTES),
        name="prep_weight",
    )(w_in)


def kernel(x, norm_gain, w_in, w_out, final_gain):
    b, s, d = x.shape
    depth = norm_gain.shape[0]
    assert d == D_MODEL and s % Q_BLOCK == 0 and w_in.shape[2] == sum(SPLITS)
    topk = min(TOPK_MAX, s // 4)
    rope_a = _rotary_tables(s, ATT_HEAD_DIM, ATT_ROPE_DIM, ROPE_THETA)
    rope_r = _rotary_tables(s, RET_QK_DIM, RET_QK_DIM, RET_THETA)
    ret_tables = _retention_tables()
    assert depth == 1, "the final norm is fused into the single layer's output projection"
    w = _prep_weight(w_in[0])
    qa, ka, vt, ga, qi, qr, kr, vr, gr, ki, wi = _in_proj(
        x, norm_gain[0][None, :], w, rope_a, rope_r)
    ya = _dsa(qi, wi, ki, qa, ka, vt, ga, topk)
    yr = _retention(qr, kr, vr, gr, ret_tables)
    wo = w_out[0].astype(jnp.bfloat16)
    out = _out_proj(x.reshape(b * s, d), ya.reshape(b * s, ATT_WIDTH),
                    yr.reshape(b * s, RET_V_WIDTH), wo[:ATT_WIDTH], wo[ATT_WIDTH:],
                    final_gain[None, :], 512)
    return out.reshape(b, s, d)
```

```python
import functools

import jax
import jax.numpy as jnp
import numpy as np
from jax import lax
from jax.experimental import pallas as pl
from jax.experimental.pallas import tpu as pltpu

D_MODEL = 1024
ATT_HEADS = 8
ATT_HEAD_DIM = 64
ATT_WIDTH = ATT_HEADS * ATT_HEAD_DIM
ATT_ROPE_DIM = ATT_HEAD_DIM // 4
ROPE_THETA = 500000.0
IDX_HEADS = 4
IDX_DIM = 64
IDX_WIDTH = IDX_HEADS * IDX_DIM
IDX_ROPE_DIM = IDX_DIM // 4
TOPK_MAX = 256
Q_BLOCK = 256
RET_HEADS = 4
RET_QK_DIM = 64
RET_V_DIM = 128
RET_QK_WIDTH = RET_HEADS * RET_QK_DIM
RET_V_WIDTH = RET_HEADS * RET_V_DIM
RET_CHUNK = 128
RET_THETA = 10000.0
MIX_WIDTH = ATT_WIDTH + RET_V_WIDTH
SPLITS = (ATT_WIDTH, ATT_WIDTH, ATT_WIDTH, ATT_WIDTH, IDX_WIDTH, IDX_DIM, IDX_HEADS,
          RET_QK_WIDTH, RET_QK_WIDTH, RET_V_WIDTH, RET_V_WIDTH)
EPS = 1e-6

LANES = 128
SUBLANES = 8
BF16_ROWS = 16
MXU_COLS = 256
LOG2_E = float(np.log2(np.e))
MASK_VALUE = -1e30
VMEM_LIMIT_BYTES = 48 * 1024 * 1024

OFF_QA = 0
OFF_KA = OFF_QA + ATT_WIDTH
OFF_VA = OFF_KA + ATT_WIDTH
OFF_GA = OFF_VA + ATT_WIDTH
OFF_QI = OFF_GA + ATT_WIDTH
OFF_QR = OFF_QI + IDX_WIDTH
OFF_KR = OFF_QR + RET_QK_WIDTH
OFF_VR = OFF_KR + RET_QK_WIDTH
OFF_GR = OFF_VR + RET_V_WIDTH
OFF_KI = OFF_GR + RET_V_WIDTH
OFF_WI = OFF_KI + LANES
PROJ_WIDTH = OFF_WI + LANES
KEY_CHUNK = 256
RET_STEP_CHUNKS = 4
COUNT_ROWS = 64


def _rotary_tables(seq, head_dim, rot_dim, theta):
    half = rot_dim // 2
    inv = 1.0 / (theta ** (jnp.arange(half, dtype=jnp.float32) / half))
    ang = jnp.arange(seq).astype(jnp.float32)[:, None] * inv[None, :]
    cos = jnp.cos(ang)
    sin = jnp.sin(ang)
    pad = head_dim - rot_dim
    ones = jnp.ones((seq, pad), jnp.float32)
    zeros = jnp.zeros((seq, pad), jnp.float32)
    zh = jnp.zeros((seq, half), jnp.float32)
    c = jnp.concatenate([cos, cos, ones], axis=1)
    s_prev = jnp.concatenate([zh, sin, zeros], axis=1)
    s_next = jnp.concatenate([-sin, zh, zeros], axis=1)
    reps = LANES // head_dim
    return jnp.stack([jnp.tile(c, (1, reps)), jnp.tile(s_prev, (1, reps)),
                      jnp.tile(s_next, (1, reps))])


def _retention_tables():
    c = RET_CHUNK
    gamma = 1.0 - 2.0 ** (-5.0 - jnp.arange(RET_HEADS, dtype=jnp.float32))
    log_g = jnp.log(gamma)
    idx = jnp.arange(c, dtype=jnp.float32)
    diff = idx[:, None] - idx[None, :]
    decay = jnp.where(diff[None] >= 0,
                      jnp.exp(log_g[:, None, None] * jnp.maximum(diff, 0.0)[None]), 0.0)
    zeta = jnp.exp(log_g[:, None] * (c - 1.0 - idx)[None, :])
    xi = jnp.exp(log_g[:, None] * (idx + 1.0)[None, :])
    g_chunk = jnp.exp(log_g * c)
    zeta_b = jnp.broadcast_to(zeta[:, :, None], (RET_HEADS, c, LANES))
    xi_b = jnp.broadcast_to(xi[:, :, None], (RET_HEADS, c, LANES))
    g_rows = jnp.repeat(g_chunk, RET_QK_DIM)[:, None]
    g_rows = jnp.broadcast_to(g_rows, (RET_HEADS * RET_QK_DIM, RET_V_DIM))
    return decay, zeta_b, xi_b, g_rows


def _rope(z, tab_ref, half):
    return (z * tab_ref[0] + pltpu.roll(z, half, 1) * tab_ref[1]
            + pltpu.roll(z, LANES - half, 1) * tab_ref[2])


def _silu(g):
    return g * (1.0 / (1.0 + jnp.exp(-g)))


def _in_proj_kernel(x_ref, gain_ref, w_ref, rope_a_ref, rope_r_ref,
                    qa_ref, ka_ref, vt_ref, ga_ref, qi_ref, qr_ref, kr_ref, vr_ref, gr_ref,
                    ki_ref, wi_ref, u_ref):
    x = x_ref[0]
    ms = jnp.mean(x * x, axis=-1, keepdims=True)
    u_ref[...] = ((x * lax.rsqrt(ms + EPS)) * gain_ref[...]).astype(jnp.bfloat16)

    def proj(off, width):
        out = []
        for j in range(width // MXU_COLS):
            c0 = off + j * MXU_COLS
            z = jnp.dot(u_ref[...], w_ref[:, c0:c0 + MXU_COLS], preferred_element_type=jnp.float32)
            for h in range(MXU_COLS // LANES):
                lo = j * MXU_COLS + h * LANES
                out.append((slice(lo, lo + LANES), z[:, h * LANES:(h + 1) * LANES]))
        return out

    a_half = ATT_ROPE_DIM // 2
    r_half = RET_QK_DIM // 2
    att_scale = ATT_HEAD_DIM ** -0.5 * LOG2_E
    ret_scale = RET_QK_DIM ** -0.5
    for sl, z in proj(OFF_QA, ATT_WIDTH):
        qa_ref[0, :, sl] = (_rope(z, rope_a_ref, a_half) * att_scale).astype(qa_ref.dtype)
    for sl, z in proj(OFF_KA, ATT_WIDTH):
        ka_ref[0, :, sl] = _rope(z, rope_a_ref, a_half).astype(ka_ref.dtype)
    for sl, z in proj(OFF_VA, ATT_WIDTH):
        vt_ref[0, 0, sl, :] = z.T.astype(vt_ref.dtype)
    for sl, z in proj(OFF_GA, ATT_WIDTH):
        ga_ref[0, :, sl] = _silu(z).astype(ga_ref.dtype)
    for sl, z in proj(OFF_QI, IDX_WIDTH):
        qi_ref[0, :, sl] = _rope(z, rope_a_ref, a_half).astype(qi_ref.dtype)
    for sl, z in proj(OFF_QR, RET_QK_WIDTH):
        qr_ref[0, :, sl] = _rope(z, rope_r_ref, r_half).astype(qr_ref.dtype)
    for sl, z in proj(OFF_KR, RET_QK_WIDTH):
        kr_ref[0, :, sl] = (_rope(z, rope_r_ref, r_half) * ret_scale).astype(kr_ref.dtype)
    for sl, z in proj(OFF_VR, RET_V_WIDTH):
        vr_ref[0, :, sl] = z.astype(vr_ref.dtype)
    for sl, z in proj(OFF_GR, RET_V_WIDTH):
        gr_ref[0, :, sl] = _silu(z).astype(gr_ref.dtype)
    (_, ki), (_, wi) = proj(OFF_KI, 2 * LANES)
    ki_ref[0] = _rope(ki, rope_a_ref, a_half).astype(ki_ref.dtype)
    wi_ref[0] = wi * ((IDX_DIM ** -0.5) * (IDX_HEADS ** -0.5))


def _in_proj(x, gain, w, rope_a, rope_r):
    b, s, d = x.shape
    tm = KEY_CHUNK
    row = lambda width: pl.BlockSpec((1, tm, width), lambda si, bi: (bi, si, 0))
    tab = pl.BlockSpec((3, tm, LANES), lambda si, bi: (0, si, 0))
    vt_spec = pl.BlockSpec((1, 1, ATT_WIDTH, tm), lambda si, bi: (bi, si, 0, 0))
    bf = jnp.bfloat16
    out_shapes = [
        jax.ShapeDtypeStruct((b, s, ATT_WIDTH), bf),
        jax.ShapeDtypeStruct((b, s, ATT_WIDTH), bf),
        jax.ShapeDtypeStruct((b, s // tm, ATT_WIDTH, tm), bf),
        jax.ShapeDtypeStruct((b, s, ATT_WIDTH), bf),
        jax.ShapeDtypeStruct((b, s, IDX_WIDTH), bf),
        jax.ShapeDtypeStruct((b, s, RET_QK_WIDTH), bf),
        jax.ShapeDtypeStruct((b, s, RET_QK_WIDTH), bf),
        jax.ShapeDtypeStruct((b, s, RET_V_WIDTH), bf),
        jax.ShapeDtypeStruct((b, s, RET_V_WIDTH), bf),
        jax.ShapeDtypeStruct((b, s, LANES), bf),
        jax.ShapeDtypeStruct((b, s, LANES), jnp.float32),
    ]
    out_specs = [row(ATT_WIDTH), row(ATT_WIDTH), vt_spec, row(ATT_WIDTH),
                 row(IDX_WIDTH), row(RET_QK_WIDTH), row(RET_QK_WIDTH),
                 row(RET_V_WIDTH), row(RET_V_WIDTH), row(LANES), row(LANES)]
    return pl.pallas_call(
        _in_proj_kernel,
        grid=(s // tm, b),
        in_specs=[row(d),
                  pl.BlockSpec((1, d), lambda si, bi: (0, 0)),
                  pl.BlockSpec((d, PROJ_WIDTH), lambda si, bi: (0, 0)),
                  tab, tab],
        out_specs=out_specs,
        out_shape=out_shapes,
        scratch_shapes=[pltpu.VMEM((tm, d), jnp.bfloat16)],
        compiler_params=pltpu.CompilerParams(
            dimension_semantics=("arbitrary", "arbitrary"), vmem_limit_bytes=VMEM_LIMIT_BYTES),
        name="in_proj",
    )(x, gain, w, rope_a, rope_r)


_NT = (((1,), (1,)), ((), ()))


def _key_to_float(key):
    k = key ^ jnp.int32(-2 ** 31)
    bits = jnp.where(k >= 0, k, k ^ jnp.int32(2 ** 31 - 1))
    return lax.bitcast_convert_type(bits, jnp.float32)


def _split_heads_t(blk):
    blk_t = blk.astype(jnp.float32).T
    feat = lax.broadcasted_iota(jnp.int32, blk_t.shape, 0)
    zero = jnp.zeros_like(blk_t)
    both = jnp.concatenate([jnp.where(feat < LANES // 2, blk_t, zero),
                            jnp.where(feat >= LANES // 2, blk_t, zero)], axis=1)
    return both.astype(jnp.bfloat16)


def _dsa_kernel(qi_ref, wi_ref, ki_ref, qa_ref, ka_ref, vt_ref, ga_ref,
                out_ref, sc_ref, bias_ref, s_a, s_b, p_a, p_b, acc_ref, *, topk):
    i = pl.program_id(1)
    n_kc = sc_ref.shape[0]
    n_chunks = ((i + 1) * Q_BLOCK + KEY_CHUNK - 1) // KEY_CHUNK
    q_pos = i * Q_BLOCK + lax.broadcasted_iota(jnp.int32, (1, Q_BLOCK), 1)
    k_off = lax.broadcasted_iota(jnp.int32, (KEY_CHUNK, 1), 0)

    qi = qi_ref[0]
    qi_pairs = [_split_heads_t(qi[:, p * LANES:(p + 1) * LANES]) for p in range(IDX_HEADS // 2)]
    w_t = wi_ref[0].T
    w_rows = [w_t[h:h + 1, :] for h in range(IDX_HEADS)]

    s_bufs = (s_a, s_b)
    n_steps = (n_chunks + 1) // 2

    def logits_to(c, buf):
        start = pl.multiple_of(c * KEY_CHUNK, KEY_CHUNK)
        ki = ki_ref[0, pl.ds(start, KEY_CHUNK), :]
        for p in range(IDX_HEADS // 2):
            buf[p] = jnp.dot(ki, qi_pairs[p], preferred_element_type=jnp.float32)

    def score_stage(c, cur):
        logits_to(jnp.minimum(c + 1, n_kc - 1), s_bufs[1 - cur])
        buf = s_bufs[cur]
        acc = None
        for h in range(IDX_HEADS):
            lanes = slice((h % 2) * Q_BLOCK, (h % 2 + 1) * Q_BLOCK)
            term = w_rows[h] * jnp.maximum(buf[h // 2, :, lanes], 0.0)
            acc = term if acc is None else acc + term
        sc_ref[c] = jnp.where(c * KEY_CHUNK + k_off <= q_pos, acc, -jnp.inf)

    def score_step(j, carry):
        score_stage(2 * j, 0)
        score_stage(2 * j + 1, 1)
        return carry

    logits_to(0, s_a)
    lax.fori_loop(0, n_steps, score_step, 0)

    def count(pred):
        rows = COUNT_ROWS

        def body(j, acc):
            for c in (2 * j, 2 * j + 1):
                hit = jnp.where(pred(sc_ref[c]), 1.0, 0.0)
                for r in range(KEY_CHUNK // rows):
                    acc = acc + hit[r * rows:(r + 1) * rows]
            return acc
        acc = lax.fori_loop(0, n_steps, body, jnp.zeros((rows, Q_BLOCK), jnp.float32))
        return jnp.sum(acc, axis=0, keepdims=True)

    @pl.when((i + 1) * Q_BLOCK <= topk)
    def _():
        bias_ref[0] = jnp.where(k_off <= q_pos, 0.0, MASK_VALUE)

    @pl.when((i + 1) * Q_BLOCK > topk)
    def _():
        def bit_step(b, key):
            cand = key | jnp.left_shift(jnp.int32(1), 31 - b)
            f = _key_to_float(cand)
            total = count(lambda s: s >= f)
            return jnp.where(total >= topk, cand, key)

        key = lax.fori_loop(0, 32, bit_step, jnp.zeros((1, Q_BLOCK), jnp.int32))
        thr = _key_to_float(key)
        need = topk - count(lambda s: s > thr)

        n_groups = KEY_CHUNK // SUBLANES
        sub = lax.broadcasted_iota(jnp.int32, (1, SUBLANES, Q_BLOCK), 1)
        keep = {k: jnp.where(sub >= k, 1.0, 0.0) for k in (1, 2, 4)}

        def body(c, run):
            s = sc_ref[c]
            eq = s == thr
            cnt = jnp.where(eq, 1.0, 0.0)
            for k in (1, 2, 4):
                shifted = pltpu.roll(cnt, k, 0).reshape(n_groups, SUBLANES, Q_BLOCK) * keep[k]
                cnt = cnt + shifted.reshape(KEY_CHUNK, Q_BLOCK)
            cnt3 = cnt.reshape(n_groups, SUBLANES, Q_BLOCK)
            groups = []
            for g in range(n_groups):
                groups.append(cnt3[g] + run)
                run = run + cnt3[g, SUBLANES - 1:SUBLANES, :]
            incl = jnp.concatenate(groups, axis=0)
            sel = (s > thr) | (eq & (incl <= need))
            bias_ref[c] = jnp.where(sel, 0.0, MASK_VALUE)
            return run

        lax.fori_loop(0, n_chunks, body, jnp.zeros((1, Q_BLOCK), jnp.float32))

    n_pairs = ATT_HEADS // 2
    q_pairs = [_split_heads_t(qa_ref[0, :, p * LANES:(p + 1) * LANES]) for p in range(n_pairs)]
    half = ATT_HEAD_DIM

    pairs = range(n_pairs)
    p_bufs = (p_a, p_b)
    bias_ref[n_chunks] = jnp.full((KEY_CHUNK, Q_BLOCK), MASK_VALUE, jnp.float32)

    def scores_to(c, s_buf):
        start = pl.multiple_of(c * KEY_CHUNK, KEY_CHUNK)
        bias = bias_ref[c]
        bias2 = jnp.concatenate([bias, bias], axis=1)
        cmax = []
        for p in pairs:
            s = jnp.dot(ka_ref[0, pl.ds(start, KEY_CHUNK), p * LANES:(p + 1) * LANES], q_pairs[p],
                        preferred_element_type=jnp.float32) + bias2
            s_buf[p] = s
            cmax.append(jnp.max(s, axis=0, keepdims=True))
        return cmax

    ones_rows = jnp.ones((BF16_ROWS, KEY_CHUNK), jnp.bfloat16)

    def weighted_values(c, p_buf):
        return [jnp.dot(jnp.concatenate([vt_ref[0, c, p * LANES:(p + 1) * LANES, :], ones_rows], axis=0),
                        p_buf[p], preferred_element_type=jnp.float32) for p in pairs]

    def accumulate(l, alpha, o):
        l_new = []
        for p in pairs:
            acc_ref[p, 0] = alpha[p][:, :Q_BLOCK] * acc_ref[p, 0] + o[p][:half, :Q_BLOCK]
            acc_ref[p, 1] = alpha[p][:, Q_BLOCK:] * acc_ref[p, 1] + o[p][half:2 * half, Q_BLOCK:]
            l_new.append(alpha[p] * l[p] + o[p][2 * half:2 * half + 1, :])
        return l_new

    def softmax_to(s_buf, p_buf, cmax, m):
        m_new = [jnp.maximum(m[p], cmax[p]) for p in pairs]
        alpha = [jnp.exp2(m[p] - m_new[p]) for p in pairs]
        for p in pairs:
            p_buf[p] = jnp.exp2(s_buf[p] - m_new[p]).astype(jnp.bfloat16)
        return alpha, m_new

    def stage(c, cur, carry):
        cmax, alpha_prev, m, l = carry
        other = 1 - cur
        o_prev = weighted_values(jnp.maximum(c - 1, 0), p_bufs[other])
        cmax_next = scores_to(jnp.minimum(c + 1, n_kc - 1), s_bufs[other])
        alpha, m = softmax_to(s_bufs[cur], p_bufs[cur], cmax, m)
        return cmax_next, alpha, m, accumulate(l, alpha_prev, o_prev)

    def att_step(j, carry):
        return stage(2 * j + 1, 1, stage(2 * j, 0, carry))

    zeros_row = jnp.zeros((1, 2 * Q_BLOCK), jnp.float32)
    p_b[...] = jnp.zeros(p_b.shape, p_b.dtype)
    acc_ref[...] = jnp.zeros(acc_ref.shape, acc_ref.dtype)
    carry = (scores_to(0, s_a), [zeros_row + 1.0 for _ in pairs], [zeros_row + MASK_VALUE for _ in pairs],
             [zeros_row for _ in pairs])
    _, alpha_last, _, l = lax.fori_loop(0, n_steps, att_step, carry)
    l = accumulate(l, alpha_last, weighted_values(2 * n_steps - 1, p_b))
    for p in pairs:
        a0, a1 = acc_ref[p, 0], acc_ref[p, 1]
        psl = slice(p * LANES, (p + 1) * LANES)
        y_t = jnp.concatenate([a0 / l[p][:, :Q_BLOCK], a1 / l[p][:, Q_BLOCK:]], axis=0)
        out_ref[0, :, psl] = (y_t.T * ga_ref[0, :, psl].astype(jnp.float32)).astype(out_ref.dtype)


def _dsa(qi, wi, ki, qa, ka, vt, ga, topk):
    b, s, _ = qa.shape
    n_kc = s // KEY_CHUNK
    att_buf = (ATT_HEADS // 2, KEY_CHUNK, 2 * Q_BLOCK)
    qblk = lambda width: pl.BlockSpec((1, Q_BLOCK, width), lambda bi, i: (bi, i, 0))
    full = lambda width: pl.BlockSpec((1, s, width), lambda bi, i: (bi, 0, 0))
    return pl.pallas_call(
        functools.partial(_dsa_kernel, topk=topk),
        grid=(b, s // Q_BLOCK),
        in_specs=[qblk(IDX_WIDTH), qblk(LANES), full(LANES), qblk(ATT_WIDTH), full(ATT_WIDTH),
                  pl.BlockSpec((1, n_kc, ATT_WIDTH, KEY_CHUNK), lambda bi, i: (bi, 0, 0, 0)),
                  qblk(ATT_WIDTH)],
        out_specs=qblk(ATT_WIDTH),
        out_shape=jax.ShapeDtypeStruct((b, s, ATT_WIDTH), jnp.bfloat16),
        scratch_shapes=[pltpu.VMEM((n_kc, KEY_CHUNK, Q_BLOCK), jnp.float32),
                        pltpu.VMEM((n_kc + 1, KEY_CHUNK, Q_BLOCK), jnp.float32),
                        pltpu.VMEM(att_buf, jnp.float32), pltpu.VMEM(att_buf, jnp.float32),
                        pltpu.VMEM(att_buf, jnp.bfloat16), pltpu.VMEM(att_buf, jnp.bfloat16),
                        pltpu.VMEM((ATT_HEADS // 2, 2, ATT_HEAD_DIM, Q_BLOCK), jnp.float32)],
        compiler_params=pltpu.CompilerParams(
            dimension_semantics=("arbitrary", "arbitrary"), vmem_limit_bytes=VMEM_LIMIT_BYTES),
        name="sparse_attention",
    )(qi, wi, ki, qa, ka, vt, ga)


_TN = (((0,), (0,)), ((), ()))


def _retention_kernel(qr_ref, kr_ref, vr_ref, gr_ref, decay_ref, zeta_ref, xi_ref, grow_ref,
                      out_ref, state_ref):
    @pl.when(pl.program_id(1) == 0)
    def _():
        state_ref[...] = jnp.zeros_like(state_ref)

    lane = lax.broadcasted_iota(jnp.int32, (RET_CHUNK, LANES), 1)
    for pair in range(RET_HEADS // 2):
        psl = slice(pair * LANES, (pair + 1) * LANES)
        state = state_ref[psl, :]
        for r in range(RET_STEP_CHUNKS):
            rows = slice(r * RET_CHUNK, (r + 1) * RET_CHUNK)
            q_pair = qr_ref[0, rows, psl].astype(jnp.float32)
            k_pair = kr_ref[0, rows, psl].astype(jnp.float32)
            state_b = state.astype(jnp.bfloat16)
            kv = jnp.zeros((LANES, RET_V_DIM), jnp.float32)
            for sub in range(2):
                h = 2 * pair + sub
                vsl = slice(h * RET_V_DIM, (h + 1) * RET_V_DIM)
                in_head = (lane >= sub * RET_QK_DIM) & (lane < (sub + 1) * RET_QK_DIM)
                q_h = jnp.where(in_head, q_pair, 0.0)
                k_h = jnp.where(in_head, k_pair, 0.0)
                v = vr_ref[0, rows, vsl]
                scores = lax.dot_general(q_h.astype(jnp.bfloat16), k_h.astype(jnp.bfloat16), _NT,
                                         preferred_element_type=jnp.float32) * decay_ref[h]
                inner = jnp.dot(scores.astype(jnp.bfloat16), v, preferred_element_type=jnp.float32)
                cross = jnp.dot((q_h * xi_ref[h]).astype(jnp.bfloat16), state_b,
                                preferred_element_type=jnp.float32)
                o = inner + cross
                o = o * lax.rsqrt(jnp.mean(o * o, axis=-1, keepdims=True) + EPS)
                out_ref[0, rows, vsl] = (o * gr_ref[0, rows, vsl].astype(jnp.float32)).astype(out_ref.dtype)
                kv = kv + lax.dot_general((k_h * zeta_ref[h]).astype(jnp.bfloat16), v, _TN,
                                          preferred_element_type=jnp.float32)
            state = state * grow_ref[psl, :] + kv
        state_ref[psl, :] = state


def _retention(qr, kr, vr, gr, tables):
    b, s, _ = vr.shape
    decay, zeta_b, xi_b, g_rows = tables
    rows = RET_CHUNK * RET_STEP_CHUNKS
    blk = lambda width: pl.BlockSpec((1, rows, width), lambda bi, i: (bi, i, 0))
    const3 = lambda a: pl.BlockSpec(a.shape, lambda bi, i: (0, 0, 0))
    return pl.pallas_call(
        _retention_kernel,
        grid=(b, s // rows),
        in_specs=[blk(RET_QK_WIDTH), blk(RET_QK_WIDTH), blk(RET_V_WIDTH), blk(RET_V_WIDTH),
                  const3(decay), const3(zeta_b), const3(xi_b),
                  pl.BlockSpec(g_rows.shape, lambda bi, i: (0, 0))],
        out_specs=blk(RET_V_WIDTH),
        out_shape=jax.ShapeDtypeStruct((b, s, RET_V_WIDTH), jnp.bfloat16),
        scratch_shapes=[pltpu.VMEM((RET_HEADS * RET_QK_DIM, RET_V_DIM), jnp.float32)],
        compiler_params=pltpu.CompilerParams(
            dimension_semantics=("arbitrary", "arbitrary"), vmem_limit_bytes=VMEM_LIMIT_BYTES),
        name="retention",
    )(qr, kr, vr, gr, decay, zeta_b, xi_b, g_rows)


def _out_proj_kernel(x_ref, ya_ref, yr_ref, wa_ref, wr_ref, gain_ref, out_ref):
    h = (x_ref[...]
         + jnp.dot(ya_ref[...], wa_ref[...], preferred_element_type=jnp.float32)
         + jnp.dot(yr_ref[...], wr_ref[...], preferred_element_type=jnp.float32))
    ms = jnp.mean(h * h, axis=-1, keepdims=True)
    out_ref[...] = (h * lax.rsqrt(ms + EPS)) * gain_ref[...]


def _out_proj(x2, ya2, yr2, wa, wr, gain, tm):
    n, d = x2.shape
    row = lambda width: pl.BlockSpec((tm, width), lambda r: (r, 0))
    const = lambda a: pl.BlockSpec(a.shape, lambda r: (0, 0))
    return pl.pallas_call(
        _out_proj_kernel,
        grid=(n // tm,),
        in_specs=[row(d), row(ATT_WIDTH), row(RET_V_WIDTH), const(wa), const(wr), const(gain)],
        out_specs=row(d),
        out_shape=jax.ShapeDtypeStruct((n, d), jnp.float32),
        compiler_params=pltpu.CompilerParams(
            dimension_semantics=("arbitrary",), vmem_limit_bytes=VMEM_LIMIT_BYTES),
        name="out_proj",
    )(x2, ya2, yr2, wa, wr, gain)


def _prep_weight_kernel(w_ref, out_ref):
    offs = np.concatenate([[0], np.cumsum(SPLITS)]).tolist()
    q_a, k_a, v_a, g_a, q_i, k_i, w_i, q_r, k_r, v_r, g_r = [
        slice(offs[j], offs[j + 1]) for j in range(len(SPLITS))]
    dst = 0
    for src_cols in (q_a, k_a, v_a, g_a, q_i, q_r, k_r, v_r, g_r, k_i, k_i, w_i):
        width = src_cols.stop - src_cols.start
        out_ref[:, dst:dst + width] = w_ref[:, src_cols].astype(out_ref.dtype)
        dst += width
    out_ref[:, dst:] = jnp.zeros((out_ref.shape[0], PROJ_WIDTH - dst), out_ref.dtype)


def _prep_weight(w_in):
    d, width = w_in.shape
    tr = 128
    return pl.pallas_call(
        _prep_weight_kernel,
        grid=(d // tr,),
        in_specs=[pl.BlockSpec((tr, width), lambda r: (r, 0))],
        out_specs=pl.BlockSpec((tr, PROJ_WIDTH), lambda r: (r, 0)),
        out_shape=jax.ShapeDtypeStruct((d, PROJ_WIDTH), jnp.bfloat16),
        compiler_params=pltpu.CompilerParams(
            dimension_semantics=("arbitrary",), vmem_limit_bytes=VMEM_LIMIT_BYTES),
        name="prep_weight",
    )(w_in)


def kernel(x, norm_gain, w_in, w_out, final_gain):
    b, s, d = x.shape
    depth = norm_gain.shape[0]
    assert d == D_MODEL and s % Q_BLOCK == 0 and w_in.shape[2] == sum(SPLITS)
    topk = min(TOPK_MAX, s // 4)
    rope_a = _rotary_tables(s, ATT_HEAD_DIM, ATT_ROPE_DIM, ROPE_THETA)
    rope_r = _rotary_tables(s, RET_QK_DIM, RET_QK_DIM, RET_THETA)
    ret_tables = _retention_tables()
    assert depth == 1, "the final norm is fused into the single layer's output projection"
    w = _prep_weight(w_in[0])
    qa, ka, vt, ga, qi, qr, kr, vr, gr, ki, wi = _in_proj(
        x, norm_gain[0][None, :], w, rope_a, rope_r)
    ya = _dsa(qi, wi, ki, qa, ka, vt, ga, topk)
    yr = _retention(qr, kr, vr, gr, ret_tables)
    wo = w_out[0].astype(jnp.bfloat16)
    out = _out_proj(x.reshape(b * s, d), ya.reshape(b * s, ATT_WIDTH),
                    yr.reshape(b * s, RET_V_WIDTH), wo[:ATT_WIDTH], wo[ATT_WIDTH:],
                    final_gain[None, :], 512)
    return out.reshape(b, s, d)
```

```python
import functools

import jax
import jax.numpy as jnp
import numpy as np
from jax import lax
from jax.experimental import pallas as pl
from jax.experimental.pallas import tpu as pltpu

D_MODEL = 1024
ATT_HEADS = 8
ATT_HEAD_DIM = 64
ATT_WIDTH = ATT_HEADS * ATT_HEAD_DIM
ATT_ROPE_DIM = ATT_HEAD_DIM // 4
ROPE_THETA = 500000.0
IDX_HEADS = 4
IDX_DIM = 64
IDX_WIDTH = IDX_HEADS * IDX_DIM
IDX_ROPE_DIM = IDX_DIM // 4
TOPK_MAX = 256
Q_BLOCK = 256
RET_HEADS = 4
RET_QK_DIM = 64
RET_V_DIM = 128
RET_QK_WIDTH = RET_HEADS * RET_QK_DIM
RET_V_WIDTH = RET_HEADS * RET_V_DIM
RET_CHUNK = 128
RET_THETA = 10000.0
MIX_WIDTH = ATT_WIDTH + RET_V_WIDTH
SPLITS = (ATT_WIDTH, ATT_WIDTH, ATT_WIDTH, ATT_WIDTH, IDX_WIDTH, IDX_DIM, IDX_HEADS,
          RET_QK_WIDTH, RET_QK_WIDTH, RET_V_WIDTH, RET_V_WIDTH)
EPS = 1e-6

LANES = 128
SUBLANES = 8
BF16_ROWS = 16
MXU_COLS = 256
LOG2_E = float(np.log2(np.e))
MASK_VALUE = -1e30
VMEM_LIMIT_BYTES = 48 * 1024 * 1024

OFF_QA = 0
OFF_KA = OFF_QA + ATT_WIDTH
OFF_VA = OFF_KA + ATT_WIDTH
OFF_GA = OFF_VA + ATT_WIDTH
OFF_QI = OFF_GA + ATT_WIDTH
OFF_QR = OFF_QI + IDX_WIDTH
OFF_KR = OFF_QR + RET_QK_WIDTH
OFF_VR = OFF_KR + RET_QK_WIDTH
OFF_GR = OFF_VR + RET_V_WIDTH
OFF_KI = OFF_GR + RET_V_WIDTH
OFF_WI = OFF_KI + LANES
PROJ_WIDTH = OFF_WI + LANES
KEY_CHUNK = 256
RET_STEP_CHUNKS = 4
COUNT_ROWS = 32


def _rotary_tables(seq, head_dim, rot_dim, theta):
    half = rot_dim // 2
    inv = 1.0 / (theta ** (jnp.arange(half, dtype=jnp.float32) / half))
    ang = jnp.arange(seq).astype(jnp.float32)[:, None] * inv[None, :]
    cos = jnp.cos(ang)
    sin = jnp.sin(ang)
    pad = head_dim - rot_dim
    ones = jnp.ones((seq, pad), jnp.float32)
    zeros = jnp.zeros((seq, pad), jnp.float32)
    zh = jnp.zeros((seq, half), jnp.float32)
    c = jnp.concatenate([cos, cos, ones], axis=1)
    s_prev = jnp.concatenate([zh, sin, zeros], axis=1)
    s_next = jnp.concatenate([-sin, zh, zeros], axis=1)
    reps = LANES // head_dim
    return jnp.stack([jnp.tile(c, (1, reps)), jnp.tile(s_prev, (1, reps)),
                      jnp.tile(s_next, (1, reps))])


def _retention_tables():
    c = RET_CHUNK
    gamma = 1.0 - 2.0 ** (-5.0 - jnp.arange(RET_HEADS, dtype=jnp.float32))
    log_g = jnp.log(gamma)
    idx = jnp.arange(c, dtype=jnp.float32)
    diff = idx[:, None] - idx[None, :]
    decay = jnp.where(diff[None] >= 0,
                      jnp.exp(log_g[:, None, None] * jnp.maximum(diff, 0.0)[None]), 0.0)
    zeta = jnp.exp(log_g[:, None] * (c - 1.0 - idx)[None, :])
    xi = jnp.exp(log_g[:, None] * (idx + 1.0)[None, :])
    g_chunk = jnp.exp(log_g * c)
    zeta_b = jnp.broadcast_to(zeta[:, :, None], (RET_HEADS, c, LANES))
    xi_b = jnp.broadcast_to(xi[:, :, None], (RET_HEADS, c, LANES))
    g_rows = jnp.repeat(g_chunk, RET_QK_DIM)[:, None]
    g_rows = jnp.broadcast_to(g_rows, (RET_HEADS * RET_QK_DIM, RET_V_DIM))
    return decay, zeta_b, xi_b, g_rows


def _rope(z, tab_ref, half):
    return (z * tab_ref[0] + pltpu.roll(z, half, 1) * tab_ref[1]
            + pltpu.roll(z, LANES - half, 1) * tab_ref[2])


def _silu(g):
    return g * (1.0 / (1.0 + jnp.exp(-g)))


def _in_proj_kernel(x_ref, gain_ref, w_ref, rope_a_ref, rope_r_ref,
                    qa_ref, ka_ref, vt_ref, ga_ref, qi_ref, qr_ref, kr_ref, vr_ref, gr_ref,
                    ki_ref, wi_ref, u_ref):
    x = x_ref[0]
    ms = jnp.mean(x * x, axis=-1, keepdims=True)
    u_ref[...] = ((x * lax.rsqrt(ms + EPS)) * gain_ref[...]).astype(jnp.bfloat16)

    def proj(off, width):
        out = []
        for j in range(width // MXU_COLS):
            c0 = off + j * MXU_COLS
            z = jnp.dot(u_ref[...], w_ref[:, c0:c0 + MXU_COLS], preferred_element_type=jnp.float32)
            for h in range(MXU_COLS // LANES):
                lo = j * MXU_COLS + h * LANES
                out.append((slice(lo, lo + LANES), z[:, h * LANES:(h + 1) * LANES]))
        return out

    a_half = ATT_ROPE_DIM // 2
    r_half = RET_QK_DIM // 2
    att_scale = ATT_HEAD_DIM ** -0.5 * LOG2_E
    ret_scale = RET_QK_DIM ** -0.5
    for sl, z in proj(OFF_QA, ATT_WIDTH):
        qa_ref[0, :, sl] = (_rope(z, rope_a_ref, a_half) * att_scale).astype(qa_ref.dtype)
    for sl, z in proj(OFF_KA, ATT_WIDTH):
        ka_ref[0, :, sl] = _rope(z, rope_a_ref, a_half).astype(ka_ref.dtype)
    for sl, z in proj(OFF_VA, ATT_WIDTH):
        vt_ref[0, 0, sl, :] = z.T.astype(vt_ref.dtype)
    for sl, z in proj(OFF_GA, ATT_WIDTH):
        ga_ref[0, :, sl] = _silu(z).astype(ga_ref.dtype)
    for sl, z in proj(OFF_QI, IDX_WIDTH):
        qi_ref[0, :, sl] = _rope(z, rope_a_ref, a_half).astype(qi_ref.dtype)
    for sl, z in proj(OFF_QR, RET_QK_WIDTH):
        qr_ref[0, :, sl] = _rope(z, rope_r_ref, r_half).astype(qr_ref.dtype)
    for sl, z in proj(OFF_KR, RET_QK_WIDTH):
        kr_ref[0, :, sl] = (_rope(z, rope_r_ref, r_half) * ret_scale).astype(kr_ref.dtype)
    for sl, z in proj(OFF_VR, RET_V_WIDTH):
        vr_ref[0, :, sl] = z.astype(vr_ref.dtype)
    for sl, z in proj(OFF_GR, RET_V_WIDTH):
        gr_ref[0, :, sl] = _silu(z).astype(gr_ref.dtype)
    (_, ki), (_, wi) = proj(OFF_KI, 2 * LANES)
    ki_ref[0] = _rope(ki, rope_a_ref, a_half).astype(ki_ref.dtype)
    wi_ref[0] = wi * ((IDX_DIM ** -0.5) * (IDX_HEADS ** -0.5))


def _in_proj(x, gain, w, rope_a, rope_r):
    b, s, d = x.shape
    tm = KEY_CHUNK
    row = lambda width: pl.BlockSpec((1, tm, width), lambda si, bi: (bi, si, 0))
    tab = pl.BlockSpec((3, tm, LANES), lambda si, bi: (0, si, 0))
    vt_spec = pl.BlockSpec((1, 1, ATT_WIDTH, tm), lambda si, bi: (bi, si, 0, 0))
    bf = jnp.bfloat16
    out_shapes = [
        jax.ShapeDtypeStruct((b, s, ATT_WIDTH), bf),
        jax.ShapeDtypeStruct((b, s, ATT_WIDTH), bf),
        jax.ShapeDtypeStruct((b, s // tm, ATT_WIDTH, tm), bf),
        jax.ShapeDtypeStruct((b, s, ATT_WIDTH), bf),
        jax.ShapeDtypeStruct((b, s, IDX_WIDTH), bf),
        jax.ShapeDtypeStruct((b, s, RET_QK_WIDTH), bf),
        jax.ShapeDtypeStruct((b, s, RET_QK_WIDTH), bf),
        jax.ShapeDtypeStruct((b, s, RET_V_WIDTH), bf),
        jax.ShapeDtypeStruct((b, s, RET_V_WIDTH), bf),
        jax.ShapeDtypeStruct((b, s, LANES), bf),
        jax.ShapeDtypeStruct((b, s, LANES), jnp.float32),
    ]
    out_specs = [row(ATT_WIDTH), row(ATT_WIDTH), vt_spec, row(ATT_WIDTH),
                 row(IDX_WIDTH), row(RET_QK_WIDTH), row(RET_QK_WIDTH),
                 row(RET_V_WIDTH), row(RET_V_WIDTH), row(LANES), row(LANES)]
    return pl.pallas_call(
        _in_proj_kernel,
        grid=(s // tm, b),
        in_specs=[row(d),
                  pl.BlockSpec((1, d), lambda si, bi: (0, 0)),
                  pl.BlockSpec((d, PROJ_WIDTH), lambda si, bi: (0, 0)),
                  tab, tab],
        out_specs=out_specs,
        out_shape=out_shapes,
        scratch_shapes=[pltpu.VMEM((tm, d), jnp.bfloat16)],
        compiler_params=pltpu.CompilerParams(
            dimension_semantics=("arbitrary", "arbitrary"), vmem_limit_bytes=VMEM_LIMIT_BYTES),
        name="in_proj",
    )(x, gain, w, rope_a, rope_r)


_NT = (((1,), (1,)), ((), ()))


def _key_to_float(key):
    k = key ^ jnp.int32(-2 ** 31)
    bits = jnp.where(k >= 0, k, k ^ jnp.int32(2 ** 31 - 1))
    return lax.bitcast_convert_type(bits, jnp.float32)


def _split_heads_t(blk):
    blk_t = blk.astype(jnp.float32).T
    feat = lax.broadcasted_iota(jnp.int32, blk_t.shape, 0)
    zero = jnp.zeros_like(blk_t)
    both = jnp.concatenate([jnp.where(feat < LANES // 2, blk_t, zero),
                            jnp.where(feat >= LANES // 2, blk_t, zero)], axis=1)
    return both.astype(jnp.bfloat16)


def _dsa_kernel(qi_ref, wi_ref, ki_ref, qa_ref, ka_ref, vt_ref, ga_ref,
                out_ref, sc_ref, bias_ref, s_a, s_b, p_a, p_b, acc_ref, *, topk):
    i = pl.program_id(1)
    n_kc = sc_ref.shape[0]
    n_chunks = ((i + 1) * Q_BLOCK + KEY_CHUNK - 1) // KEY_CHUNK
    q_pos = i * Q_BLOCK + lax.broadcasted_iota(jnp.int32, (1, Q_BLOCK), 1)
    k_off = lax.broadcasted_iota(jnp.int32, (KEY_CHUNK, 1), 0)

    qi = qi_ref[0]
    qi_pairs = [_split_heads_t(qi[:, p * LANES:(p + 1) * LANES]) for p in range(IDX_HEADS // 2)]
    w_t = wi_ref[0].T
    w_rows = [w_t[h:h + 1, :] for h in range(IDX_HEADS)]

    s_bufs = (s_a, s_b)
    n_steps = (n_chunks + 1) // 2

    def logits_to(c, buf):
        start = pl.multiple_of(c * KEY_CHUNK, KEY_CHUNK)
        ki = ki_ref[0, pl.ds(start, KEY_CHUNK), :]
        for p in range(IDX_HEADS // 2):
            buf[p] = jnp.dot(ki, qi_pairs[p], preferred_element_type=jnp.float32)

    def score_stage(c, cur):
        logits_to(jnp.minimum(c + 1, n_kc - 1), s_bufs[1 - cur])
        buf = s_bufs[cur]
        acc = None
        for h in range(IDX_HEADS):
            lanes = slice((h % 2) * Q_BLOCK, (h % 2 + 1) * Q_BLOCK)
            term = w_rows[h] * jnp.maximum(buf[h // 2, :, lanes], 0.0)
            acc = term if acc is None else acc + term
        sc_ref[c] = jnp.where(c * KEY_CHUNK + k_off <= q_pos, acc, -jnp.inf)

    def score_step(j, carry):
        score_stage(2 * j, 0)
        score_stage(2 * j + 1, 1)
        return carry

    logits_to(0, s_a)
    lax.fori_loop(0, n_steps, score_step, 0)

    def count(pred):
        rows = COUNT_ROWS

        def body(j, acc):
            for c in (2 * j, 2 * j + 1):
                hit = jnp.where(pred(sc_ref[c]), 1.0, 0.0)
                for r in range(KEY_CHUNK // rows):
                    acc = acc + hit[r * rows:(r + 1) * rows]
            return acc
        acc = lax.fori_loop(0, n_steps, body, jnp.zeros((rows, Q_BLOCK), jnp.float32))
        return jnp.sum(acc, axis=0, keepdims=True)

    @pl.when((i + 1) * Q_BLOCK <= topk)
    def _():
        bias_ref[0] = jnp.where(k_off <= q_pos, 0.0, MASK_VALUE)

    @pl.when((i + 1) * Q_BLOCK > topk)
    def _():
        def bit_step(b, key):
            cand = key | jnp.left_shift(jnp.int32(1), 31 - b)
            f = _key_to_float(cand)
            total = count(lambda s: s >= f)
            return jnp.where(total >= topk, cand, key)

        key = lax.fori_loop(0, 32, bit_step, jnp.zeros((1, Q_BLOCK), jnp.int32))
        thr = _key_to_float(key)
        need = topk - count(lambda s: s > thr)

        n_groups = KEY_CHUNK // SUBLANES
        sub = lax.broadcasted_iota(jnp.int32, (1, SUBLANES, Q_BLOCK), 1)
        keep = {k: jnp.where(sub >= k, 1.0, 0.0) for k in (1, 2, 4)}

        def body(c, run):
            s = sc_ref[c]
            eq = s == thr
            cnt = jnp.where(eq, 1.0, 0.0)
            for k in (1, 2, 4):
                shifted = pltpu.roll(cnt, k, 0).reshape(n_groups, SUBLANES, Q_BLOCK) * keep[k]
                cnt = cnt + shifted.reshape(KEY_CHUNK, Q_BLOCK)
            cnt3 = cnt.reshape(n_groups, SUBLANES, Q_BLOCK)
            groups = []
            for g in range(n_groups):
                groups.append(cnt3[g] + run)
                run = run + cnt3[g, SUBLANES - 1:SUBLANES, :]
            incl = jnp.concatenate(groups, axis=0)
            sel = (s > thr) | (eq & (incl <= need))
            bias_ref[c] = jnp.where(sel, 0.0, MASK_VALUE)
            return run

        lax.fori_loop(0, n_chunks, body, jnp.zeros((1, Q_BLOCK), jnp.float32))

    n_pairs = ATT_HEADS // 2
    q_pairs = [_split_heads_t(qa_ref[0, :, p * LANES:(p + 1) * LANES]) for p in range(n_pairs)]
    half = ATT_HEAD_DIM

    pairs = range(n_pairs)
    p_bufs = (p_a, p_b)
    bias_ref[n_chunks] = jnp.full((KEY_CHUNK, Q_BLOCK), MASK_VALUE, jnp.float32)

    def scores_to(c, s_buf):
        start = pl.multiple_of(c * KEY_CHUNK, KEY_CHUNK)
        bias = bias_ref[c]
        bias2 = jnp.concatenate([bias, bias], axis=1)
        cmax = []
        for p in pairs:
            s = jnp.dot(ka_ref[0, pl.ds(start, KEY_CHUNK), p * LANES:(p + 1) * LANES], q_pairs[p],
                        preferred_element_type=jnp.float32) + bias2
            s_buf[p] = s
            cmax.append(jnp.max(s, axis=0, keepdims=True))
        return cmax

    ones_rows = jnp.ones((BF16_ROWS, KEY_CHUNK), jnp.bfloat16)

    def weighted_values(c, p_buf):
        return [jnp.dot(jnp.concatenate([vt_ref[0, c, p * LANES:(p + 1) * LANES, :], ones_rows], axis=0),
                        p_buf[p], preferred_element_type=jnp.float32) for p in pairs]

    def accumulate(l, alpha, o):
        l_new = []
        for p in pairs:
            acc_ref[p, 0] = alpha[p][:, :Q_BLOCK] * acc_ref[p, 0] + o[p][:half, :Q_BLOCK]
            acc_ref[p, 1] = alpha[p][:, Q_BLOCK:] * acc_ref[p, 1] + o[p][half:2 * half, Q_BLOCK:]
            l_new.append(alpha[p] * l[p] + o[p][2 * half:2 * half + 1, :])
        return l_new

    def softmax_to(s_buf, p_buf, cmax, m):
        m_new = [jnp.maximum(m[p], cmax[p]) for p in pairs]
        alpha = [jnp.exp2(m[p] - m_new[p]) for p in pairs]
        for p in pairs:
            p_buf[p] = jnp.exp2(s_buf[p] - m_new[p]).astype(jnp.bfloat16)
        return alpha, m_new

    def stage(c, cur, carry):
        cmax, alpha_prev, m, l = carry
        other = 1 - cur
        cmax_next = scores_to(jnp.minimum(c + 1, n_kc - 1), s_bufs[other])
        o_prev = weighted_values(jnp.maximum(c - 1, 0), p_bufs[other])
        alpha, m = softmax_to(s_bufs[cur], p_bufs[cur], cmax, m)
        return cmax_next, alpha, m, accumulate(l, alpha_prev, o_prev)

    def att_step(j, carry):
        return stage(2 * j + 1, 1, stage(2 * j, 0, carry))

    zeros_row = jnp.zeros((1, 2 * Q_BLOCK), jnp.float32)
    p_b[...] = jnp.zeros(p_b.shape, p_b.dtype)
    acc_ref[...] = jnp.zeros(acc_ref.shape, acc_ref.dtype)
    carry = (scores_to(0, s_a), [zeros_row + 1.0 for _ in pairs], [zeros_row + MASK_VALUE for _ in pairs],
             [zeros_row for _ in pairs])
    _, alpha_last, _, l = lax.fori_loop(0, n_steps, att_step, carry)
    l = accumulate(l, alpha_last, weighted_values(2 * n_steps - 1, p_b))
    for p in pairs:
        a0, a1 = acc_ref[p, 0], acc_ref[p, 1]
        psl = slice(p * LANES, (p + 1) * LANES)
        y_t = jnp.concatenate([a0 / l[p][:, :Q_BLOCK], a1 / l[p][:, Q_BLOCK:]], axis=0)
        out_ref[0, :, psl] = (y_t.T * ga_ref[0, :, psl].astype(jnp.float32)).astype(out_ref.dtype)


def _dsa(qi, wi, ki, qa, ka, vt, ga, topk):
    b, s, _ = qa.shape
    n_kc = s // KEY_CHUNK
    att_buf = (ATT_HEADS // 2, KEY_CHUNK, 2 * Q_BLOCK)
    qblk = lambda width: pl.BlockSpec((1, Q_BLOCK, width), lambda bi, i: (bi, i, 0))
    full = lambda width: pl.BlockSpec((1, s, width), lambda bi, i: (bi, 0, 0))
    return pl.pallas_call(
        functools.partial(_dsa_kernel, topk=topk),
        grid=(b, s // Q_BLOCK),
        in_specs=[qblk(IDX_WIDTH), qblk(LANES), full(LANES), qblk(ATT_WIDTH), full(ATT_WIDTH),
                  pl.BlockSpec((1, n_kc, ATT_WIDTH, KEY_CHUNK), lambda bi, i: (bi, 0, 0, 0)),
                  qblk(ATT_WIDTH)],
        out_specs=qblk(ATT_WIDTH),
        out_shape=jax.ShapeDtypeStruct((b, s, ATT_WIDTH), jnp.bfloat16),
        scratch_shapes=[pltpu.VMEM((n_kc, KEY_CHUNK, Q_BLOCK), jnp.float32),
                        pltpu.VMEM((n_kc + 1, KEY_CHUNK, Q_BLOCK), jnp.float32),
                        pltpu.VMEM(att_buf, jnp.float32), pltpu.VMEM(att_buf, jnp.float32),
                        pltpu.VMEM(att_buf, jnp.bfloat16), pltpu.VMEM(att_buf, jnp.bfloat16),
                        pltpu.VMEM((ATT_HEADS // 2, 2, ATT_HEAD_DIM, Q_BLOCK), jnp.float32)],
        compiler_params=pltpu.CompilerParams(
            dimension_semantics=("arbitrary", "arbitrary"), vmem_limit_bytes=VMEM_LIMIT_BYTES),
        name="sparse_attention",
    )(qi, wi, ki, qa, ka, vt, ga)


_TN = (((0,), (0,)), ((), ()))


def _retention_kernel(qr_ref, kr_ref, vr_ref, gr_ref, decay_ref, zeta_ref, xi_ref, grow_ref,
                      out_ref, state_ref):
    @pl.when(pl.program_id(1) == 0)
    def _():
        state_ref[...] = jnp.zeros_like(state_ref)

    lane = lax.broadcasted_iota(jnp.int32, (RET_CHUNK, LANES), 1)
    for pair in range(RET_HEADS // 2):
        psl = slice(pair * LANES, (pair + 1) * LANES)
        state = state_ref[psl, :]
        for r in range(RET_STEP_CHUNKS):
            rows = slice(r * RET_CHUNK, (r + 1) * RET_CHUNK)
            q_pair = qr_ref[0, rows, psl].astype(jnp.float32)
            k_pair = kr_ref[0, rows, psl].astype(jnp.float32)
            state_b = state.astype(jnp.bfloat16)
            kv = jnp.zeros((LANES, RET_V_DIM), jnp.float32)
            for sub in range(2):
                h = 2 * pair + sub
                vsl = slice(h * RET_V_DIM, (h + 1) * RET_V_DIM)
                in_head = (lane >= sub * RET_QK_DIM) & (lane < (sub + 1) * RET_QK_DIM)
                q_h = jnp.where(in_head, q_pair, 0.0)
                k_h = jnp.where(in_head, k_pair, 0.0)
                v = vr_ref[0, rows, vsl]
                scores = lax.dot_general(q_h.astype(jnp.bfloat16), k_h.astype(jnp.bfloat16), _NT,
                                         preferred_element_type=jnp.float32) * decay_ref[h]
                inner = jnp.dot(scores.astype(jnp.bfloat16), v, preferred_element_type=jnp.float32)
                cross = jnp.dot((q_h * xi_ref[h]).astype(jnp.bfloat16), state_b,
                                preferred_element_type=jnp.float32)
                o = inner + cross
                o = o * lax.rsqrt(jnp.mean(o * o, axis=-1, keepdims=True) + EPS)
                out_ref[0, rows, vsl] = (o * gr_ref[0, rows, vsl].astype(jnp.float32)).astype(out_ref.dtype)
                kv = kv + lax.dot_general((k_h * zeta_ref[h]).astype(jnp.bfloat16), v, _TN,
                                          preferred_element_type=jnp.float32)
            state = state * grow_ref[psl, :] + kv
        state_ref[psl, :] = state


def _retention(qr, kr, vr, gr, tables):
    b, s, _ = vr.shape
    decay, zeta_b, xi_b, g_rows = tables
    rows = RET_CHUNK * RET_STEP_CHUNKS
    blk = lambda width: pl.BlockSpec((1, rows, width), lambda bi, i: (bi, i, 0))
    const3 = lambda a: pl.BlockSpec(a.shape, lambda bi, i: (0, 0, 0))
    return pl.pallas_call(
        _retention_kernel,
        grid=(b, s // rows),
        in_specs=[blk(RET_QK_WIDTH), blk(RET_QK_WIDTH), blk(RET_V_WIDTH), blk(RET_V_WIDTH),
                  const3(decay), const3(zeta_b), const3(xi_b),
                  pl.BlockSpec(g_rows.shape, lambda bi, i: (0, 0))],
        out_specs=blk(RET_V_WIDTH),
        out_shape=jax.ShapeDtypeStruct((b, s, RET_V_WIDTH), jnp.bfloat16),
        scratch_shapes=[pltpu.VMEM((RET_HEADS * RET_QK_DIM, RET_V_DIM), jnp.float32)],
        compiler_params=pltpu.CompilerParams(
            dimension_semantics=("arbitrary", "arbitrary"), vmem_limit_bytes=VMEM_LIMIT_BYTES),
        name="retention",
    )(qr, kr, vr, gr, decay, zeta_b, xi_b, g_rows)


def _out_proj_kernel(x_ref, ya_ref, yr_ref, wa_ref, wr_ref, gain_ref, out_ref):
    h = (x_ref[...]
         + jnp.dot(ya_ref[...], wa_ref[...], preferred_element_type=jnp.float32)
         + jnp.dot(yr_ref[...], wr_ref[...], preferred_element_type=jnp.float32))
    ms = jnp.mean(h * h, axis=-1, keepdims=True)
    out_ref[...] = (h * lax.rsqrt(ms + EPS)) * gain_ref[...]


def _out_proj(x2, ya2, yr2, wa, wr, gain, tm):
    n, d = x2.shape
    row = lambda width: pl.BlockSpec((tm, width), lambda r: (r, 0))
    const = lambda a: pl.BlockSpec(a.shape, lambda r: (0, 0))
    return pl.pallas_call(
        _out_proj_kernel,
        grid=(n // tm,),
        in_specs=[row(d), row(ATT_WIDTH), row(RET_V_WIDTH), const(wa), const(wr), const(gain)],
        out_specs=row(d),
        out_shape=jax.ShapeDtypeStruct((n, d), jnp.float32),
        compiler_params=pltpu.CompilerParams(
            dimension_semantics=("arbitrary",), vmem_limit_bytes=VMEM_LIMIT_BYTES),
        name="out_proj",
    )(x2, ya2, yr2, wa, wr, gain)


def _prep_weight_kernel(w_ref, out_ref):
    offs = np.concatenate([[0], np.cumsum(SPLITS)]).tolist()
    q_a, k_a, v_a, g_a, q_i, k_i, w_i, q_r, k_r, v_r, g_r = [
        slice(offs[j], offs[j + 1]) for j in range(len(SPLITS))]
    dst = 0
    for src_cols in (q_a, k_a, v_a, g_a, q_i, q_r, k_r, v_r, g_r, k_i, k_i, w_i):
        width = src_cols.stop - src_cols.start
        out_ref[:, dst:dst + width] = w_ref[:, src_cols].astype(out_ref.dtype)
        dst += width
    out_ref[:, dst:] = jnp.zeros((out_ref.shape[0], PROJ_WIDTH - dst), out_ref.dtype)


def _prep_weight(w_in):
    d, width = w_in.shape
    tr = 128
    return pl.pallas_call(
        _prep_weight_kernel,
        grid=(d // tr,),
        in_specs=[pl.BlockSpec((tr, width), lambda r: (r, 0))],
        out_specs=pl.BlockSpec((tr, PROJ_WIDTH), lambda r: (r, 0)),
        out_shape=jax.ShapeDtypeStruct((d, PROJ_WIDTH), jnp.bfloat16),
        compiler_params=pltpu.CompilerParams(
            dimension_semantics=("arbitrary",), vmem_limit_bytes=VMEM_LIMIT_BYTES),
        name="prep_weight",
    )(w_in)


def kernel(x, norm_gain, w_in, w_out, final_gain):
    b, s, d = x.shape
    depth = norm_gain.shape[0]
    assert d == D_MODEL and s % Q_BLOCK == 0 and w_in.shape[2] == sum(SPLITS)
    topk = min(TOPK_MAX, s // 4)
    rope_a = _rotary_tables(s, ATT_HEAD_DIM, ATT_ROPE_DIM, ROPE_THETA)
    rope_r = _rotary_tables(s, RET_QK_DIM, RET_QK_DIM, RET_THETA)
    ret_tables = _retention_tables()
    assert depth == 1, "the final norm is fused into the single layer's output projection"
    w = _prep_weight(w_in[0])
    qa, ka, vt, ga, qi, qr, kr, vr, gr, ki, wi = _in_proj(
        x, norm_gain[0][None, :], w, rope_a, rope_r)
    ya = _dsa(qi, wi, ki, qa, ka, vt, ga, topk)
    yr = _retention(qr, kr, vr, gr, ret_tables)
    wo = w_out[0].astype(jnp.bfloat16)
    out = _out_proj(x.reshape(b * s, d), ya.reshape(b * s, ATT_WIDTH),
                    yr.reshape(b * s, RET_V_WIDTH), wo[:ATT_WIDTH], wo[ATT_WIDTH:],
                    final_gain[None, :], 512)
    return out.reshape(b, s, d)
```

```python
import functools

import jax
import jax.numpy as jnp
import numpy as np
from jax import lax
from jax.experimental import pallas as pl
from jax.experimental.pallas import tpu as pltpu

D_MODEL = 1024
ATT_HEADS = 8
ATT_HEAD_DIM = 64
ATT_WIDTH = ATT_HEADS * ATT_HEAD_DIM
ATT_ROPE_DIM = ATT_HEAD_DIM // 4
ROPE_THETA = 500000.0
IDX_HEADS = 4
IDX_DIM = 64
IDX_WIDTH = IDX_HEADS * IDX_DIM
IDX_ROPE_DIM = IDX_DIM // 4
TOPK_MAX = 256
Q_BLOCK = 256
RET_HEADS = 4
RET_QK_DIM = 64
RET_V_DIM = 128
RET_QK_WIDTH = RET_HEADS * RET_QK_DIM
RET_V_WIDTH = RET_HEADS * RET_V_DIM
RET_CHUNK = 128
RET_THETA = 10000.0
MIX_WIDTH = ATT_WIDTH + RET_V_WIDTH
SPLITS = (ATT_WIDTH, ATT_WIDTH, ATT_WIDTH, ATT_WIDTH, IDX_WIDTH, IDX_DIM, IDX_HEADS,
          RET_QK_WIDTH, RET_QK_WIDTH, RET_V_WIDTH, RET_V_WIDTH)
EPS = 1e-6

LANES = 128
SUBLANES = 8
BF16_ROWS = 16
MXU_COLS = 256
LOG2_E = float(np.log2(np.e))
MASK_VALUE = -1e30
VMEM_LIMIT_BYTES = 48 * 1024 * 1024

OFF_QA = 0
OFF_KA = OFF_QA + ATT_WIDTH
OFF_VA = OFF_KA + ATT_WIDTH
OFF_GA = OFF_VA + ATT_WIDTH
OFF_QI = OFF_GA + ATT_WIDTH
OFF_QR = OFF_QI + IDX_WIDTH
OFF_KR = OFF_QR + RET_QK_WIDTH
OFF_VR = OFF_KR + RET_QK_WIDTH
OFF_GR = OFF_VR + RET_V_WIDTH
OFF_KI = OFF_GR + RET_V_WIDTH
OFF_WI = OFF_KI + LANES
PROJ_WIDTH = OFF_WI + LANES
KEY_CHUNK = 256
RET_STEP_CHUNKS = 8
OUT_PROJ_ROWS = 1024
COUNT_ROWS = 32


def _rotary_tables(seq, head_dim, rot_dim, theta):
    half = rot_dim // 2
    inv = 1.0 / (theta ** (jnp.arange(half, dtype=jnp.float32) / half))
    ang = jnp.arange(seq).astype(jnp.float32)[:, None] * inv[None, :]
    cos = jnp.cos(ang)
    sin = jnp.sin(ang)
    pad = head_dim - rot_dim
    ones = jnp.ones((seq, pad), jnp.float32)
    zeros = jnp.zeros((seq, pad), jnp.float32)
    zh = jnp.zeros((seq, half), jnp.float32)
    c = jnp.concatenate([cos, cos, ones], axis=1)
    s_prev = jnp.concatenate([zh, sin, zeros], axis=1)
    s_next = jnp.concatenate([-sin, zh, zeros], axis=1)
    reps = LANES // head_dim
    return jnp.stack([jnp.tile(c, (1, reps)), jnp.tile(s_prev, (1, reps)),
                      jnp.tile(s_next, (1, reps))])


def _retention_tables():
    c = RET_CHUNK
    gamma = 1.0 - 2.0 ** (-5.0 - jnp.arange(RET_HEADS, dtype=jnp.float32))
    log_g = jnp.log(gamma)
    idx = jnp.arange(c, dtype=jnp.float32)
    diff = idx[:, None] - idx[None, :]
    decay = jnp.where(diff[None] >= 0,
                      jnp.exp(log_g[:, None, None] * jnp.maximum(diff, 0.0)[None]), 0.0)
    zeta = jnp.exp(log_g[:, None] * (c - 1.0 - idx)[None, :])
    xi = jnp.exp(log_g[:, None] * (idx + 1.0)[None, :])
    g_chunk = jnp.exp(log_g * c)
    zeta_b = jnp.broadcast_to(zeta[:, :, None], (RET_HEADS, c, LANES))
    xi_b = jnp.broadcast_to(xi[:, :, None], (RET_HEADS, c, LANES))
    g_rows = jnp.repeat(g_chunk, RET_QK_DIM)[:, None]
    g_rows = jnp.broadcast_to(g_rows, (RET_HEADS * RET_QK_DIM, RET_V_DIM))
    return decay, zeta_b, xi_b, g_rows


def _rope(z, tab_ref, half):
    return (z * tab_ref[0] + pltpu.roll(z, half, 1) * tab_ref[1]
            + pltpu.roll(z, LANES - half, 1) * tab_ref[2])


def _silu(g):
    return g * (1.0 / (1.0 + jnp.exp(-g)))


def _in_proj_kernel(x_ref, gain_ref, w_ref, rope_a_ref, rope_r_ref,
                    qa_ref, ka_ref, vt_ref, ga_ref, qi_ref, qr_ref, kr_ref, vr_ref, gr_ref,
                    ki_ref, wi_ref, u_ref):
    x = x_ref[0]
    ms = jnp.mean(x * x, axis=-1, keepdims=True)
    u_ref[...] = ((x * lax.rsqrt(ms + EPS)) * gain_ref[...]).astype(jnp.bfloat16)

    def proj(off, width):
        out = []
        for j in range(width // MXU_COLS):
            c0 = off + j * MXU_COLS
            z = jnp.dot(u_ref[...], w_ref[:, c0:c0 + MXU_COLS], preferred_element_type=jnp.float32)
            for h in range(MXU_COLS // LANES):
                lo = j * MXU_COLS + h * LANES
                out.append((slice(lo, lo + LANES), z[:, h * LANES:(h + 1) * LANES]))
        return out

    a_half = ATT_ROPE_DIM // 2
    r_half = RET_QK_DIM // 2
    att_scale = ATT_HEAD_DIM ** -0.5 * LOG2_E
    ret_scale = RET_QK_DIM ** -0.5
    for sl, z in proj(OFF_QA, ATT_WIDTH):
        qa_ref[0, :, sl] = (_rope(z, rope_a_ref, a_half) * att_scale).astype(qa_ref.dtype)
    for sl, z in proj(OFF_KA, ATT_WIDTH):
        ka_ref[0, :, sl] = _rope(z, rope_a_ref, a_half).astype(ka_ref.dtype)
    for sl, z in proj(OFF_VA, ATT_WIDTH):
        vt_ref[0, 0, sl, :] = z.T.astype(vt_ref.dtype)
    for sl, z in proj(OFF_GA, ATT_WIDTH):
        ga_ref[0, :, sl] = _silu(z).astype(ga_ref.dtype)
    for sl, z in proj(OFF_QI, IDX_WIDTH):
        qi_ref[0, :, sl] = _rope(z, rope_a_ref, a_half).astype(qi_ref.dtype)
    for sl, z in proj(OFF_QR, RET_QK_WIDTH):
        qr_ref[0, :, sl] = _rope(z, rope_r_ref, r_half).astype(qr_ref.dtype)
    for sl, z in proj(OFF_KR, RET_QK_WIDTH):
        kr_ref[0, :, sl] = (_rope(z, rope_r_ref, r_half) * ret_scale).astype(kr_ref.dtype)
    for sl, z in proj(OFF_VR, RET_V_WIDTH):
        vr_ref[0, :, sl] = z.astype(vr_ref.dtype)
    for sl, z in proj(OFF_GR, RET_V_WIDTH):
        gr_ref[0, :, sl] = _silu(z).astype(gr_ref.dtype)
    (_, ki), (_, wi) = proj(OFF_KI, 2 * LANES)
    ki_ref[0] = _rope(ki, rope_a_ref, a_half).astype(ki_ref.dtype)
    wi_ref[0] = wi * ((IDX_DIM ** -0.5) * (IDX_HEADS ** -0.5))


def _in_proj(x, gain, w, rope_a, rope_r):
    b, s, d = x.shape
    tm = KEY_CHUNK
    row = lambda width: pl.BlockSpec((1, tm, width), lambda si, bi: (bi, si, 0))
    tab = pl.BlockSpec((3, tm, LANES), lambda si, bi: (0, si, 0))
    vt_spec = pl.BlockSpec((1, 1, ATT_WIDTH, tm), lambda si, bi: (bi, si, 0, 0))
    bf = jnp.bfloat16
    out_shapes = [
        jax.ShapeDtypeStruct((b, s, ATT_WIDTH), bf),
        jax.ShapeDtypeStruct((b, s, ATT_WIDTH), bf),
        jax.ShapeDtypeStruct((b, s // tm, ATT_WIDTH, tm), bf),
        jax.ShapeDtypeStruct((b, s, ATT_WIDTH), bf),
        jax.ShapeDtypeStruct((b, s, IDX_WIDTH), bf),
        jax.ShapeDtypeStruct((b, s, RET_QK_WIDTH), bf),
        jax.ShapeDtypeStruct((b, s, RET_QK_WIDTH), bf),
        jax.ShapeDtypeStruct((b, s, RET_V_WIDTH), bf),
        jax.ShapeDtypeStruct((b, s, RET_V_WIDTH), bf),
        jax.ShapeDtypeStruct((b, s, LANES), bf),
        jax.ShapeDtypeStruct((b, s, LANES), jnp.float32),
    ]
    out_specs = [row(ATT_WIDTH), row(ATT_WIDTH), vt_spec, row(ATT_WIDTH),
                 row(IDX_WIDTH), row(RET_QK_WIDTH), row(RET_QK_WIDTH),
                 row(RET_V_WIDTH), row(RET_V_WIDTH), row(LANES), row(LANES)]
    return pl.pallas_call(
        _in_proj_kernel,
        grid=(s // tm, b),
        in_specs=[row(d),
                  pl.BlockSpec((1, d), lambda si, bi: (0, 0)),
                  pl.BlockSpec((d, PROJ_WIDTH), lambda si, bi: (0, 0)),
                  tab, tab],
        out_specs=out_specs,
        out_shape=out_shapes,
        scratch_shapes=[pltpu.VMEM((tm, d), jnp.bfloat16)],
        compiler_params=pltpu.CompilerParams(
            dimension_semantics=("arbitrary", "arbitrary"), vmem_limit_bytes=VMEM_LIMIT_BYTES),
        name="in_proj",
    )(x, gain, w, rope_a, rope_r)


_NT = (((1,), (1,)), ((), ()))


def _key_to_float(key):
    k = key ^ jnp.int32(-2 ** 31)
    bits = jnp.where(k >= 0, k, k ^ jnp.int32(2 ** 31 - 1))
    return lax.bitcast_convert_type(bits, jnp.float32)


def _split_heads_t(blk):
    blk_t = blk.astype(jnp.float32).T
    feat = lax.broadcasted_iota(jnp.int32, blk_t.shape, 0)
    zero = jnp.zeros_like(blk_t)
    both = jnp.concatenate([jnp.where(feat < LANES // 2, blk_t, zero),
                            jnp.where(feat >= LANES // 2, blk_t, zero)], axis=1)
    return both.astype(jnp.bfloat16)


def _dsa_kernel(qi_ref, wi_ref, ki_ref, qa_ref, ka_ref, vt_ref, ga_ref,
                out_ref, sc_ref, bias_ref, s_a, s_b, p_a, p_b, acc_ref, *, topk):
    i = pl.program_id(1)
    n_kc = sc_ref.shape[0]
    n_chunks = ((i + 1) * Q_BLOCK + KEY_CHUNK - 1) // KEY_CHUNK
    q_pos = i * Q_BLOCK + lax.broadcasted_iota(jnp.int32, (1, Q_BLOCK), 1)
    k_off = lax.broadcasted_iota(jnp.int32, (KEY_CHUNK, 1), 0)

    qi = qi_ref[0]
    qi_pairs = [_split_heads_t(qi[:, p * LANES:(p + 1) * LANES]) for p in range(IDX_HEADS // 2)]
    q_pairs = [_split_heads_t(qa_ref[0, :, p * LANES:(p + 1) * LANES]) for p in range(ATT_HEADS // 2)]
    w_t = wi_ref[0].T
    w_rows = [w_t[h:h + 1, :] for h in range(IDX_HEADS)]

    s_bufs = (s_a, s_b)
    n_steps = (n_chunks + 1) // 2

    def logits_to(c, buf):
        start = pl.multiple_of(c * KEY_CHUNK, KEY_CHUNK)
        ki = ki_ref[0, pl.ds(start, KEY_CHUNK), :]
        for p in range(IDX_HEADS // 2):
            buf[p] = jnp.dot(ki, qi_pairs[p], preferred_element_type=jnp.float32)

    def score_stage(c, cur):
        logits_to(jnp.minimum(c + 1, n_kc - 1), s_bufs[1 - cur])
        buf = s_bufs[cur]
        acc = None
        for h in range(IDX_HEADS):
            lanes = slice((h % 2) * Q_BLOCK, (h % 2 + 1) * Q_BLOCK)
            term = w_rows[h] * jnp.maximum(buf[h // 2, :, lanes], 0.0)
            acc = term if acc is None else acc + term
        sc_ref[c] = jnp.where(c * KEY_CHUNK + k_off <= q_pos, acc, -jnp.inf)

    def score_step(j, carry):
        score_stage(2 * j, 0)
        score_stage(2 * j + 1, 1)
        return carry

    logits_to(0, s_a)
    lax.fori_loop(0, n_steps, score_step, 0)

    def count(pred):
        rows = COUNT_ROWS

        def body(j, acc):
            for c in (2 * j, 2 * j + 1):
                hit = jnp.where(pred(sc_ref[c]), 1.0, 0.0)
                for r in range(KEY_CHUNK // rows):
                    acc = acc + hit[r * rows:(r + 1) * rows]
            return acc
        acc = lax.fori_loop(0, n_steps, body, jnp.zeros((rows, Q_BLOCK), jnp.float32))
        return jnp.sum(acc, axis=0, keepdims=True)

    @pl.when((i + 1) * Q_BLOCK <= topk)
    def _():
        bias_ref[0] = jnp.where(k_off <= q_pos, 0.0, MASK_VALUE)

    @pl.when((i + 1) * Q_BLOCK > topk)
    def _():
        def bit_step(b, key):
            cand = key | jnp.left_shift(jnp.int32(1), 31 - b)
            f = _key_to_float(cand)
            total = count(lambda s: s >= f)
            return jnp.where(total >= topk, cand, key)

        key = lax.fori_loop(0, 32, bit_step, jnp.zeros((1, Q_BLOCK), jnp.int32))
        thr = _key_to_float(key)
        need = topk - count(lambda s: s > thr)

        n_groups = KEY_CHUNK // SUBLANES
        sub = lax.broadcasted_iota(jnp.int32, (1, SUBLANES, Q_BLOCK), 1)
        keep = {k: jnp.where(sub >= k, 1.0, 0.0) for k in (1, 2, 4)}

        def body(c, run):
            s = sc_ref[c]
            eq = s == thr
            cnt = jnp.where(eq, 1.0, 0.0)
            for k in (1, 2, 4):
                shifted = pltpu.roll(cnt, k, 0).reshape(n_groups, SUBLANES, Q_BLOCK) * keep[k]
                cnt = cnt + shifted.reshape(KEY_CHUNK, Q_BLOCK)
            cnt3 = cnt.reshape(n_groups, SUBLANES, Q_BLOCK)
            groups = []
            for g in range(n_groups):
                groups.append(cnt3[g] + run)
                run = run + cnt3[g, SUBLANES - 1:SUBLANES, :]
            incl = jnp.concatenate(groups, axis=0)
            sel = (s > thr) | (eq & (incl <= need))
            bias_ref[c] = jnp.where(sel, 0.0, MASK_VALUE)
            return run

        lax.fori_loop(0, n_chunks, body, jnp.zeros((1, Q_BLOCK), jnp.float32))

    n_pairs = ATT_HEADS // 2
    half = ATT_HEAD_DIM

    pairs = range(n_pairs)
    p_bufs = (p_a, p_b)
    bias_ref[n_chunks] = jnp.full((KEY_CHUNK, Q_BLOCK), MASK_VALUE, jnp.float32)

    def scores_to(c, s_buf):
        start = pl.multiple_of(c * KEY_CHUNK, KEY_CHUNK)
        bias = bias_ref[c]
        bias2 = jnp.concatenate([bias, bias], axis=1)
        cmax = []
        for p in pairs:
            s = jnp.dot(ka_ref[0, pl.ds(start, KEY_CHUNK), p * LANES:(p + 1) * LANES], q_pairs[p],
                        preferred_element_type=jnp.float32) + bias2
            s_buf[p] = s
            cmax.append(jnp.max(s, axis=0, keepdims=True))
        return cmax

    ones_rows = jnp.ones((BF16_ROWS, KEY_CHUNK), jnp.bfloat16)

    def weighted_values(c, p_buf):
        return [jnp.dot(jnp.concatenate([vt_ref[0, c, p * LANES:(p + 1) * LANES, :], ones_rows], axis=0),
                        p_buf[p], preferred_element_type=jnp.float32) for p in pairs]

    def accumulate(l, alpha, o):
        l_new = []
        for p in pairs:
            acc_ref[p, 0] = alpha[p][:, :Q_BLOCK] * acc_ref[p, 0] + o[p][:half, :Q_BLOCK]
            acc_ref[p, 1] = alpha[p][:, Q_BLOCK:] * acc_ref[p, 1] + o[p][half:2 * half, Q_BLOCK:]
            l_new.append(alpha[p] * l[p] + o[p][2 * half:2 * half + 1, :])
        return l_new

    def softmax_to(s_buf, p_buf, cmax, m):
        m_new = [jnp.maximum(m[p], cmax[p]) for p in pairs]
        alpha = [jnp.exp2(m[p] - m_new[p]) for p in pairs]
        for p in pairs:
            p_buf[p] = jnp.exp2(s_buf[p] - m_new[p]).astype(jnp.bfloat16)
        return alpha, m_new

    def stage(c, cur, carry):
        cmax, alpha_prev, m, l = carry
        other = 1 - cur
        cmax_next = scores_to(jnp.minimum(c + 1, n_kc - 1), s_bufs[other])
        o_prev = weighted_values(jnp.maximum(c - 1, 0), p_bufs[other])
        alpha, m = softmax_to(s_bufs[cur], p_bufs[cur], cmax, m)
        return cmax_next, alpha, m, accumulate(l, alpha_prev, o_prev)

    def att_step(j, carry):
        return stage(2 * j + 1, 1, stage(2 * j, 0, carry))

    zeros_row = jnp.zeros((1, 2 * Q_BLOCK), jnp.float32)
    p_b[...] = jnp.zeros(p_b.shape, p_b.dtype)
    acc_ref[...] = jnp.zeros(acc_ref.shape, acc_ref.dtype)
    carry = (scores_to(0, s_a), [zeros_row + 1.0 for _ in pairs], [zeros_row + MASK_VALUE for _ in pairs],
             [zeros_row for _ in pairs])
    _, alpha_last, _, l = lax.fori_loop(0, n_steps, att_step, carry)
    l = accumulate(l, alpha_last, weighted_values(2 * n_steps - 1, p_b))
    for p in pairs:
        a0, a1 = acc_ref[p, 0], acc_ref[p, 1]
        psl = slice(p * LANES, (p + 1) * LANES)
        y_t = jnp.concatenate([a0 / l[p][:, :Q_BLOCK], a1 / l[p][:, Q_BLOCK:]], axis=0)
        out_ref[0, :, psl] = (y_t.T * ga_ref[0, :, psl].astype(jnp.float32)).astype(out_ref.dtype)


def _dsa(qi, wi, ki, qa, ka, vt, ga, topk):
    b, s, _ = qa.shape
    n_kc = s // KEY_CHUNK
    att_buf = (ATT_HEADS // 2, KEY_CHUNK, 2 * Q_BLOCK)
    qblk = lambda width: pl.BlockSpec((1, Q_BLOCK, width), lambda bi, i: (bi, i, 0))
    full = lambda width: pl.BlockSpec((1, s, width), lambda bi, i: (bi, 0, 0))
    return pl.pallas_call(
        functools.partial(_dsa_kernel, topk=topk),
        grid=(b, s // Q_BLOCK),
        in_specs=[qblk(IDX_WIDTH), qblk(LANES), full(LANES), qblk(ATT_WIDTH), full(ATT_WIDTH),
                  pl.BlockSpec((1, n_kc, ATT_WIDTH, KEY_CHUNK), lambda bi, i: (bi, 0, 0, 0)),
                  qblk(ATT_WIDTH)],
        out_specs=qblk(ATT_WIDTH),
        out_shape=jax.ShapeDtypeStruct((b, s, ATT_WIDTH), jnp.bfloat16),
        scratch_shapes=[pltpu.VMEM((n_kc, KEY_CHUNK, Q_BLOCK), jnp.float32),
                        pltpu.VMEM((n_kc + 1, KEY_CHUNK, Q_BLOCK), jnp.float32),
                        pltpu.VMEM(att_buf, jnp.float32), pltpu.VMEM(att_buf, jnp.float32),
                        pltpu.VMEM(att_buf, jnp.bfloat16), pltpu.VMEM(att_buf, jnp.bfloat16),
                        pltpu.VMEM((ATT_HEADS // 2, 2, ATT_HEAD_DIM, Q_BLOCK), jnp.float32)],
        compiler_params=pltpu.CompilerParams(
            dimension_semantics=("arbitrary", "arbitrary"), vmem_limit_bytes=VMEM_LIMIT_BYTES),
        name="sparse_attention",
    )(qi, wi, ki, qa, ka, vt, ga)


_TN = (((0,), (0,)), ((), ()))


def _retention_kernel(qr_ref, kr_ref, vr_ref, gr_ref, decay_ref, zeta_ref, xi_ref, grow_ref,
                      out_ref, state_ref):
    @pl.when(pl.program_id(1) == 0)
    def _():
        state_ref[...] = jnp.zeros_like(state_ref)

    lane = lax.broadcasted_iota(jnp.int32, (RET_CHUNK, LANES), 1)
    for pair in range(RET_HEADS // 2):
        psl = slice(pair * LANES, (pair + 1) * LANES)
        state = state_ref[psl, :]
        for r in range(RET_STEP_CHUNKS):
            rows = slice(r * RET_CHUNK, (r + 1) * RET_CHUNK)
            q_pair = qr_ref[0, rows, psl].astype(jnp.float32)
            k_pair = kr_ref[0, rows, psl].astype(jnp.float32)
            state_b = state.astype(jnp.bfloat16)
            kv = jnp.zeros((LANES, RET_V_DIM), jnp.float32)
            for sub in range(2):
                h = 2 * pair + sub
                vsl = slice(h * RET_V_DIM, (h + 1) * RET_V_DIM)
                in_head = (lane >= sub * RET_QK_DIM) & (lane < (sub + 1) * RET_QK_DIM)
                q_h = jnp.where(in_head, q_pair, 0.0)
                k_h = jnp.where(in_head, k_pair, 0.0)
                v = vr_ref[0, rows, vsl]
                scores = lax.dot_general(q_h.astype(jnp.bfloat16), k_h.astype(jnp.bfloat16), _NT,
                                         preferred_element_type=jnp.float32) * decay_ref[h]
                inner = jnp.dot(scores.astype(jnp.bfloat16), v, preferred_element_type=jnp.float32)
                cross = jnp.dot((q_h * xi_ref[h]).astype(jnp.bfloat16), state_b,
                                preferred_element_type=jnp.float32)
                o = inner + cross
                o = o * lax.rsqrt(jnp.mean(o * o, axis=-1, keepdims=True) + EPS)
                out_ref[0, rows, vsl] = (o * gr_ref[0, rows, vsl].astype(jnp.float32)).astype(out_ref.dtype)
                kv = kv + lax.dot_general((k_h * zeta_ref[h]).astype(jnp.bfloat16), v, _TN,
                                          preferred_element_type=jnp.float32)
            state = state * grow_ref[psl, :] + kv
        state_ref[psl, :] = state


def _retention(qr, kr, vr, gr, tables):
    b, s, _ = vr.shape
    decay, zeta_b, xi_b, g_rows = tables
    rows = RET_CHUNK * RET_STEP_CHUNKS
    blk = lambda width: pl.BlockSpec((1, rows, width), lambda bi, i: (bi, i, 0))
    const3 = lambda a: pl.BlockSpec(a.shape, lambda bi, i: (0, 0, 0))
    return pl.pallas_call(
        _retention_kernel,
        grid=(b, s // rows),
        in_specs=[blk(RET_QK_WIDTH), blk(RET_QK_WIDTH), blk(RET_V_WIDTH), blk(RET_V_WIDTH),
                  const3(decay), const3(zeta_b), const3(xi_b),
                  pl.BlockSpec(g_rows.shape, lambda bi, i: (0, 0))],
        out_specs=blk(RET_V_WIDTH),
        out_shape=jax.ShapeDtypeStruct((b, s, RET_V_WIDTH), jnp.bfloat16),
        scratch_shapes=[pltpu.VMEM((RET_HEADS * RET_QK_DIM, RET_V_DIM), jnp.float32)],
        compiler_params=pltpu.CompilerParams(
            dimension_semantics=("arbitrary", "arbitrary"), vmem_limit_bytes=VMEM_LIMIT_BYTES),
        name="retention",
    )(qr, kr, vr, gr, decay, zeta_b, xi_b, g_rows)


def _out_proj_kernel(x_ref, ya_ref, yr_ref, wa_ref, wr_ref, gain_ref, out_ref):
    h = (x_ref[...]
         + jnp.dot(ya_ref[...], wa_ref[...], preferred_element_type=jnp.float32)
         + jnp.dot(yr_ref[...], wr_ref[...], preferred_element_type=jnp.float32))
    ms = jnp.mean(h * h, axis=-1, keepdims=True)
    out_ref[...] = (h * lax.rsqrt(ms + EPS)) * gain_ref[...]


def _out_proj(x2, ya2, yr2, wa, wr, gain, tm):
    n, d = x2.shape
    row = lambda width: pl.BlockSpec((tm, width), lambda r: (r, 0))
    const = lambda a: pl.BlockSpec(a.shape, lambda r: (0, 0))
    return pl.pallas_call(
        _out_proj_kernel,
        grid=(n // tm,),
        in_specs=[row(d), row(ATT_WIDTH), row(RET_V_WIDTH), const(wa), const(wr), const(gain)],
        out_specs=row(d),
        out_shape=jax.ShapeDtypeStruct((n, d), jnp.float32),
        compiler_params=pltpu.CompilerParams(
            dimension_semantics=("arbitrary",), vmem_limit_bytes=VMEM_LIMIT_BYTES),
        name="out_proj",
    )(x2, ya2, yr2, wa, wr, gain)


def _prep_weight_kernel(w_ref, out_ref):
    offs = np.concatenate([[0], np.cumsum(SPLITS)]).tolist()
    q_a, k_a, v_a, g_a, q_i, k_i, w_i, q_r, k_r, v_r, g_r = [
        slice(offs[j], offs[j + 1]) for j in range(len(SPLITS))]
    dst = 0
    for src_cols in (q_a, k_a, v_a, g_a, q_i, q_r, k_r, v_r, g_r, k_i, k_i, w_i):
        width = src_cols.stop - src_cols.start
        out_ref[:, dst:dst + width] = w_ref[:, src_cols].astype(out_ref.dtype)
        dst += width
    out_ref[:, dst:] = jnp.zeros((out_ref.shape[0], PROJ_WIDTH - dst), out_ref.dtype)


def _prep_weight(w_in):
    d, width = w_in.shape
    tr = 128
    return pl.pallas_call(
        _prep_weight_kernel,
        grid=(d // tr,),
        in_specs=[pl.BlockSpec((tr, width), lambda r: (r, 0))],
        out_specs=pl.BlockSpec((tr, PROJ_WIDTH), lambda r: (r, 0)),
        out_shape=jax.ShapeDtypeStruct((d, PROJ_WIDTH), jnp.bfloat16),
        compiler_params=pltpu.CompilerParams(
            dimension_semantics=("arbitrary",), vmem_limit_bytes=VMEM_LIMIT_BYTES),
        name="prep_weight",
    )(w_in)


def kernel(x, norm_gain, w_in, w_out, final_gain):
    b, s, d = x.shape
    depth = norm_gain.shape[0]
    assert d == D_MODEL and s % Q_BLOCK == 0 and w_in.shape[2] == sum(SPLITS)
    topk = min(TOPK_MAX, s // 4)
    rope_a = _rotary_tables(s, ATT_HEAD_DIM, ATT_ROPE_DIM, ROPE_THETA)
    rope_r = _rotary_tables(s, RET_QK_DIM, RET_QK_DIM, RET_THETA)
    ret_tables = _retention_tables()
    assert depth == 1, "the final norm is fused into the single layer's output projection"
    w = _prep_weight(w_in[0])
    qa, ka, vt, ga, qi, qr, kr, vr, gr, ki, wi = _in_proj(
        x, norm_gain[0][None, :], w, rope_a, rope_r)
    ya = _dsa(qi, wi, ki, qa, ka, vt, ga, topk)
    yr = _retention(qr, kr, vr, gr, ret_tables)
    wo = w_out[0].astype(jnp.bfloat16)
    out = _out_proj(x.reshape(b * s, d), ya.reshape(b * s, ATT_WIDTH),
                    yr.reshape(b * s, RET_V_WIDTH), wo[:ATT_WIDTH], wo[ATT_WIDTH:],
                    final_gain[None, :], OUT_PROJ_ROWS)
    return out.reshape(b, s, d)
```

```python
import functools

import jax
import jax.numpy as jnp
import numpy as np
from jax import lax
from jax.experimental import pallas as pl
from jax.experimental.pallas import tpu as pltpu

D_MODEL = 1024
ATT_HEADS = 8
ATT_HEAD_DIM = 64
ATT_WIDTH = ATT_HEADS * ATT_HEAD_DIM
ATT_ROPE_DIM = ATT_HEAD_DIM // 4
ROPE_THETA = 500000.0
IDX_HEADS = 4
IDX_DIM = 64
IDX_WIDTH = IDX_HEADS * IDX_DIM
IDX_ROPE_DIM = IDX_DIM // 4
TOPK_MAX = 256
Q_BLOCK = 256
RET_HEADS = 4
RET_QK_DIM = 64
RET_V_DIM = 128
RET_QK_WIDTH = RET_HEADS * RET_QK_DIM
RET_V_WIDTH = RET_HEADS * RET_V_DIM
RET_CHUNK = 128
RET_THETA = 10000.0
MIX_WIDTH = ATT_WIDTH + RET_V_WIDTH
SPLITS = (ATT_WIDTH, ATT_WIDTH, ATT_WIDTH, ATT_WIDTH, IDX_WIDTH, IDX_DIM, IDX_HEADS,
          RET_QK_WIDTH, RET_QK_WIDTH, RET_V_WIDTH, RET_V_WIDTH)
EPS = 1e-6

LANES = 128
SUBLANES = 8
BF16_ROWS = 16
MXU_COLS = 256
LOG2_E = float(np.log2(np.e))
MASK_VALUE = -1e30
VMEM_LIMIT_BYTES = 48 * 1024 * 1024

OFF_QA = 0
OFF_KA = OFF_QA + ATT_WIDTH
OFF_VA = OFF_KA + ATT_WIDTH
OFF_GA = OFF_VA + ATT_WIDTH
OFF_QI = OFF_GA + ATT_WIDTH
OFF_QR = OFF_QI + IDX_WIDTH
OFF_KR = OFF_QR + RET_QK_WIDTH
OFF_VR = OFF_KR + RET_QK_WIDTH
OFF_GR = OFF_VR + RET_V_WIDTH
OFF_KI = OFF_GR + RET_V_WIDTH
OFF_WI = OFF_KI + LANES
PROJ_WIDTH = OFF_WI + LANES
KEY_CHUNK = 256
RET_STEP_CHUNKS = 8
OUT_PROJ_ROWS = 1024
COUNT_ROWS = 32


def _rotary_tables(seq, head_dim, rot_dim, theta):
    half = rot_dim // 2
    inv = 1.0 / (theta ** (jnp.arange(half, dtype=jnp.float32) / half))
    dim = np.arange(LANES) % head_dim
    rotated = dim < rot_dim
    inv_lane = jnp.where(rotated, inv[dim % half], 0.0)
    ang = jnp.arange(seq).astype(jnp.float32)[:, None] * inv_lane[None, :]
    sin = jnp.sin(ang)
    s_prev = jnp.where((dim >= half) & rotated, sin, 0.0)
    s_next = jnp.where(dim < half, -sin, 0.0)
    return jnp.stack([jnp.cos(ang), s_prev, s_next])


def _retention_tables():
    c = RET_CHUNK
    gamma = 1.0 - 2.0 ** (-5.0 - jnp.arange(RET_HEADS, dtype=jnp.float32))
    log_g = jnp.log(gamma)
    idx = jnp.arange(c, dtype=jnp.float32)
    diff = idx[:, None] - idx[None, :]
    decay = jnp.where(diff[None] >= 0,
                      jnp.exp(log_g[:, None, None] * jnp.maximum(diff, 0.0)[None]), 0.0)
    zeta = jnp.exp(log_g[:, None] * (c - 1.0 - idx)[None, :])
    xi = jnp.exp(log_g[:, None] * (idx + 1.0)[None, :])
    g_chunk = jnp.exp(log_g * c)
    zeta_b = jnp.broadcast_to(zeta[:, :, None], (RET_HEADS, c, LANES))
    xi_b = jnp.broadcast_to(xi[:, :, None], (RET_HEADS, c, LANES))
    g_rows = jnp.repeat(g_chunk, RET_QK_DIM)[:, None]
    g_rows = jnp.broadcast_to(g_rows, (RET_HEADS * RET_QK_DIM, RET_V_DIM))
    return decay, zeta_b, xi_b, g_rows


def _rope(z, tab_ref, half):
    return (z * tab_ref[0] + pltpu.roll(z, half, 1) * tab_ref[1]
            + pltpu.roll(z, LANES - half, 1) * tab_ref[2])


def _silu(g):
    return g * (1.0 / (1.0 + jnp.exp(-g)))


def _in_proj_kernel(x_ref, gain_ref, w_ref, rope_a_ref, rope_r_ref,
                    qa_ref, ka_ref, vt_ref, ga_ref, qi_ref, qr_ref, kr_ref, vr_ref, gr_ref,
                    ki_ref, wi_ref, u_ref):
    x = x_ref[0]
    ms = jnp.mean(x * x, axis=-1, keepdims=True)
    u_ref[...] = ((x * lax.rsqrt(ms + EPS)) * gain_ref[...]).astype(jnp.bfloat16)

    def proj(off, width):
        out = []
        for j in range(width // MXU_COLS):
            c0 = off + j * MXU_COLS
            z = jnp.dot(u_ref[...], w_ref[:, c0:c0 + MXU_COLS], preferred_element_type=jnp.float32)
            for h in range(MXU_COLS // LANES):
                lo = j * MXU_COLS + h * LANES
                out.append((slice(lo, lo + LANES), z[:, h * LANES:(h + 1) * LANES]))
        return out

    a_half = ATT_ROPE_DIM // 2
    r_half = RET_QK_DIM // 2
    att_scale = ATT_HEAD_DIM ** -0.5 * LOG2_E
    ret_scale = RET_QK_DIM ** -0.5
    for sl, z in proj(OFF_QA, ATT_WIDTH):
        qa_ref[0, :, sl] = (_rope(z, rope_a_ref, a_half) * att_scale).astype(qa_ref.dtype)
    for sl, z in proj(OFF_KA, ATT_WIDTH):
        ka_ref[0, :, sl] = _rope(z, rope_a_ref, a_half).astype(ka_ref.dtype)
    for sl, z in proj(OFF_VA, ATT_WIDTH):
        vt_ref[0, 0, sl, :] = z.T.astype(vt_ref.dtype)
    for sl, z in proj(OFF_GA, ATT_WIDTH):
        ga_ref[0, :, sl] = _silu(z).astype(ga_ref.dtype)
    for sl, z in proj(OFF_QI, IDX_WIDTH):
        qi_ref[0, :, sl] = _rope(z, rope_a_ref, a_half).astype(qi_ref.dtype)
    for sl, z in proj(OFF_QR, RET_QK_WIDTH):
        qr_ref[0, :, sl] = _rope(z, rope_r_ref, r_half).astype(qr_ref.dtype)
    for sl, z in proj(OFF_KR, RET_QK_WIDTH):
        kr_ref[0, :, sl] = (_rope(z, rope_r_ref, r_half) * ret_scale).astype(kr_ref.dtype)
    for sl, z in proj(OFF_VR, RET_V_WIDTH):
        vr_ref[0, :, sl] = z.astype(vr_ref.dtype)
    for sl, z in proj(OFF_GR, RET_V_WIDTH):
        gr_ref[0, :, sl] = _silu(z).astype(gr_ref.dtype)
    (_, ki), (_, wi) = proj(OFF_KI, 2 * LANES)
    ki_ref[0] = _rope(ki, rope_a_ref, a_half).astype(ki_ref.dtype)
    wi_ref[0] = wi * ((IDX_DIM ** -0.5) * (IDX_HEADS ** -0.5))


def _in_proj(x, gain, w, rope_a, rope_r):
    b, s, d = x.shape
    tm = KEY_CHUNK
    row = lambda width: pl.BlockSpec((1, tm, width), lambda si, bi: (bi, si, 0))
    tab = pl.BlockSpec((3, tm, LANES), lambda si, bi: (0, si, 0))
    vt_spec = pl.BlockSpec((1, 1, ATT_WIDTH, tm), lambda si, bi: (bi, si, 0, 0))
    bf = jnp.bfloat16
    out_shapes = [
        jax.ShapeDtypeStruct((b, s, ATT_WIDTH), bf),
        jax.ShapeDtypeStruct((b, s, ATT_WIDTH), bf),
        jax.ShapeDtypeStruct((b, s // tm, ATT_WIDTH, tm), bf),
        jax.ShapeDtypeStruct((b, s, ATT_WIDTH), bf),
        jax.ShapeDtypeStruct((b, s, IDX_WIDTH), bf),
        jax.ShapeDtypeStruct((b, s, RET_QK_WIDTH), bf),
        jax.ShapeDtypeStruct((b, s, RET_QK_WIDTH), bf),
        jax.ShapeDtypeStruct((b, s, RET_V_WIDTH), bf),
        jax.ShapeDtypeStruct((b, s, RET_V_WIDTH), bf),
        jax.ShapeDtypeStruct((b, s, LANES), bf),
        jax.ShapeDtypeStruct((b, s, LANES), jnp.float32),
    ]
    out_specs = [row(ATT_WIDTH), row(ATT_WIDTH), vt_spec, row(ATT_WIDTH),
                 row(IDX_WIDTH), row(RET_QK_WIDTH), row(RET_QK_WIDTH),
                 row(RET_V_WIDTH), row(RET_V_WIDTH), row(LANES), row(LANES)]
    return pl.pallas_call(
        _in_proj_kernel,
        grid=(s // tm, b),
        in_specs=[row(d),
                  pl.BlockSpec((1, d), lambda si, bi: (0, 0)),
                  pl.BlockSpec((d, PROJ_WIDTH), lambda si, bi: (0, 0)),
                  tab, tab],
        out_specs=out_specs,
        out_shape=out_shapes,
        scratch_shapes=[pltpu.VMEM((tm, d), jnp.bfloat16)],
        compiler_params=pltpu.CompilerParams(
            dimension_semantics=("arbitrary", "arbitrary"), vmem_limit_bytes=VMEM_LIMIT_BYTES),
        name="in_proj",
    )(x, gain, w, rope_a, rope_r)


_NT = (((1,), (1,)), ((), ()))


def _key_to_float(key):
    k = key ^ jnp.int32(-2 ** 31)
    bits = jnp.where(k >= 0, k, k ^ jnp.int32(2 ** 31 - 1))
    return lax.bitcast_convert_type(bits, jnp.float32)


def _split_heads_t(blk):
    blk_t = blk.astype(jnp.float32).T
    feat = lax.broadcasted_iota(jnp.int32, blk_t.shape, 0)
    zero = jnp.zeros_like(blk_t)
    both = jnp.concatenate([jnp.where(feat < LANES // 2, blk_t, zero),
                            jnp.where(feat >= LANES // 2, blk_t, zero)], axis=1)
    return both.astype(jnp.bfloat16)


def _dsa_kernel(qi_ref, wi_ref, ki_ref, qa_ref, ka_ref, vt_ref, ga_ref,
                out_ref, sc_ref, bias_ref, s_a, s_b, p_a, p_b, acc_ref, *, topk):
    i = pl.program_id(1)
    n_kc = sc_ref.shape[0]
    n_chunks = ((i + 1) * Q_BLOCK + KEY_CHUNK - 1) // KEY_CHUNK
    q_pos = i * Q_BLOCK + lax.broadcasted_iota(jnp.int32, (1, Q_BLOCK), 1)
    k_off = lax.broadcasted_iota(jnp.int32, (KEY_CHUNK, 1), 0)

    qi = qi_ref[0]
    qi_pairs = [_split_heads_t(qi[:, p * LANES:(p + 1) * LANES]) for p in range(IDX_HEADS // 2)]
    q_pairs = [_split_heads_t(qa_ref[0, :, p * LANES:(p + 1) * LANES]) for p in range(ATT_HEADS // 2)]
    w_t = wi_ref[0].T
    w_rows = [w_t[h:h + 1, :] for h in range(IDX_HEADS)]

    s_bufs = (s_a, s_b)
    n_steps = (n_chunks + 1) // 2

    def logits_to(c, buf):
        start = pl.multiple_of(c * KEY_CHUNK, KEY_CHUNK)
        ki = ki_ref[0, pl.ds(start, KEY_CHUNK), :]
        for p in range(IDX_HEADS // 2):
            buf[p] = jnp.dot(ki, qi_pairs[p], preferred_element_type=jnp.float32)

    def score_stage(c, cur):
        logits_to(jnp.minimum(c + 1, n_kc - 1), s_bufs[1 - cur])
        buf = s_bufs[cur]
        acc = None
        for h in range(IDX_HEADS):
            lanes = slice((h % 2) * Q_BLOCK, (h % 2 + 1) * Q_BLOCK)
            term = w_rows[h] * jnp.maximum(buf[h // 2, :, lanes], 0.0)
            acc = term if acc is None else acc + term
        sc_ref[c] = jnp.where(c * KEY_CHUNK + k_off <= q_pos, acc, -jnp.inf)

    def score_step(j, carry):
        score_stage(2 * j, 0)
        score_stage(2 * j + 1, 1)
        return carry

    logits_to(0, s_a)
    lax.fori_loop(0, n_steps, score_step, 0)

    def count(pred):
        rows = COUNT_ROWS

        def body(j, acc):
            for c in (2 * j, 2 * j + 1):
                hit = jnp.where(pred(sc_ref[c]), 1.0, 0.0)
                for r in range(KEY_CHUNK // rows):
                    acc = acc + hit[r * rows:(r + 1) * rows]
            return acc
        acc = lax.fori_loop(0, n_steps, body, jnp.zeros((rows, Q_BLOCK), jnp.float32))
        return jnp.sum(acc, axis=0, keepdims=True)

    @pl.when((i + 1) * Q_BLOCK <= topk)
    def _():
        bias_ref[0] = jnp.where(k_off <= q_pos, 0.0, MASK_VALUE)

    @pl.when((i + 1) * Q_BLOCK > topk)
    def _():
        def bit_step(b, carry):
            key, n_ge = carry
            cand = key | jnp.left_shift(jnp.int32(1), 31 - b)
            f = _key_to_float(cand)
            total = count(lambda s: s >= f)
            take = total >= topk
            return jnp.where(take, cand, key), jnp.where(take, total, n_ge)

        key, n_ge = lax.fori_loop(0, 32, bit_step, (jnp.zeros((1, Q_BLOCK), jnp.int32),
                                                     jnp.zeros((1, Q_BLOCK), jnp.float32)))
        thr = _key_to_float(key)
        surplus = n_ge - topk

        n_groups = KEY_CHUNK // SUBLANES
        sub = lax.broadcasted_iota(jnp.int32, (1, SUBLANES, Q_BLOCK), 1)
        keep = {k: jnp.where(sub < SUBLANES - k, 1.0, 0.0) for k in (1, 2, 4)}

        def body(j, later):
            c = n_chunks - 1 - j
            s = sc_ref[c]
            eq = s == thr
            cnt3 = jnp.where(eq, 1.0, 0.0).reshape(n_groups, SUBLANES, Q_BLOCK)
            for k in (1, 2, 4):
                cnt3 = cnt3 + pltpu.roll(cnt3, SUBLANES - k, 1) * keep[k]
            groups = [None] * n_groups
            for g in reversed(range(n_groups)):
                groups[g] = cnt3[g] + later
                later = later + cnt3[g, 0:1, :]
            from_here = jnp.concatenate(groups, axis=0)
            sel = (s > thr) | (eq & (from_here > surplus))
            bias_ref[c] = jnp.where(sel, 0.0, MASK_VALUE)
            return later

        lax.fori_loop(0, n_chunks, body, jnp.zeros((1, Q_BLOCK), jnp.float32))

    n_pairs = ATT_HEADS // 2
    half = ATT_HEAD_DIM

    pairs = range(n_pairs)
    p_bufs = (p_a, p_b)
    bias_ref[n_chunks] = jnp.full((KEY_CHUNK, Q_BLOCK), MASK_VALUE, jnp.float32)

    def scores_to(c, s_buf):
        start = pl.multiple_of(c * KEY_CHUNK, KEY_CHUNK)
        bias = bias_ref[c]
        bias2 = jnp.concatenate([bias, bias], axis=1)
        cmax = []
        for p in pairs:
            s = jnp.dot(ka_ref[0, pl.ds(start, KEY_CHUNK), p * LANES:(p + 1) * LANES], q_pairs[p],
                        preferred_element_type=jnp.float32) + bias2
            s_buf[p] = s
            cmax.append(jnp.max(s, axis=0, keepdims=True))
        return cmax

    ones_rows = jnp.ones((BF16_ROWS, KEY_CHUNK), jnp.bfloat16)

    def weighted_values(c, p_buf):
        return [jnp.dot(jnp.concatenate([vt_ref[0, c, p * LANES:(p + 1) * LANES, :], ones_rows], axis=0),
                        p_buf[p], preferred_element_type=jnp.float32) for p in pairs]

    def accumulate(l, alpha, o):
        l_new = []
        for p in pairs:
            acc_ref[p, 0] = alpha[p][:, :Q_BLOCK] * acc_ref[p, 0] + o[p][:half, :Q_BLOCK]
            acc_ref[p, 1] = alpha[p][:, Q_BLOCK:] * acc_ref[p, 1] + o[p][half:2 * half, Q_BLOCK:]
            l_new.append(alpha[p] * l[p] + o[p][2 * half:2 * half + 1, :])
        return l_new

    def softmax_to(s_buf, p_buf, cmax, m):
        m_new = [jnp.maximum(m[p], cmax[p]) for p in pairs]
        alpha = [jnp.exp2(m[p] - m_new[p]) for p in pairs]
        for p in pairs:
            p_buf[p] = jnp.exp2(s_buf[p] - m_new[p]).astype(jnp.bfloat16)
        return alpha, m_new

    def stage(c, cur, carry):
        cmax, alpha_prev, m, l = carry
        other = 1 - cur
        cmax_next = scores_to(jnp.minimum(c + 1, n_kc - 1), s_bufs[other])
        o_prev = weighted_values(jnp.maximum(c - 1, 0), p_bufs[other])
        alpha, m = softmax_to(s_bufs[cur], p_bufs[cur], cmax, m)
        return cmax_next, alpha, m, accumulate(l, alpha_prev, o_prev)

    def att_step(j, carry):
        return stage(2 * j + 1, 1, stage(2 * j, 0, carry))

    zeros_row = jnp.zeros((1, 2 * Q_BLOCK), jnp.float32)
    p_b[...] = jnp.zeros(p_b.shape, p_b.dtype)
    acc_ref[...] = jnp.zeros(acc_ref.shape, acc_ref.dtype)
    carry = (scores_to(0, s_a), [zeros_row + 1.0 for _ in pairs], [zeros_row + MASK_VALUE for _ in pairs],
             [zeros_row for _ in pairs])
    _, alpha_last, _, l = lax.fori_loop(0, n_steps, att_step, carry)
    l = accumulate(l, alpha_last, weighted_values(2 * n_steps - 1, p_b))
    for p in pairs:
        a0, a1 = acc_ref[p, 0], acc_ref[p, 1]
        psl = slice(p * LANES, (p + 1) * LANES)
        y_t = jnp.concatenate([a0 / l[p][:, :Q_BLOCK], a1 / l[p][:, Q_BLOCK:]], axis=0)
        out_ref[0, :, psl] = (y_t.T * ga_ref[0, :, psl].astype(jnp.float32)).astype(out_ref.dtype)


def _dsa(qi, wi, ki, qa, ka, vt, ga, topk):
    b, s, _ = qa.shape
    n_kc = s // KEY_CHUNK
    att_buf = (ATT_HEADS // 2, KEY_CHUNK, 2 * Q_BLOCK)
    qblk = lambda width: pl.BlockSpec((1, Q_BLOCK, width), lambda bi, i: (bi, i, 0))
    full = lambda width: pl.BlockSpec((1, s, width), lambda bi, i: (bi, 0, 0))
    return pl.pallas_call(
        functools.partial(_dsa_kernel, topk=topk),
        grid=(b, s // Q_BLOCK),
        in_specs=[qblk(IDX_WIDTH), qblk(LANES), full(LANES), qblk(ATT_WIDTH), full(ATT_WIDTH),
                  pl.BlockSpec((1, n_kc, ATT_WIDTH, KEY_CHUNK), lambda bi, i: (bi, 0, 0, 0)),
                  qblk(ATT_WIDTH)],
        out_specs=qblk(ATT_WIDTH),
        out_shape=jax.ShapeDtypeStruct((b, s, ATT_WIDTH), jnp.bfloat16),
        scratch_shapes=[pltpu.VMEM((n_kc, KEY_CHUNK, Q_BLOCK), jnp.float32),
                        pltpu.VMEM((n_kc + 1, KEY_CHUNK, Q_BLOCK), jnp.float32),
                        pltpu.VMEM(att_buf, jnp.float32), pltpu.VMEM(att_buf, jnp.float32),
                        pltpu.VMEM(att_buf, jnp.bfloat16), pltpu.VMEM(att_buf, jnp.bfloat16),
                        pltpu.VMEM((ATT_HEADS // 2, 2, ATT_HEAD_DIM, Q_BLOCK), jnp.float32)],
        compiler_params=pltpu.CompilerParams(
            dimension_semantics=("arbitrary", "arbitrary"), vmem_limit_bytes=VMEM_LIMIT_BYTES),
        name="sparse_attention",
    )(qi, wi, ki, qa, ka, vt, ga)


_TN = (((0,), (0,)), ((), ()))


def _retention_kernel(qr_ref, kr_ref, vr_ref, gr_ref, decay_ref, zeta_ref, xi_ref, grow_ref,
                      out_ref, state_ref):
    @pl.when(pl.program_id(1) == 0)
    def _():
        state_ref[...] = jnp.zeros_like(state_ref)

    lane = lax.broadcasted_iota(jnp.int32, (RET_CHUNK, LANES), 1)
    for pair in range(RET_HEADS // 2):
        psl = slice(pair * LANES, (pair + 1) * LANES)
        state = state_ref[psl, :]
        for r in range(RET_STEP_CHUNKS):
            rows = slice(r * RET_CHUNK, (r + 1) * RET_CHUNK)
            q_pair = qr_ref[0, rows, psl].astype(jnp.float32)
            k_pair = kr_ref[0, rows, psl].astype(jnp.float32)
            state_b = state.astype(jnp.bfloat16)
            kv = jnp.zeros((LANES, RET_V_DIM), jnp.float32)
            for sub in range(2):
                h = 2 * pair + sub
                vsl = slice(h * RET_V_DIM, (h + 1) * RET_V_DIM)
                in_head = (lane >= sub * RET_QK_DIM) & (lane < (sub + 1) * RET_QK_DIM)
                q_h = jnp.where(in_head, q_pair, 0.0)
                k_h = jnp.where(in_head, k_pair, 0.0)
                v = vr_ref[0, rows, vsl]
                scores = lax.dot_general(q_h.astype(jnp.bfloat16), k_h.astype(jnp.bfloat16), _NT,
                                         preferred_element_type=jnp.float32) * decay_ref[h]
                inner = jnp.dot(scores.astype(jnp.bfloat16), v, preferred_element_type=jnp.float32)
                cross = jnp.dot((q_h * xi_ref[h]).astype(jnp.bfloat16), state_b,
                                preferred_element_type=jnp.float32)
                o = inner + cross
                o = o * lax.rsqrt(jnp.mean(o * o, axis=-1, keepdims=True) + EPS)
                out_ref[0, rows, vsl] = (o * gr_ref[0, rows, vsl].astype(jnp.float32)).astype(out_ref.dtype)
                kv = kv + lax.dot_general((k_h * zeta_ref[h]).astype(jnp.bfloat16), v, _TN,
                                          preferred_element_type=jnp.float32)
            state = state * grow_ref[psl, :] + kv
        state_ref[psl, :] = state


def _retention(qr, kr, vr, gr, tables):
    b, s, _ = vr.shape
    decay, zeta_b, xi_b, g_rows = tables
    rows = RET_CHUNK * RET_STEP_CHUNKS
    blk = lambda width: pl.BlockSpec((1, rows, width), lambda bi, i: (bi, i, 0))
    const3 = lambda a: pl.BlockSpec(a.shape, lambda bi, i: (0, 0, 0))
    return pl.pallas_call(
        _retention_kernel,
        grid=(b, s // rows),
        in_specs=[blk(RET_QK_WIDTH), blk(RET_QK_WIDTH), blk(RET_V_WIDTH), blk(RET_V_WIDTH),
                  const3(decay), const3(zeta_b), const3(xi_b),
                  pl.BlockSpec(g_rows.shape, lambda bi, i: (0, 0))],
        out_specs=blk(RET_V_WIDTH),
        out_shape=jax.ShapeDtypeStruct((b, s, RET_V_WIDTH), jnp.bfloat16),
        scratch_shapes=[pltpu.VMEM((RET_HEADS * RET_QK_DIM, RET_V_DIM), jnp.float32)],
        compiler_params=pltpu.CompilerParams(
            dimension_semantics=("arbitrary", "arbitrary"), vmem_limit_bytes=VMEM_LIMIT_BYTES),
        name="retention",
    )(qr, kr, vr, gr, decay, zeta_b, xi_b, g_rows)


def _out_proj_kernel(x_ref, ya_ref, yr_ref, wa_ref, wr_ref, gain_ref, out_ref):
    h = (x_ref[...]
         + jnp.dot(ya_ref[...], wa_ref[...], preferred_element_type=jnp.float32)
         + jnp.dot(yr_ref[...], wr_ref[...], preferred_element_type=jnp.float32))
    ms = jnp.mean(h * h, axis=-1, keepdims=True)
    out_ref[...] = (h * lax.rsqrt(ms + EPS)) * gain_ref[...]


def _out_proj(x2, ya2, yr2, wa, wr, gain, tm):
    n, d = x2.shape
    row = lambda width: pl.BlockSpec((tm, width), lambda r: (r, 0))
    const = lambda a: pl.BlockSpec(a.shape, lambda r: (0, 0))
    return pl.pallas_call(
        _out_proj_kernel,
        grid=(n // tm,),
        in_specs=[row(d), row(ATT_WIDTH), row(RET_V_WIDTH), const(wa), const(wr), const(gain)],
        out_specs=row(d),
        out_shape=jax.ShapeDtypeStruct((n, d), jnp.float32),
        compiler_params=pltpu.CompilerParams(
            dimension_semantics=("arbitrary",), vmem_limit_bytes=VMEM_LIMIT_BYTES),
        name="out_proj",
    )(x2, ya2, yr2, wa, wr, gain)


def _prep_weight_kernel(w_ref, out_ref):
    offs = np.concatenate([[0], np.cumsum(SPLITS)]).tolist()
    q_a, k_a, v_a, g_a, q_i, k_i, w_i, q_r, k_r, v_r, g_r = [
        slice(offs[j], offs[j + 1]) for j in range(len(SPLITS))]
    dst = 0
    for src_cols in (q_a, k_a, v_a, g_a, q_i, q_r, k_r, v_r, g_r, k_i, k_i, w_i):
        width = src_cols.stop - src_cols.start
        out_ref[:, dst:dst + width] = w_ref[:, src_cols].astype(out_ref.dtype)
        dst += width
    out_ref[:, dst:] = jnp.zeros((out_ref.shape[0], PROJ_WIDTH - dst), out_ref.dtype)


def _prep_weight(w_in):
    d, width = w_in.shape
    tr = 128
    return pl.pallas_call(
        _prep_weight_kernel,
        grid=(d // tr,),
        in_specs=[pl.BlockSpec((tr, width), lambda r: (r, 0))],
        out_specs=pl.BlockSpec((tr, PROJ_WIDTH), lambda r: (r, 0)),
        out_shape=jax.ShapeDtypeStruct((d, PROJ_WIDTH), jnp.bfloat16),
        compiler_params=pltpu.CompilerParams(
            dimension_semantics=("arbitrary",), vmem_limit_bytes=VMEM_LIMIT_BYTES),
        name="prep_weight",
    )(w_in)


def kernel(x, norm_gain, w_in, w_out, final_gain):
    b, s, d = x.shape
    depth = norm_gain.shape[0]
    assert d == D_MODEL and s % Q_BLOCK == 0 and w_in.shape[2] == sum(SPLITS)
    topk = min(TOPK_MAX, s // 4)
    rope_a = _rotary_tables(s, ATT_HEAD_DIM, ATT_ROPE_DIM, ROPE_THETA)
    rope_r = _rotary_tables(s, RET_QK_DIM, RET_QK_DIM, RET_THETA)
    ret_tables = _retention_tables()
    assert depth == 1, "the final norm is fused into the single layer's output projection"
    w = _prep_weight(w_in[0])
    qa, ka, vt, ga, qi, qr, kr, vr, gr, ki, wi = _in_proj(
        x, norm_gain[0][None, :], w, rope_a, rope_r)
    ya = _dsa(qi, wi, ki, qa, ka, vt, ga, topk)
    yr = _retention(qr, kr, vr, gr, ret_tables)
    wo = w_out[0].astype(jnp.bfloat16)
    out = _out_proj(x.reshape(b * s, d), ya.reshape(b * s, ATT_WIDTH),
                    yr.reshape(b * s, RET_V_WIDTH), wo[:ATT_WIDTH], wo[ATT_WIDTH:],
                    final_gain[None, :], OUT_PROJ_ROWS)
    return out.reshape(b, s, d)
```

```python
import functools

import jax
import jax.numpy as jnp
import numpy as np
from jax import lax
from jax.experimental import pallas as pl
from jax.experimental.pallas import tpu as pltpu

D_MODEL = 1024
ATT_HEADS = 8
ATT_HEAD_DIM = 64
ATT_WIDTH = ATT_HEADS * ATT_HEAD_DIM
ATT_ROPE_DIM = ATT_HEAD_DIM // 4
ROPE_THETA = 500000.0
IDX_HEADS = 4
IDX_DIM = 64
IDX_WIDTH = IDX_HEADS * IDX_DIM
IDX_ROPE_DIM = IDX_DIM // 4
TOPK_MAX = 256
Q_BLOCK = 256
RET_HEADS = 4
RET_QK_DIM = 64
RET_V_DIM = 128
RET_QK_WIDTH = RET_HEADS * RET_QK_DIM
RET_V_WIDTH = RET_HEADS * RET_V_DIM
RET_CHUNK = 128
RET_THETA = 10000.0
MIX_WIDTH = ATT_WIDTH + RET_V_WIDTH
SPLITS = (ATT_WIDTH, ATT_WIDTH, ATT_WIDTH, ATT_WIDTH, IDX_WIDTH, IDX_DIM, IDX_HEADS,
          RET_QK_WIDTH, RET_QK_WIDTH, RET_V_WIDTH, RET_V_WIDTH)
EPS = 1e-6

LANES = 128
SUBLANES = 8
BF16_ROWS = 16
MXU_COLS = 256
LOG2_E = float(np.log2(np.e))
MASK_VALUE = -1e30
VMEM_LIMIT_BYTES = 48 * 1024 * 1024

OFF_QA = 0
OFF_KA = OFF_QA + ATT_WIDTH
OFF_VA = OFF_KA + ATT_WIDTH
OFF_GA = OFF_VA + ATT_WIDTH
OFF_QI = OFF_GA + ATT_WIDTH
OFF_QR = OFF_QI + IDX_WIDTH
OFF_KR = OFF_QR + RET_QK_WIDTH
OFF_VR = OFF_KR + RET_QK_WIDTH
OFF_GR = OFF_VR + RET_V_WIDTH
OFF_KI = OFF_GR + RET_V_WIDTH
OFF_WI = OFF_KI + LANES
PROJ_WIDTH = OFF_WI + LANES
KEY_CHUNK = 256
RET_STEP_CHUNKS = 8
OUT_PROJ_ROWS = 1024
COUNT_ROWS = 32


def _rotary_tables(seq, head_dim, rot_dim, theta):
    half = rot_dim // 2
    inv = 1.0 / (theta ** (jnp.arange(half, dtype=jnp.float32) / half))
    ang = jnp.arange(seq).astype(jnp.float32)[:, None] * inv[None, :]
    cos = jnp.cos(ang)
    sin = jnp.sin(ang)
    pad = head_dim - rot_dim
    ones = jnp.ones((seq, pad), jnp.float32)
    zeros = jnp.zeros((seq, pad), jnp.float32)
    zh = jnp.zeros((seq, half), jnp.float32)
    c = jnp.concatenate([cos, cos, ones], axis=1)
    s_prev = jnp.concatenate([zh, sin, zeros], axis=1)
    s_next = jnp.concatenate([-sin, zh, zeros], axis=1)
    reps = LANES // head_dim
    return jnp.stack([jnp.tile(c, (1, reps)), jnp.tile(s_prev, (1, reps)),
                      jnp.tile(s_next, (1, reps))])


def _retention_tables():
    c = RET_CHUNK
    gamma = 1.0 - 2.0 ** (-5.0 - jnp.arange(RET_HEADS, dtype=jnp.float32))
    log_g = jnp.log(gamma)
    idx = jnp.arange(c, dtype=jnp.float32)
    diff = idx[:, None] - idx[None, :]
    decay = jnp.where(diff[None] >= 0,
                      jnp.exp(log_g[:, None, None] * jnp.maximum(diff, 0.0)[None]), 0.0)
    zeta = jnp.exp(log_g[:, None] * (c - 1.0 - idx)[None, :])
    xi = jnp.exp(log_g[:, None] * (idx + 1.0)[None, :])
    g_chunk = jnp.exp(log_g * c)
    zeta_b = jnp.broadcast_to(zeta[:, :, None], (RET_HEADS, c, LANES))
    xi_b = jnp.broadcast_to(xi[:, :, None], (RET_HEADS, c, LANES))
    g_rows = jnp.repeat(g_chunk, RET_QK_DIM)[:, None]
    g_rows = jnp.broadcast_to(g_rows, (RET_HEADS * RET_QK_DIM, RET_V_DIM))
    return decay, zeta_b, xi_b, g_rows


def _rope(z, tab_ref, half):
    return (z * tab_ref[0] + pltpu.roll(z, half, 1) * tab_ref[1]
            + pltpu.roll(z, LANES - half, 1) * tab_ref[2])


def _silu(g):
    return g * (1.0 / (1.0 + jnp.exp(-g)))


def _in_proj_kernel(x_ref, gain_ref, w_ref, rope_a_ref, rope_r_ref,
                    qa_ref, ka_ref, vt_ref, ga_ref, qi_ref, qr_ref, kr_ref, vr_ref, gr_ref,
                    ki_ref, wi_ref, u_ref):
    x = x_ref[0]
    ms = jnp.mean(x * x, axis=-1, keepdims=True)
    u_ref[...] = ((x * lax.rsqrt(ms + EPS)) * gain_ref[...]).astype(jnp.bfloat16)

    def proj(off, width):
        out = []
        for j in range(width // MXU_COLS):
            c0 = off + j * MXU_COLS
            z = jnp.dot(u_ref[...], w_ref[:, c0:c0 + MXU_COLS], preferred_element_type=jnp.float32)
            for h in range(MXU_COLS // LANES):
                lo = j * MXU_COLS + h * LANES
                out.append((slice(lo, lo + LANES), z[:, h * LANES:(h + 1) * LANES]))
        return out

    a_half = ATT_ROPE_DIM // 2
    r_half = RET_QK_DIM // 2
    att_scale = ATT_HEAD_DIM ** -0.5 * LOG2_E
    ret_scale = RET_QK_DIM ** -0.5
    for sl, z in proj(OFF_QA, ATT_WIDTH):
        qa_ref[0, :, sl] = (_rope(z, rope_a_ref, a_half) * att_scale).astype(qa_ref.dtype)
    for sl, z in proj(OFF_KA, ATT_WIDTH):
        ka_ref[0, :, sl] = _rope(z, rope_a_ref, a_half).astype(ka_ref.dtype)
    for sl, z in proj(OFF_VA, ATT_WIDTH):
        vt_ref[0, 0, sl, :] = z.T.astype(vt_ref.dtype)
    for sl, z in proj(OFF_GA, ATT_WIDTH):
        ga_ref[0, :, sl] = _silu(z).astype(ga_ref.dtype)
    for sl, z in proj(OFF_QI, IDX_WIDTH):
        qi_ref[0, :, sl] = _rope(z, rope_a_ref, a_half).astype(qi_ref.dtype)
    for sl, z in proj(OFF_QR, RET_QK_WIDTH):
        qr_ref[0, :, sl] = _rope(z, rope_r_ref, r_half).astype(qr_ref.dtype)
    for sl, z in proj(OFF_KR, RET_QK_WIDTH):
        kr_ref[0, :, sl] = (_rope(z, rope_r_ref, r_half) * ret_scale).astype(kr_ref.dtype)
    for sl, z in proj(OFF_VR, RET_V_WIDTH):
        vr_ref[0, :, sl] = z.astype(vr_ref.dtype)
    for sl, z in proj(OFF_GR, RET_V_WIDTH):
        gr_ref[0, :, sl] = _silu(z).astype(gr_ref.dtype)
    (_, ki), (_, wi) = proj(OFF_KI, 2 * LANES)
    ki_ref[0] = _rope(ki, rope_a_ref, a_half).astype(ki_ref.dtype)
    wi_ref[0] = wi * ((IDX_DIM ** -0.5) * (IDX_HEADS ** -0.5))


def _in_proj(x, gain, w, rope_a, rope_r):
    b, s, d = x.shape
    tm = KEY_CHUNK
    row = lambda width: pl.BlockSpec((1, tm, width), lambda si, bi: (bi, si, 0))
    tab = pl.BlockSpec((3, tm, LANES), lambda si, bi: (0, si, 0))
    vt_spec = pl.BlockSpec((1, 1, ATT_WIDTH, tm), lambda si, bi: (bi, si, 0, 0))
    bf = jnp.bfloat16
    out_shapes = [
        jax.ShapeDtypeStruct((b, s, ATT_WIDTH), bf),
        jax.ShapeDtypeStruct((b, s, ATT_WIDTH), bf),
        jax.ShapeDtypeStruct((b, s // tm, ATT_WIDTH, tm), bf),
        jax.ShapeDtypeStruct((b, s, ATT_WIDTH), bf),
        jax.ShapeDtypeStruct((b, s, IDX_WIDTH), bf),
        jax.ShapeDtypeStruct((b, s, RET_QK_WIDTH), bf),
        jax.ShapeDtypeStruct((b, s, RET_QK_WIDTH), bf),
        jax.ShapeDtypeStruct((b, s, RET_V_WIDTH), bf),
        jax.ShapeDtypeStruct((b, s, RET_V_WIDTH), bf),
        jax.ShapeDtypeStruct((b, s, LANES), bf),
        jax.ShapeDtypeStruct((b, s, LANES), jnp.float32),
    ]
    out_specs = [row(ATT_WIDTH), row(ATT_WIDTH), vt_spec, row(ATT_WIDTH),
                 row(IDX_WIDTH), row(RET_QK_WIDTH), row(RET_QK_WIDTH),
                 row(RET_V_WIDTH), row(RET_V_WIDTH), row(LANES), row(LANES)]
    return pl.pallas_call(
        _in_proj_kernel,
        grid=(s // tm, b),
        in_specs=[row(d),
                  pl.BlockSpec((1, d), lambda si, bi: (0, 0)),
                  pl.BlockSpec((d, PROJ_WIDTH), lambda si, bi: (0, 0)),
                  tab, tab],
        out_specs=out_specs,
        out_shape=out_shapes,
        scratch_shapes=[pltpu.VMEM((tm, d), jnp.bfloat16)],
        compiler_params=pltpu.CompilerParams(
            dimension_semantics=("arbitrary", "arbitrary"), vmem_limit_bytes=VMEM_LIMIT_BYTES),
        name="in_proj",
    )(x, gain, w, rope_a, rope_r)


_NT = (((1,), (1,)), ((), ()))


def _key_to_float(key):
    k = key ^ jnp.int32(-2 ** 31)
    bits = jnp.where(k >= 0, k, k ^ jnp.int32(2 ** 31 - 1))
    return lax.bitcast_convert_type(bits, jnp.float32)


def _split_heads_t(blk):
    blk_t = blk.astype(jnp.float32).T
    feat = lax.broadcasted_iota(jnp.int32, blk_t.shape, 0)
    zero = jnp.zeros_like(blk_t)
    both = jnp.concatenate([jnp.where(feat < LANES // 2, blk_t, zero),
                            jnp.where(feat >= LANES // 2, blk_t, zero)], axis=1)
    return both.astype(jnp.bfloat16)


def _dsa_kernel(qi_ref, wi_ref, ki_ref, qa_ref, ka_ref, vt_ref, ga_ref,
                out_ref, sc_ref, bias_ref, s_a, s_b, p_a, p_b, acc_ref, *, topk):
    i = pl.program_id(1)
    n_kc = sc_ref.shape[0]
    n_chunks = ((i + 1) * Q_BLOCK + KEY_CHUNK - 1) // KEY_CHUNK
    q_pos = i * Q_BLOCK + lax.broadcasted_iota(jnp.int32, (1, Q_BLOCK), 1)
    k_off = lax.broadcasted_iota(jnp.int32, (KEY_CHUNK, 1), 0)

    qi = qi_ref[0]
    qi_pairs = [_split_heads_t(qi[:, p * LANES:(p + 1) * LANES]) for p in range(IDX_HEADS // 2)]
    q_pairs = [_split_heads_t(qa_ref[0, :, p * LANES:(p + 1) * LANES]) for p in range(ATT_HEADS // 2)]
    w_t = wi_ref[0].T
    w_rows = [w_t[h:h + 1, :] for h in range(IDX_HEADS)]

    s_bufs = (s_a, s_b)
    n_steps = (n_chunks + 1) // 2

    def logits_to(c, buf):
        start = pl.multiple_of(c * KEY_CHUNK, KEY_CHUNK)
        ki = ki_ref[0, pl.ds(start, KEY_CHUNK), :]
        for p in range(IDX_HEADS // 2):
            buf[p] = jnp.dot(ki, qi_pairs[p], preferred_element_type=jnp.float32)

    def score_stage(c, cur):
        logits_to(jnp.minimum(c + 1, n_kc - 1), s_bufs[1 - cur])
        buf = s_bufs[cur]
        acc = None
        for h in range(IDX_HEADS):
            lanes = slice((h % 2) * Q_BLOCK, (h % 2 + 1) * Q_BLOCK)
            term = w_rows[h] * jnp.maximum(buf[h // 2, :, lanes], 0.0)
            acc = term if acc is None else acc + term
        sc_ref[c] = jnp.where(c * KEY_CHUNK + k_off <= q_pos, acc, -jnp.inf)

    def score_step(j, carry):
        score_stage(2 * j, 0)
        score_stage(2 * j + 1, 1)
        return carry

    logits_to(0, s_a)
    lax.fori_loop(0, n_steps, score_step, 0)

    def count(pred):
        rows = COUNT_ROWS

        def body(j, acc):
            for c in (2 * j, 2 * j + 1):
                hit = jnp.where(pred(sc_ref[c]), 1.0, 0.0)
                for r in range(KEY_CHUNK // rows):
                    acc = acc + hit[r * rows:(r + 1) * rows]
            return acc
        acc = lax.fori_loop(0, n_steps, body, jnp.zeros((rows, Q_BLOCK), jnp.float32))
        return jnp.sum(acc, axis=0, keepdims=True)

    @pl.when((i + 1) * Q_BLOCK <= topk)
    def _():
        bias_ref[0] = jnp.where(k_off <= q_pos, 0.0, MASK_VALUE)

    @pl.when((i + 1) * Q_BLOCK > topk)
    def _():
        def bit_step(b, carry):
            key, n_ge = carry
            cand = key | jnp.left_shift(jnp.int32(1), 31 - b)
            f = _key_to_float(cand)
            total = count(lambda s: s >= f)
            take = total >= topk
            return jnp.where(take, cand, key), jnp.where(take, total, n_ge)

        key, n_ge = lax.fori_loop(0, 32, bit_step, (jnp.zeros((1, Q_BLOCK), jnp.int32),
                                                     jnp.zeros((1, Q_BLOCK), jnp.float32)))
        thr = _key_to_float(key)
        surplus = n_ge - topk

        n_groups = KEY_CHUNK // SUBLANES
        sub = lax.broadcasted_iota(jnp.int32, (1, SUBLANES, Q_BLOCK), 1)
        keep = {k: jnp.where(sub < SUBLANES - k, 1.0, 0.0) for k in (1, 2, 4)}

        def body(j, later):
            c = n_chunks - 1 - j
            s = sc_ref[c]
            eq = s == thr
            cnt3 = jnp.where(eq, 1.0, 0.0).reshape(n_groups, SUBLANES, Q_BLOCK)
            for k in (1, 2, 4):
                cnt3 = cnt3 + pltpu.roll(cnt3, SUBLANES - k, 1) * keep[k]
            groups = [None] * n_groups
            for g in reversed(range(n_groups)):
                groups[g] = cnt3[g] + later
                later = later + cnt3[g, 0:1, :]
            from_here = jnp.concatenate(groups, axis=0)
            sel = (s > thr) | (eq & (from_here > surplus))
            bias_ref[c] = jnp.where(sel, 0.0, MASK_VALUE)
            return later

        lax.fori_loop(0, n_chunks, body, jnp.zeros((1, Q_BLOCK), jnp.float32))

    n_pairs = ATT_HEADS // 2
    half = ATT_HEAD_DIM

    pairs = range(n_pairs)
    p_bufs = (p_a, p_b)
    bias_ref[n_chunks] = jnp.full((KEY_CHUNK, Q_BLOCK), MASK_VALUE, jnp.float32)

    def scores_to(c, s_buf):
        start = pl.multiple_of(c * KEY_CHUNK, KEY_CHUNK)
        bias = bias_ref[c]
        bias2 = jnp.concatenate([bias, bias], axis=1)
        cmax = []
        for p in pairs:
            s = jnp.dot(ka_ref[0, pl.ds(start, KEY_CHUNK), p * LANES:(p + 1) * LANES], q_pairs[p],
                        preferred_element_type=jnp.float32) + bias2
            s_buf[p] = s
            cmax.append(jnp.max(s, axis=0, keepdims=True))
        return cmax

    ones_rows = jnp.ones((BF16_ROWS, KEY_CHUNK), jnp.bfloat16)

    def weighted_values(c, p_buf):
        return [jnp.dot(jnp.concatenate([vt_ref[0, c, p * LANES:(p + 1) * LANES, :], ones_rows], axis=0),
                        p_buf[p], preferred_element_type=jnp.float32) for p in pairs]

    def accumulate(l, alpha, o):
        l_new = []
        for p in pairs:
            acc_ref[p, 0] = alpha[p][:, :Q_BLOCK] * acc_ref[p, 0] + o[p][:half, :Q_BLOCK]
            acc_ref[p, 1] = alpha[p][:, Q_BLOCK:] * acc_ref[p, 1] + o[p][half:2 * half, Q_BLOCK:]
            l_new.append(alpha[p] * l[p] + o[p][2 * half:2 * half + 1, :])
        return l_new

    def softmax_to(s_buf, p_buf, cmax, m):
        m_new = [jnp.maximum(m[p], cmax[p]) for p in pairs]
        alpha = [jnp.exp2(m[p] - m_new[p]) for p in pairs]
        for p in pairs:
            p_buf[p] = jnp.exp2(s_buf[p] - m_new[p]).astype(jnp.bfloat16)
        return alpha, m_new

    def stage(c, cur, carry):
        cmax, alpha_prev, m, l = carry
        other = 1 - cur
        cmax_next = scores_to(jnp.minimum(c + 1, n_kc - 1), s_bufs[other])
        o_prev = weighted_values(jnp.maximum(c - 1, 0), p_bufs[other])
        alpha, m = softmax_to(s_bufs[cur], p_bufs[cur], cmax, m)
        return cmax_next, alpha, m, accumulate(l, alpha_prev, o_prev)

    def att_step(j, carry):
        return stage(2 * j + 1, 1, stage(2 * j, 0, carry))

    zeros_row = jnp.zeros((1, 2 * Q_BLOCK), jnp.float32)
    p_b[...] = jnp.zeros(p_b.shape, p_b.dtype)
    acc_ref[...] = jnp.zeros(acc_ref.shape, acc_ref.dtype)
    carry = (scores_to(0, s_a), [zeros_row + 1.0 for _ in pairs], [zeros_row + MASK_VALUE for _ in pairs],
             [zeros_row for _ in pairs])
    _, alpha_last, _, l = lax.fori_loop(0, n_steps, att_step, carry)
    l = accumulate(l, alpha_last, weighted_values(2 * n_steps - 1, p_b))
    for p in pairs:
        a0, a1 = acc_ref[p, 0], acc_ref[p, 1]
        psl = slice(p * LANES, (p + 1) * LANES)
        y_t = jnp.concatenate([a0 / l[p][:, :Q_BLOCK], a1 / l[p][:, Q_BLOCK:]], axis=0)
        out_ref[0, :, psl] = (y_t.T * ga_ref[0, :, psl].astype(jnp.float32)).astype(out_ref.dtype)


def _dsa(qi, wi, ki, qa, ka, vt, ga, topk):
    b, s, _ = qa.shape
    n_kc = s // KEY_CHUNK
    att_buf = (ATT_HEADS // 2, KEY_CHUNK, 2 * Q_BLOCK)
    qblk = lambda width: pl.BlockSpec((1, Q_BLOCK, width), lambda bi, i: (bi, i, 0))
    full = lambda width: pl.BlockSpec((1, s, width), lambda bi, i: (bi, 0, 0))
    return pl.pallas_call(
        functools.partial(_dsa_kernel, topk=topk),
        grid=(b, s // Q_BLOCK),
        in_specs=[qblk(IDX_WIDTH), qblk(LANES), full(LANES), qblk(ATT_WIDTH), full(ATT_WIDTH),
                  pl.BlockSpec((1, n_kc, ATT_WIDTH, KEY_CHUNK), lambda bi, i: (bi, 0, 0, 0)),
                  qblk(ATT_WIDTH)],
        out_specs=qblk(ATT_WIDTH),
        out_shape=jax.ShapeDtypeStruct((b, s, ATT_WIDTH), jnp.bfloat16),
        scratch_shapes=[pltpu.VMEM((n_kc, KEY_CHUNK, Q_BLOCK), jnp.float32),
                        pltpu.VMEM((n_kc + 1, KEY_CHUNK, Q_BLOCK), jnp.float32),
                        pltpu.VMEM(att_buf, jnp.float32), pltpu.VMEM(att_buf, jnp.float32),
                        pltpu.VMEM(att_buf, jnp.bfloat16), pltpu.VMEM(att_buf, jnp.bfloat16),
                        pltpu.VMEM((ATT_HEADS // 2, 2, ATT_HEAD_DIM, Q_BLOCK), jnp.float32)],
        compiler_params=pltpu.CompilerParams(
            dimension_semantics=("arbitrary", "arbitrary"), vmem_limit_bytes=VMEM_LIMIT_BYTES),
        name="sparse_attention",
    )(qi, wi, ki, qa, ka, vt, ga)


_TN = (((0,), (0,)), ((), ()))


def _retention_kernel(qr_ref, kr_ref, vr_ref, gr_ref, decay_ref, zeta_ref, xi_ref, grow_ref,
                      out_ref, state_ref):
    @pl.when(pl.program_id(1) == 0)
    def _():
        state_ref[...] = jnp.zeros_like(state_ref)

    lane = lax.broadcasted_iota(jnp.int32, (RET_CHUNK, LANES), 1)
    for pair in range(RET_HEADS // 2):
        psl = slice(pair * LANES, (pair + 1) * LANES)
        state = state_ref[psl, :]
        for r in range(RET_STEP_CHUNKS):
            rows = slice(r * RET_CHUNK, (r + 1) * RET_CHUNK)
            q_pair = qr_ref[0, rows, psl].astype(jnp.float32)
            k_pair = kr_ref[0, rows, psl].astype(jnp.float32)
            state_b = state.astype(jnp.bfloat16)
            kv = jnp.zeros((LANES, RET_V_DIM), jnp.float32)
            for sub in range(2):
                h = 2 * pair + sub
                vsl = slice(h * RET_V_DIM, (h + 1) * RET_V_DIM)
                in_head = (lane >= sub * RET_QK_DIM) & (lane < (sub + 1) * RET_QK_DIM)
                q_h = jnp.where(in_head, q_pair, 0.0)
                k_h = jnp.where(in_head, k_pair, 0.0)
                v = vr_ref[0, rows, vsl]
                scores = lax.dot_general(q_h.astype(jnp.bfloat16), k_h.astype(jnp.bfloat16), _NT,
                                         preferred_element_type=jnp.float32) * decay_ref[h]
                inner = jnp.dot(scores.astype(jnp.bfloat16), v, preferred_element_type=jnp.float32)
                cross = jnp.dot((q_h * xi_ref[h]).astype(jnp.bfloat16), state_b,
                                preferred_element_type=jnp.float32)
                o = inner + cross
                o = o * lax.rsqrt(jnp.mean(o * o, axis=-1, keepdims=True) + EPS)
                out_ref[0, rows, vsl] = (o * gr_ref[0, rows, vsl].astype(jnp.float32)).astype(out_ref.dtype)
                kv = kv + lax.dot_general((k_h * zeta_ref[h]).astype(jnp.bfloat16), v, _TN,
                                          preferred_element_type=jnp.float32)
            state = state * grow_ref[psl, :] + kv
        state_ref[psl, :] = state


def _retention(qr, kr, vr, gr, tables):
    b, s, _ = vr.shape
    decay, zeta_b, xi_b, g_rows = tables
    rows = RET_CHUNK * RET_STEP_CHUNKS
    blk = lambda width: pl.BlockSpec((1, rows, width), lambda bi, i: (bi, i, 0))
    const3 = lambda a: pl.BlockSpec(a.shape, lambda bi, i: (0, 0, 0))
    return pl.pallas_call(
        _retention_kernel,
        grid=(b, s // rows),
        in_specs=[blk(RET_QK_WIDTH), blk(RET_QK_WIDTH), blk(RET_V_WIDTH), blk(RET_V_WIDTH),
                  const3(decay), const3(zeta_b), const3(xi_b),
                  pl.BlockSpec(g_rows.shape, lambda bi, i: (0, 0))],
        out_specs=blk(RET_V_WIDTH),
        out_shape=jax.ShapeDtypeStruct((b, s, RET_V_WIDTH), jnp.bfloat16),
        scratch_shapes=[pltpu.VMEM((RET_HEADS * RET_QK_DIM, RET_V_DIM), jnp.float32)],
        compiler_params=pltpu.CompilerParams(
            dimension_semantics=("arbitrary", "arbitrary"), vmem_limit_bytes=VMEM_LIMIT_BYTES),
        name="retention",
    )(qr, kr, vr, gr, decay, zeta_b, xi_b, g_rows)


def _out_proj_kernel(x_ref, ya_ref, yr_ref, wa_ref, wr_ref, gain_ref, out_ref):
    h = (x_ref[...]
         + jnp.dot(ya_ref[...], wa_ref[...], preferred_element_type=jnp.float32)
         + jnp.dot(yr_ref[...], wr_ref[...], preferred_element_type=jnp.float32))
    ms = jnp.mean(h * h, axis=-1, keepdims=True)
    out_ref[...] = (h * lax.rsqrt(ms + EPS)) * gain_ref[...]


def _out_proj(x2, ya2, yr2, wa, wr, gain, tm):
    n, d = x2.shape
    row = lambda width: pl.BlockSpec((tm, width), lambda r: (r, 0))
    const = lambda a: pl.BlockSpec(a.shape, lambda r: (0, 0))
    return pl.pallas_call(
        _out_proj_kernel,
        grid=(n // tm,),
        in_specs=[row(d), row(ATT_WIDTH), row(RET_V_WIDTH), const(wa), const(wr), const(gain)],
        out_specs=row(d),
        out_shape=jax.ShapeDtypeStruct((n, d), jnp.float32),
        compiler_params=pltpu.CompilerParams(
            dimension_semantics=("arbitrary",), vmem_limit_bytes=VMEM_LIMIT_BYTES),
        name="out_proj",
    )(x2, ya2, yr2, wa, wr, gain)


def _prep_weight_kernel(w_ref, out_ref):
    offs = np.concatenate([[0], np.cumsum(SPLITS)]).tolist()
    q_a, k_a, v_a, g_a, q_i, k_i, w_i, q_r, k_r, v_r, g_r = [
        slice(offs[j], offs[j + 1]) for j in range(len(SPLITS))]
    dst = 0
    for src_cols in (q_a, k_a, v_a, g_a, q_i, q_r, k_r, v_r, g_r, k_i, k_i, w_i):
        width = src_cols.stop - src_cols.start
        out_ref[:, dst:dst + width] = w_ref[:, src_cols].astype(out_ref.dtype)
        dst += width
    out_ref[:, dst:] = jnp.zeros((out_ref.shape[0], PROJ_WIDTH - dst), out_ref.dtype)


def _prep_weight(w_in):
    d, width = w_in.shape
    tr = 128
    return pl.pallas_call(
        _prep_weight_kernel,
        grid=(d // tr,),
        in_specs=[pl.BlockSpec((tr, width), lambda r: (r, 0))],
        out_specs=pl.BlockSpec((tr, PROJ_WIDTH), lambda r: (r, 0)),
        out_shape=jax.ShapeDtypeStruct((d, PROJ_WIDTH), jnp.bfloat16),
        compiler_params=pltpu.CompilerParams(
            dimension_semantics=("arbitrary",), vmem_limit_bytes=VMEM_LIMIT_BYTES),
        name="prep_weight",
    )(w_in)


def kernel(x, norm_gain, w_in, w_out, final_gain):
    b, s, d = x.shape
    depth = norm_gain.shape[0]
    assert d == D_MODEL and s % Q_BLOCK == 0 and w_in.shape[2] == sum(SPLITS)
    topk = min(TOPK_MAX, s // 4)
    rope_a = _rotary_tables(s, ATT_HEAD_DIM, ATT_ROPE_DIM, ROPE_THETA)
    rope_r = _rotary_tables(s, RET_QK_DIM, RET_QK_DIM, RET_THETA)
    ret_tables = _retention_tables()
    assert depth == 1, "the final norm is fused into the single layer's output projection"
    w = _prep_weight(w_in[0])
    qa, ka, vt, ga, qi, qr, kr, vr, gr, ki, wi = _in_proj(
        x, norm_gain[0][None, :], w, rope_a, rope_r)
    ya = _dsa(qi, wi, ki, qa, ka, vt, ga, topk)
    yr = _retention(qr, kr, vr, gr, ret_tables)
    wo = w_out[0].astype(jnp.bfloat16)
    out = _out_proj(x.reshape(b * s, d), ya.reshape(b * s, ATT_WIDTH),
                    yr.reshape(b * s, RET_V_WIDTH), wo[:ATT_WIDTH], wo[ATT_WIDTH:],
                    final_gain[None, :], OUT_PROJ_ROWS)
    return out.reshape(b, s, d)
```

```python
import functools

import jax
import jax.numpy as jnp
import numpy as np
from jax import lax
from jax.experimental import pallas as pl
from jax.experimental.pallas import tpu as pltpu

D_MODEL = 1024
ATT_HEADS = 8
ATT_HEAD_DIM = 64
ATT_WIDTH = ATT_HEADS * ATT_HEAD_DIM
ATT_ROPE_DIM = ATT_HEAD_DIM // 4
ROPE_THETA = 500000.0
IDX_HEADS = 4
IDX_DIM = 64
IDX_WIDTH = IDX_HEADS * IDX_DIM
IDX_ROPE_DIM = IDX_DIM // 4
TOPK_MAX = 256
Q_BLOCK = 256
RET_HEADS = 4
RET_QK_DIM = 64
RET_V_DIM = 128
RET_QK_WIDTH = RET_HEADS * RET_QK_DIM
RET_V_WIDTH = RET_HEADS * RET_V_DIM
RET_CHUNK = 128
RET_THETA = 10000.0
MIX_WIDTH = ATT_WIDTH + RET_V_WIDTH
SPLITS = (ATT_WIDTH, ATT_WIDTH, ATT_WIDTH, ATT_WIDTH, IDX_WIDTH, IDX_DIM, IDX_HEADS,
          RET_QK_WIDTH, RET_QK_WIDTH, RET_V_WIDTH, RET_V_WIDTH)
EPS = 1e-6

LANES = 128
SUBLANES = 8
BF16_ROWS = 16
MXU_COLS = 256
LOG2_E = float(np.log2(np.e))
MASK_VALUE = -1e30
VMEM_LIMIT_BYTES = 48 * 1024 * 1024

OFF_QA = 0
OFF_KA = OFF_QA + ATT_WIDTH
OFF_VA = OFF_KA + ATT_WIDTH
OFF_GA = OFF_VA + ATT_WIDTH
OFF_QI = OFF_GA + ATT_WIDTH
OFF_QR = OFF_QI + IDX_WIDTH
OFF_KR = OFF_QR + RET_QK_WIDTH
OFF_VR = OFF_KR + RET_QK_WIDTH
OFF_GR = OFF_VR + RET_V_WIDTH
OFF_KI = OFF_GR + RET_V_WIDTH
OFF_WI = OFF_KI + LANES
PROJ_WIDTH = OFF_WI + LANES
KEY_CHUNK = 256
RET_STEP_CHUNKS = 8
COUNT_ROWS = 32


def _rotary_tables(seq, head_dim, rot_dim, theta):
    half = rot_dim // 2
    inv = 1.0 / (theta ** (jnp.arange(half, dtype=jnp.float32) / half))
    ang = jnp.arange(seq).astype(jnp.float32)[:, None] * inv[None, :]
    cos = jnp.cos(ang)
    sin = jnp.sin(ang)
    pad = head_dim - rot_dim
    ones = jnp.ones((seq, pad), jnp.float32)
    zeros = jnp.zeros((seq, pad), jnp.float32)
    zh = jnp.zeros((seq, half), jnp.float32)
    c = jnp.concatenate([cos, cos, ones], axis=1)
    s_prev = jnp.concatenate([zh, sin, zeros], axis=1)
    s_next = jnp.concatenate([-sin, zh, zeros], axis=1)
    reps = LANES // head_dim
    return jnp.stack([jnp.tile(c, (1, reps)), jnp.tile(s_prev, (1, reps)),
                      jnp.tile(s_next, (1, reps))])


def _retention_tables():
    c = RET_CHUNK
    gamma = 1.0 - 2.0 ** (-5.0 - jnp.arange(RET_HEADS, dtype=jnp.float32))
    log_g = jnp.log(gamma)
    idx = jnp.arange(c, dtype=jnp.float32)
    diff = idx[:, None] - idx[None, :]
    decay = jnp.where(diff[None] >= 0,
                      jnp.exp(log_g[:, None, None] * jnp.maximum(diff, 0.0)[None]), 0.0)
    zeta = jnp.exp(log_g[:, None] * (c - 1.0 - idx)[None, :])
    xi = jnp.exp(log_g[:, None] * (idx + 1.0)[None, :])
    g_chunk = jnp.exp(log_g * c)
    zeta_b = jnp.broadcast_to(zeta[:, :, None], (RET_HEADS, c, LANES))
    xi_b = jnp.broadcast_to(xi[:, :, None], (RET_HEADS, c, LANES))
    g_rows = jnp.repeat(g_chunk, RET_QK_DIM)[:, None]
    g_rows = jnp.broadcast_to(g_rows, (RET_HEADS * RET_QK_DIM, RET_V_DIM))
    return decay, zeta_b, xi_b, g_rows


def _rope(z, tab_ref, half):
    return (z * tab_ref[0] + pltpu.roll(z, half, 1) * tab_ref[1]
            + pltpu.roll(z, LANES - half, 1) * tab_ref[2])


def _silu(g):
    return g * (1.0 / (1.0 + jnp.exp(-g)))


def _in_proj_kernel(x_ref, gain_ref, w_ref, rope_a_ref, rope_r_ref,
                    qa_ref, ka_ref, vt_ref, ga_ref, qi_ref, qr_ref, kr_ref, vr_ref, gr_ref,
                    ki_ref, wi_ref, u_ref):
    x = x_ref[0]
    ms = jnp.mean(x * x, axis=-1, keepdims=True)
    u_ref[...] = ((x * lax.rsqrt(ms + EPS)) * gain_ref[...]).astype(jnp.bfloat16)

    def proj(off, width):
        out = []
        for j in range(width // MXU_COLS):
            c0 = off + j * MXU_COLS
            z = jnp.dot(u_ref[...], w_ref[:, c0:c0 + MXU_COLS], preferred_element_type=jnp.float32)
            for h in range(MXU_COLS // LANES):
                lo = j * MXU_COLS + h * LANES
                out.append((slice(lo, lo + LANES), z[:, h * LANES:(h + 1) * LANES]))
        return out

    a_half = ATT_ROPE_DIM // 2
    r_half = RET_QK_DIM // 2
    att_scale = ATT_HEAD_DIM ** -0.5 * LOG2_E
    ret_scale = RET_QK_DIM ** -0.5
    for sl, z in proj(OFF_QA, ATT_WIDTH):
        qa_ref[0, :, sl] = (_rope(z, rope_a_ref, a_half) * att_scale).astype(qa_ref.dtype)
    for sl, z in proj(OFF_KA, ATT_WIDTH):
        ka_ref[0, :, sl] = _rope(z, rope_a_ref, a_half).astype(ka_ref.dtype)
    for sl, z in proj(OFF_VA, ATT_WIDTH):
        vt_ref[0, 0, sl, :] = z.T.astype(vt_ref.dtype)
    for sl, z in proj(OFF_GA, ATT_WIDTH):
        ga_ref[0, :, sl] = _silu(z).astype(ga_ref.dtype)
    for sl, z in proj(OFF_QI, IDX_WIDTH):
        qi_ref[0, :, sl] = _rope(z, rope_a_ref, a_half).astype(qi_ref.dtype)
    for sl, z in proj(OFF_QR, RET_QK_WIDTH):
        qr_ref[0, :, sl] = _rope(z, rope_r_ref, r_half).astype(qr_ref.dtype)
    for sl, z in proj(OFF_KR, RET_QK_WIDTH):
        kr_ref[0, :, sl] = (_rope(z, rope_r_ref, r_half) * ret_scale).astype(kr_ref.dtype)
    for sl, z in proj(OFF_VR, RET_V_WIDTH):
        vr_ref[0, :, sl] = z.astype(vr_ref.dtype)
    for sl, z in proj(OFF_GR, RET_V_WIDTH):
        gr_ref[0, :, sl] = _silu(z).astype(gr_ref.dtype)
    (_, ki), (_, wi) = proj(OFF_KI, 2 * LANES)
    ki_ref[0] = _rope(ki, rope_a_ref, a_half).astype(ki_ref.dtype)
    wi_ref[0] = wi * ((IDX_DIM ** -0.5) * (IDX_HEADS ** -0.5))


def _in_proj(x, gain, w, rope_a, rope_r):
    b, s, d = x.shape
    tm = KEY_CHUNK
    row = lambda width: pl.BlockSpec((1, tm, width), lambda si, bi: (bi, si, 0))
    tab = pl.BlockSpec((3, tm, LANES), lambda si, bi: (0, si, 0))
    vt_spec = pl.BlockSpec((1, 1, ATT_WIDTH, tm), lambda si, bi: (bi, si, 0, 0))
    bf = jnp.bfloat16
    out_shapes = [
        jax.ShapeDtypeStruct((b, s, ATT_WIDTH), bf),
        jax.ShapeDtypeStruct((b, s, ATT_WIDTH), bf),
        jax.ShapeDtypeStruct((b, s // tm, ATT_WIDTH, tm), bf),
        jax.ShapeDtypeStruct((b, s, ATT_WIDTH), bf),
        jax.ShapeDtypeStruct((b, s, IDX_WIDTH), bf),
        jax.ShapeDtypeStruct((b, s, RET_QK_WIDTH), bf),
        jax.ShapeDtypeStruct((b, s, RET_QK_WIDTH), bf),
        jax.ShapeDtypeStruct((b, s, RET_V_WIDTH), bf),
        jax.ShapeDtypeStruct((b, s, RET_V_WIDTH), bf),
        jax.ShapeDtypeStruct((b, s, LANES), bf),
        jax.ShapeDtypeStruct((b, s, LANES), jnp.float32),
    ]
    out_specs = [row(ATT_WIDTH), row(ATT_WIDTH), vt_spec, row(ATT_WIDTH),
                 row(IDX_WIDTH), row(RET_QK_WIDTH), row(RET_QK_WIDTH),
                 row(RET_V_WIDTH), row(RET_V_WIDTH), row(LANES), row(LANES)]
    return pl.pallas_call(
        _in_proj_kernel,
        grid=(s // tm, b),
        in_specs=[row(d),
                  pl.BlockSpec((1, d), lambda si, bi: (0, 0)),
                  pl.BlockSpec((d, PROJ_WIDTH), lambda si, bi: (0, 0)),
                  tab, tab],
        out_specs=out_specs,
        out_shape=out_shapes,
        scratch_shapes=[pltpu.VMEM((tm, d), jnp.bfloat16)],
        compiler_params=pltpu.CompilerParams(
            dimension_semantics=("arbitrary", "arbitrary"), vmem_limit_bytes=VMEM_LIMIT_BYTES),
        name="in_proj",
    )(x, gain, w, rope_a, rope_r)


_NT = (((1,), (1,)), ((), ()))


def _key_to_float(key):
    k = key ^ jnp.int32(-2 ** 31)
    bits = jnp.where(k >= 0, k, k ^ jnp.int32(2 ** 31 - 1))
    return lax.bitcast_convert_type(bits, jnp.float32)


def _split_heads_t(blk):
    blk_t = blk.astype(jnp.float32).T
    feat = lax.broadcasted_iota(jnp.int32, blk_t.shape, 0)
    zero = jnp.zeros_like(blk_t)
    both = jnp.concatenate([jnp.where(feat < LANES // 2, blk_t, zero),
                            jnp.where(feat >= LANES // 2, blk_t, zero)], axis=1)
    return both.astype(jnp.bfloat16)


def _dsa_kernel(qi_ref, wi_ref, ki_ref, qa_ref, ka_ref, vt_ref, ga_ref,
                out_ref, sc_ref, bias_ref, s_a, s_b, p_a, p_b, acc_ref, *, topk):
    i = pl.program_id(1)
    n_kc = sc_ref.shape[0]
    n_chunks = ((i + 1) * Q_BLOCK + KEY_CHUNK - 1) // KEY_CHUNK
    q_pos = i * Q_BLOCK + lax.broadcasted_iota(jnp.int32, (1, Q_BLOCK), 1)
    k_off = lax.broadcasted_iota(jnp.int32, (KEY_CHUNK, 1), 0)

    qi = qi_ref[0]
    qi_pairs = [_split_heads_t(qi[:, p * LANES:(p + 1) * LANES]) for p in range(IDX_HEADS // 2)]
    q_pairs = [_split_heads_t(qa_ref[0, :, p * LANES:(p + 1) * LANES]) for p in range(ATT_HEADS // 2)]
    w_t = wi_ref[0].T
    w_rows = [w_t[h:h + 1, :] for h in range(IDX_HEADS)]

    s_bufs = (s_a, s_b)
    n_steps = (n_chunks + 1) // 2

    def logits_to(c, buf):
        start = pl.multiple_of(c * KEY_CHUNK, KEY_CHUNK)
        ki = ki_ref[0, pl.ds(start, KEY_CHUNK), :]
        for p in range(IDX_HEADS // 2):
            buf[p] = jnp.dot(ki, qi_pairs[p], preferred_element_type=jnp.float32)

    def score_stage(c, cur):
        logits_to(jnp.minimum(c + 1, n_kc - 1), s_bufs[1 - cur])
        buf = s_bufs[cur]
        acc = None
        for h in range(IDX_HEADS):
            lanes = slice((h % 2) * Q_BLOCK, (h % 2 + 1) * Q_BLOCK)
            term = w_rows[h] * jnp.maximum(buf[h // 2, :, lanes], 0.0)
            acc = term if acc is None else acc + term
        sc_ref[c] = jnp.where(c * KEY_CHUNK + k_off <= q_pos, acc, -jnp.inf)

    def score_step(j, carry):
        score_stage(2 * j, 0)
        score_stage(2 * j + 1, 1)
        return carry

    logits_to(0, s_a)
    lax.fori_loop(0, n_steps, score_step, 0)

    def count(pred):
        rows = COUNT_ROWS

        def add_chunk(c, acc):
            hit = jnp.where(pred(sc_ref[c]), 1.0, 0.0)
            for r in range(KEY_CHUNK // rows):
                acc = acc + hit[r * rows:(r + 1) * rows]
            return acc

        acc = lax.fori_loop(0, n_chunks // 2, lambda j, a: add_chunk(2 * j + 1, add_chunk(2 * j, a)),
                            jnp.zeros((rows, Q_BLOCK), jnp.float32))
        acc = lax.fori_loop(0, n_chunks % 2, lambda _, a: add_chunk(n_chunks - 1, a), acc)
        return jnp.sum(acc, axis=0, keepdims=True)

    @pl.when((i + 1) * Q_BLOCK <= topk)
    def _():
        bias_ref[0] = jnp.where(k_off <= q_pos, 0.0, MASK_VALUE)

    @pl.when((i + 1) * Q_BLOCK > topk)
    def _():
        def bit_step(b, carry):
            key, n_ge = carry
            cand = key | jnp.left_shift(jnp.int32(1), 31 - b)
            f = _key_to_float(cand)
            total = count(lambda s: s >= f)
            take = total >= topk
            return jnp.where(take, cand, key), jnp.where(take, total, n_ge)

        key, n_ge = lax.fori_loop(0, 32, bit_step, (jnp.zeros((1, Q_BLOCK), jnp.int32),
                                                     jnp.zeros((1, Q_BLOCK), jnp.float32)))
        thr = _key_to_float(key)
        surplus = n_ge - topk

        n_groups = KEY_CHUNK // SUBLANES
        sub = lax.broadcasted_iota(jnp.int32, (1, SUBLANES, Q_BLOCK), 1)
        keep = {k: jnp.where(sub < SUBLANES - k, 1.0, 0.0) for k in (1, 2, 4)}

        def body(j, later):
            c = n_chunks - 1 - j
            s = sc_ref[c]
            eq = s == thr
            cnt3 = jnp.where(eq, 1.0, 0.0).reshape(n_groups, SUBLANES, Q_BLOCK)
            for k in (1, 2, 4):
                cnt3 = cnt3 + pltpu.roll(cnt3, SUBLANES - k, 1) * keep[k]
            groups = [None] * n_groups
            for g in reversed(range(n_groups)):
                groups[g] = cnt3[g] + later
                later = later + cnt3[g, 0:1, :]
            from_here = jnp.concatenate(groups, axis=0)
            sel = (s > thr) | (eq & (from_here > surplus))
            bias_ref[c] = jnp.where(sel, 0.0, MASK_VALUE)
            return later

        lax.fori_loop(0, n_chunks, body, jnp.zeros((1, Q_BLOCK), jnp.float32))

    n_pairs = ATT_HEADS // 2
    half = ATT_HEAD_DIM

    pairs = range(n_pairs)
    p_bufs = (p_a, p_b)
    bias_ref[n_chunks] = jnp.full((KEY_CHUNK, Q_BLOCK), MASK_VALUE, jnp.float32)

    def scores_to(c, s_buf):
        start = pl.multiple_of(c * KEY_CHUNK, KEY_CHUNK)
        bias = bias_ref[c]
        bias2 = jnp.concatenate([bias, bias], axis=1)
        cmax = []
        for p in pairs:
            s = jnp.dot(ka_ref[0, pl.ds(start, KEY_CHUNK), p * LANES:(p + 1) * LANES], q_pairs[p],
                        preferred_element_type=jnp.float32) + bias2
            s_buf[p] = s
            cmax.append(jnp.max(s, axis=0, keepdims=True))
        return cmax

    ones_rows = jnp.ones((BF16_ROWS, KEY_CHUNK), jnp.bfloat16)

    def weighted_values(c, p_buf):
        return [jnp.dot(jnp.concatenate([vt_ref[0, c, p * LANES:(p + 1) * LANES, :], ones_rows], axis=0),
                        p_buf[p], preferred_element_type=jnp.float32) for p in pairs]

    def accumulate(l, alpha, o):
        l_new = []
        for p in pairs:
            acc_ref[p, 0] = alpha[p][:, :Q_BLOCK] * acc_ref[p, 0] + o[p][:half, :Q_BLOCK]
            acc_ref[p, 1] = alpha[p][:, Q_BLOCK:] * acc_ref[p, 1] + o[p][half:2 * half, Q_BLOCK:]
            l_new.append(alpha[p] * l[p] + o[p][2 * half:2 * half + 1, :])
        return l_new

    def softmax_to(s_buf, p_buf, cmax, m):
        m_new = [jnp.maximum(m[p], cmax[p]) for p in pairs]
        alpha = [jnp.exp2(m[p] - m_new[p]) for p in pairs]
        for p in pairs:
            p_buf[p] = jnp.exp2(s_buf[p] - m_new[p]).astype(jnp.bfloat16)
        return alpha, m_new

    def stage(c, cur, carry):
        cmax, alpha_prev, m, l = carry
        other = 1 - cur
        cmax_next = scores_to(jnp.minimum(c + 1, n_kc - 1), s_bufs[other])
        o_prev = weighted_values(jnp.maximum(c - 1, 0), p_bufs[other])
        alpha, m = softmax_to(s_bufs[cur], p_bufs[cur], cmax, m)
        return cmax_next, alpha, m, accumulate(l, alpha_prev, o_prev)

    def att_step(j, carry):
        return stage(2 * j + 1, 1, stage(2 * j, 0, carry))

    zeros_row = jnp.zeros((1, 2 * Q_BLOCK), jnp.float32)
    p_b[...] = jnp.zeros(p_b.shape, p_b.dtype)
    acc_ref[...] = jnp.zeros(acc_ref.shape, acc_ref.dtype)
    carry = (scores_to(0, s_a), [zeros_row + 1.0 for _ in pairs], [zeros_row + MASK_VALUE for _ in pairs],
             [zeros_row for _ in pairs])
    _, alpha_last, _, l = lax.fori_loop(0, n_steps, att_step, carry)
    l = accumulate(l, alpha_last, weighted_values(2 * n_steps - 1, p_b))
    for p in pairs:
        a0, a1 = acc_ref[p, 0], acc_ref[p, 1]
        psl = slice(p * LANES, (p + 1) * LANES)
        y_t = jnp.concatenate([a0 / l[p][:, :Q_BLOCK], a1 / l[p][:, Q_BLOCK:]], axis=0)
        out_ref[0, :, psl] = (y_t.T * ga_ref[0, :, psl].astype(jnp.float32)).astype(out_ref.dtype)


def _dsa(qi, wi, ki, qa, ka, vt, ga, topk):
    b, s, _ = qa.shape
    n_kc = s // KEY_CHUNK
    att_buf = (ATT_HEADS // 2, KEY_CHUNK, 2 * Q_BLOCK)
    qblk = lambda width: pl.BlockSpec((1, Q_BLOCK, width), lambda bi, i: (bi, i, 0))
    full = lambda width: pl.BlockSpec((1, s, width), lambda bi, i: (bi, 0, 0))
    return pl.pallas_call(
        functools.partial(_dsa_kernel, topk=topk),
        grid=(b, s // Q_BLOCK),
        in_specs=[qblk(IDX_WIDTH), qblk(LANES), full(LANES), qblk(ATT_WIDTH), full(ATT_WIDTH),
                  pl.BlockSpec((1, n_kc, ATT_WIDTH, KEY_CHUNK), lambda bi, i: (bi, 0, 0, 0)),
                  qblk(ATT_WIDTH)],
        out_specs=qblk(ATT_WIDTH),
        out_shape=jax.ShapeDtypeStruct((b, s, ATT_WIDTH), jnp.bfloat16),
        scratch_shapes=[pltpu.VMEM((n_kc, KEY_CHUNK, Q_BLOCK), jnp.float32),
                        pltpu.VMEM((n_kc + 1, KEY_CHUNK, Q_BLOCK), jnp.float32),
                        pltpu.VMEM(att_buf, jnp.float32), pltpu.VMEM(att_buf, jnp.float32),
                        pltpu.VMEM(att_buf, jnp.bfloat16), pltpu.VMEM(att_buf, jnp.bfloat16),
                        pltpu.VMEM((ATT_HEADS // 2, 2, ATT_HEAD_DIM, Q_BLOCK), jnp.float32)],
        compiler_params=pltpu.CompilerParams(
            dimension_semantics=("arbitrary", "arbitrary"), vmem_limit_bytes=VMEM_LIMIT_BYTES),
        name="sparse_attention",
    )(qi, wi, ki, qa, ka, vt, ga)


_TN = (((0,), (0,)), ((), ()))


def _retention_out_kernel(qr_ref, kr_ref, vr_ref, gr_ref, decay_ref, zeta_ref, xi_ref, grow_ref,
                          x_ref, ya_ref, wa_ref, wr_ref, gain_ref, out_ref, state_ref, yr_ref):
    @pl.when(pl.program_id(1) == 0)
    def _():
        state_ref[...] = jnp.zeros_like(state_ref)

    lane = lax.broadcasted_iota(jnp.int32, (RET_CHUNK, LANES), 1)
    for pair in range(RET_HEADS // 2):
        psl = slice(pair * LANES, (pair + 1) * LANES)
        state = state_ref[psl, :]
        for r in range(RET_STEP_CHUNKS):
            rows = slice(r * RET_CHUNK, (r + 1) * RET_CHUNK)
            q_pair = qr_ref[0, rows, psl].astype(jnp.float32)
            k_pair = kr_ref[0, rows, psl].astype(jnp.float32)
            state_b = state.astype(jnp.bfloat16)
            kv = jnp.zeros((LANES, RET_V_DIM), jnp.float32)
            for sub in range(2):
                h = 2 * pair + sub
                vsl = slice(h * RET_V_DIM, (h + 1) * RET_V_DIM)
                in_head = (lane >= sub * RET_QK_DIM) & (lane < (sub + 1) * RET_QK_DIM)
                q_h = jnp.where(in_head, q_pair, 0.0)
                k_h = jnp.where(in_head, k_pair, 0.0)
                v = vr_ref[0, rows, vsl]
                scores = lax.dot_general(q_h.astype(jnp.bfloat16), k_h.astype(jnp.bfloat16), _NT,
                                         preferred_element_type=jnp.float32) * decay_ref[h]
                inner = jnp.dot(scores.astype(jnp.bfloat16), v, preferred_element_type=jnp.float32)
                cross = jnp.dot((q_h * xi_ref[h]).astype(jnp.bfloat16), state_b,
                                preferred_element_type=jnp.float32)
                o = inner + cross
                o = o * lax.rsqrt(jnp.mean(o * o, axis=-1, keepdims=True) + EPS)
                yr_ref[rows, vsl] = (o * gr_ref[0, rows, vsl].astype(jnp.float32)).astype(yr_ref.dtype)
                kv = kv + lax.dot_general((k_h * zeta_ref[h]).astype(jnp.bfloat16), v, _TN,
                                          preferred_element_type=jnp.float32)
            state = state * grow_ref[psl, :] + kv
        state_ref[psl, :] = state

    h = (x_ref[0]
         + jnp.dot(ya_ref[0], wa_ref[...], preferred_element_type=jnp.float32)
         + jnp.dot(yr_ref[...], wr_ref[...], preferred_element_type=jnp.float32))
    ms = jnp.mean(h * h, axis=-1, keepdims=True)
    out_ref[0] = (h * lax.rsqrt(ms + EPS)) * gain_ref[...]


def _retention_out(qr, kr, vr, gr, tables, x, ya, wa, wr, gain):
    b, s, d = x.shape
    decay, zeta_b, xi_b, g_rows = tables
    rows = RET_CHUNK * RET_STEP_CHUNKS
    blk = lambda width: pl.BlockSpec((1, rows, width), lambda bi, i: (bi, i, 0))
    const3 = lambda a: pl.BlockSpec(a.shape, lambda bi, i: (0, 0, 0))
    const2 = lambda a: pl.BlockSpec(a.shape, lambda bi, i: (0, 0))
    return pl.pallas_call(
        _retention_out_kernel,
        grid=(b, s // rows),
        in_specs=[blk(RET_QK_WIDTH), blk(RET_QK_WIDTH), blk(RET_V_WIDTH), blk(RET_V_WIDTH),
                  const3(decay), const3(zeta_b), const3(xi_b), const2(g_rows),
                  blk(d), blk(ATT_WIDTH), const2(wa), const2(wr), const2(gain)],
        out_specs=blk(d),
        out_shape=jax.ShapeDtypeStruct((b, s, d), jnp.float32),
        scratch_shapes=[pltpu.VMEM((RET_HEADS * RET_QK_DIM, RET_V_DIM), jnp.float32),
                        pltpu.VMEM((rows, RET_V_WIDTH), jnp.bfloat16)],
        compiler_params=pltpu.CompilerParams(
            dimension_semantics=("arbitrary", "arbitrary"), vmem_limit_bytes=VMEM_LIMIT_BYTES),
        name="retention_out_proj",
    )(qr, kr, vr, gr, decay, zeta_b, xi_b, g_rows, x, ya, wa, wr, gain)


def _prep_weight_kernel(w_ref, out_ref):
    offs = np.concatenate([[0], np.cumsum(SPLITS)]).tolist()
    q_a, k_a, v_a, g_a, q_i, k_i, w_i, q_r, k_r, v_r, g_r = [
        slice(offs[j], offs[j + 1]) for j in range(len(SPLITS))]
    dst = 0
    for src_cols in (q_a, k_a, v_a, g_a, q_i, q_r, k_r, v_r, g_r, k_i, k_i, w_i):
        width = src_cols.stop - src_cols.start
        out_ref[:, dst:dst + width] = w_ref[:, src_cols].astype(out_ref.dtype)
        dst += width
    out_ref[:, dst:] = jnp.zeros((out_ref.shape[0], PROJ_WIDTH - dst), out_ref.dtype)


def _prep_weight(w_in):
    d, width = w_in.shape
    tr = 128
    return pl.pallas_call(
        _prep_weight_kernel,
        grid=(d // tr,),
        in_specs=[pl.BlockSpec((tr, width), lambda r: (r, 0))],
        out_specs=pl.BlockSpec((tr, PROJ_WIDTH), lambda r: (r, 0)),
        out_shape=jax.ShapeDtypeStruct((d, PROJ_WIDTH), jnp.bfloat16),
        compiler_params=pltpu.CompilerParams(
            dimension_semantics=("arbitrary",), vmem_limit_bytes=VMEM_LIMIT_BYTES),
        name="prep_weight",
    )(w_in)


def kernel(x, norm_gain, w_in, w_out, final_gain):
    b, s, d = x.shape
    depth = norm_gain.shape[0]
    assert d == D_MODEL and s % Q_BLOCK == 0 and w_in.shape[2] == sum(SPLITS)
    topk = min(TOPK_MAX, s // 4)
    rope_a = _rotary_tables(s, ATT_HEAD_DIM, ATT_ROPE_DIM, ROPE_THETA)
    rope_r = _rotary_tables(s, RET_QK_DIM, RET_QK_DIM, RET_THETA)
    ret_tables = _retention_tables()
    assert depth == 1, "the final norm is fused into the single layer's output projection"
    w = _prep_weight(w_in[0])
    qa, ka, vt, ga, qi, qr, kr, vr, gr, ki, wi = _in_proj(
        x, norm_gain[0][None, :], w, rope_a, rope_r)
    ya = _dsa(qi, wi, ki, qa, ka, vt, ga, topk)
    wo = w_out[0].astype(jnp.bfloat16)
    return _retention_out(qr, kr, vr, gr, ret_tables, x, ya, wo[:ATT_WIDTH], wo[ATT_WIDTH:],
                          final_gain[None, :])
```

```python
import functools

import jax
import jax.numpy as jnp
import numpy as np
from jax import lax
from jax.experimental import pallas as pl
from jax.experimental.pallas import tpu as pltpu

D_MODEL = 1024
ATT_HEADS = 8
ATT_HEAD_DIM = 64
ATT_WIDTH = ATT_HEADS * ATT_HEAD_DIM
ATT_ROPE_DIM = ATT_HEAD_DIM // 4
ROPE_THETA = 500000.0
IDX_HEADS = 4
IDX_DIM = 64
IDX_WIDTH = IDX_HEADS * IDX_DIM
IDX_ROPE_DIM = IDX_DIM // 4
TOPK_MAX = 256
Q_BLOCK = 256
RET_HEADS = 4
RET_QK_DIM = 64
RET_V_DIM = 128
RET_QK_WIDTH = RET_HEADS * RET_QK_DIM
RET_V_WIDTH = RET_HEADS * RET_V_DIM
RET_CHUNK = 128
RET_THETA = 10000.0
MIX_WIDTH = ATT_WIDTH + RET_V_WIDTH
SPLITS = (ATT_WIDTH, ATT_WIDTH, ATT_WIDTH, ATT_WIDTH, IDX_WIDTH, IDX_DIM, IDX_HEADS,
          RET_QK_WIDTH, RET_QK_WIDTH, RET_V_WIDTH, RET_V_WIDTH)
EPS = 1e-6

LANES = 128
SUBLANES = 8
BF16_ROWS = 16
MXU_COLS = 256
LOG2_E = float(np.log2(np.e))
MASK_VALUE = -1e30
VMEM_LIMIT_BYTES = 48 * 1024 * 1024

OFF_QA = 0
OFF_KA = OFF_QA + ATT_WIDTH
OFF_VA = OFF_KA + ATT_WIDTH
OFF_GA = OFF_VA + ATT_WIDTH
OFF_QI = OFF_GA + ATT_WIDTH
OFF_QR = OFF_QI + IDX_WIDTH
OFF_KR = OFF_QR + RET_QK_WIDTH
OFF_VR = OFF_KR + RET_QK_WIDTH
OFF_GR = OFF_VR + RET_V_WIDTH
OFF_KI = OFF_GR + RET_V_WIDTH
OFF_WI = OFF_KI + LANES
PROJ_WIDTH = OFF_WI + LANES
KEY_CHUNK = 256
IN_PROJ_ROWS = 512
RET_STEP_CHUNKS = 8
COUNT_ROWS = 32


def _rotary_tables(seq, head_dim, rot_dim, theta):
    half = rot_dim // 2
    inv = 1.0 / (theta ** (jnp.arange(half, dtype=jnp.float32) / half))
    ang = jnp.arange(seq).astype(jnp.float32)[:, None] * inv[None, :]
    cos = jnp.cos(ang)
    sin = jnp.sin(ang)
    pad = head_dim - rot_dim
    ones = jnp.ones((seq, pad), jnp.float32)
    zeros = jnp.zeros((seq, pad), jnp.float32)
    zh = jnp.zeros((seq, half), jnp.float32)
    c = jnp.concatenate([cos, cos, ones], axis=1)
    s_prev = jnp.concatenate([zh, sin, zeros], axis=1)
    s_next = jnp.concatenate([-sin, zh, zeros], axis=1)
    reps = LANES // head_dim
    return jnp.stack([jnp.tile(c, (1, reps)), jnp.tile(s_prev, (1, reps)),
                      jnp.tile(s_next, (1, reps))])


def _retention_tables():
    c = RET_CHUNK
    gamma = 1.0 - 2.0 ** (-5.0 - jnp.arange(RET_HEADS, dtype=jnp.float32))
    log_g = jnp.log(gamma)
    idx = jnp.arange(c, dtype=jnp.float32)
    diff = idx[:, None] - idx[None, :]
    decay = jnp.where(diff[None] >= 0,
                      jnp.exp(log_g[:, None, None] * jnp.maximum(diff, 0.0)[None]), 0.0)
    zeta = jnp.exp(log_g[:, None] * (c - 1.0 - idx)[None, :])
    xi = jnp.exp(log_g[:, None] * (idx + 1.0)[None, :])
    g_chunk = jnp.exp(log_g * c)
    zeta_b = jnp.broadcast_to(zeta[:, :, None], (RET_HEADS, c, LANES))
    xi_b = jnp.broadcast_to(xi[:, :, None], (RET_HEADS, c, LANES))
    g_rows = jnp.repeat(g_chunk, RET_QK_DIM)[:, None]
    g_rows = jnp.broadcast_to(g_rows, (RET_HEADS * RET_QK_DIM, RET_V_DIM))
    return decay, zeta_b, xi_b, g_rows


def _rope(z, tab_ref, half):
    return (z * tab_ref[0] + pltpu.roll(z, half, 1) * tab_ref[1]
            + pltpu.roll(z, LANES - half, 1) * tab_ref[2])


def _silu(g):
    return g * (1.0 / (1.0 + jnp.exp(-g)))


def _in_proj_kernel(x_ref, gain_ref, w_ref, rope_a_ref, rope_r_ref,
                    qa_ref, ka_ref, vt_ref, ga_ref, qi_ref, qr_ref, kr_ref, vr_ref, gr_ref,
                    ki_ref, wi_ref, u_ref):
    x = x_ref[0]
    ms = jnp.mean(x * x, axis=-1, keepdims=True)
    u_ref[...] = ((x * lax.rsqrt(ms + EPS)) * gain_ref[...]).astype(jnp.bfloat16)

    def proj(off, width):
        out = []
        for j in range(width // MXU_COLS):
            c0 = off + j * MXU_COLS
            z = jnp.dot(u_ref[...], w_ref[:, c0:c0 + MXU_COLS], preferred_element_type=jnp.float32)
            for h in range(MXU_COLS // LANES):
                lo = j * MXU_COLS + h * LANES
                out.append((slice(lo, lo + LANES), z[:, h * LANES:(h + 1) * LANES]))
        return out

    a_half = ATT_ROPE_DIM // 2
    r_half = RET_QK_DIM // 2
    att_scale = ATT_HEAD_DIM ** -0.5 * LOG2_E
    ret_scale = RET_QK_DIM ** -0.5
    for sl, z in proj(OFF_QA, ATT_WIDTH):
        qa_ref[0, :, sl] = (_rope(z, rope_a_ref, a_half) * att_scale).astype(qa_ref.dtype)
    for sl, z in proj(OFF_KA, ATT_WIDTH):
        ka_ref[0, :, sl] = _rope(z, rope_a_ref, a_half).astype(ka_ref.dtype)
    for sl, z in proj(OFF_VA, ATT_WIDTH):
        for kc in range(IN_PROJ_ROWS // KEY_CHUNK):
            vt_ref[0, kc, sl, :] = z[kc * KEY_CHUNK:(kc + 1) * KEY_CHUNK].T.astype(vt_ref.dtype)
    for sl, z in proj(OFF_GA, ATT_WIDTH):
        ga_ref[0, :, sl] = _silu(z).astype(ga_ref.dtype)
    for sl, z in proj(OFF_QI, IDX_WIDTH):
        qi_ref[0, :, sl] = _rope(z, rope_a_ref, a_half).astype(qi_ref.dtype)
    for sl, z in proj(OFF_QR, RET_QK_WIDTH):
        qr_ref[0, :, sl] = _rope(z, rope_r_ref, r_half).astype(qr_ref.dtype)
    for sl, z in proj(OFF_KR, RET_QK_WIDTH):
        kr_ref[0, :, sl] = (_rope(z, rope_r_ref, r_half) * ret_scale).astype(kr_ref.dtype)
    for sl, z in proj(OFF_VR, RET_V_WIDTH):
        vr_ref[0, :, sl] = z.astype(vr_ref.dtype)
    for sl, z in proj(OFF_GR, RET_V_WIDTH):
        gr_ref[0, :, sl] = _silu(z).astype(gr_ref.dtype)
    (_, ki), (_, wi) = proj(OFF_KI, 2 * LANES)
    ki_ref[0] = _rope(ki, rope_a_ref, a_half).astype(ki_ref.dtype)
    wi_ref[0] = wi * ((IDX_DIM ** -0.5) * (IDX_HEADS ** -0.5))


def _in_proj(x, gain, w, rope_a, rope_r):
    b, s, d = x.shape
    tm = IN_PROJ_ROWS
    row = lambda width: pl.BlockSpec((1, tm, width), lambda si, bi: (bi, si, 0))
    tab = pl.BlockSpec((3, tm, LANES), lambda si, bi: (0, si, 0))
    vt_spec = pl.BlockSpec((1, tm // KEY_CHUNK, ATT_WIDTH, KEY_CHUNK), lambda si, bi: (bi, si, 0, 0))
    bf = jnp.bfloat16
    out_shapes = [
        jax.ShapeDtypeStruct((b, s, ATT_WIDTH), bf),
        jax.ShapeDtypeStruct((b, s, ATT_WIDTH), bf),
        jax.ShapeDtypeStruct((b, s // KEY_CHUNK, ATT_WIDTH, KEY_CHUNK), bf),
        jax.ShapeDtypeStruct((b, s, ATT_WIDTH), bf),
        jax.ShapeDtypeStruct((b, s, IDX_WIDTH), bf),
        jax.ShapeDtypeStruct((b, s, RET_QK_WIDTH), bf),
        jax.ShapeDtypeStruct((b, s, RET_QK_WIDTH), bf),
        jax.ShapeDtypeStruct((b, s, RET_V_WIDTH), bf),
        jax.ShapeDtypeStruct((b, s, RET_V_WIDTH), bf),
        jax.ShapeDtypeStruct((b, s, LANES), bf),
        jax.ShapeDtypeStruct((b, s, LANES), jnp.float32),
    ]
    out_specs = [row(ATT_WIDTH), row(ATT_WIDTH), vt_spec, row(ATT_WIDTH),
                 row(IDX_WIDTH), row(RET_QK_WIDTH), row(RET_QK_WIDTH),
                 row(RET_V_WIDTH), row(RET_V_WIDTH), row(LANES), row(LANES)]
    return pl.pallas_call(
        _in_proj_kernel,
        grid=(s // tm, b),
        in_specs=[row(d),
                  pl.BlockSpec((1, d), lambda si, bi: (0, 0)),
                  pl.BlockSpec((d, PROJ_WIDTH), lambda si, bi: (0, 0)),
                  tab, tab],
        out_specs=out_specs,
        out_shape=out_shapes,
        scratch_shapes=[pltpu.VMEM((tm, d), jnp.bfloat16)],
        compiler_params=pltpu.CompilerParams(
            dimension_semantics=("arbitrary", "arbitrary"), vmem_limit_bytes=VMEM_LIMIT_BYTES),
        name="in_proj",
    )(x, gain, w, rope_a, rope_r)


_NT = (((1,), (1,)), ((), ()))


def _key_to_float(key):
    k = key ^ jnp.int32(-2 ** 31)
    bits = jnp.where(k >= 0, k, k ^ jnp.int32(2 ** 31 - 1))
    return lax.bitcast_convert_type(bits, jnp.float32)


def _split_heads_t(blk):
    blk_t = blk.astype(jnp.float32).T
    feat = lax.broadcasted_iota(jnp.int32, blk_t.shape, 0)
    zero = jnp.zeros_like(blk_t)
    both = jnp.concatenate([jnp.where(feat < LANES // 2, blk_t, zero),
                            jnp.where(feat >= LANES // 2, blk_t, zero)], axis=1)
    return both.astype(jnp.bfloat16)


def _dsa_kernel(qi_ref, wi_ref, ki_ref, qa_ref, ka_ref, vt_ref, ga_ref,
                out_ref, sc_ref, bias_ref, s_a, s_b, p_a, p_b, acc_ref, *, topk):
    i = pl.program_id(1)
    n_kc = sc_ref.shape[0]
    n_chunks = ((i + 1) * Q_BLOCK + KEY_CHUNK - 1) // KEY_CHUNK
    q_pos = i * Q_BLOCK + lax.broadcasted_iota(jnp.int32, (1, Q_BLOCK), 1)
    k_off = lax.broadcasted_iota(jnp.int32, (KEY_CHUNK, 1), 0)

    qi = qi_ref[0]
    qi_pairs = [_split_heads_t(qi[:, p * LANES:(p + 1) * LANES]) for p in range(IDX_HEADS // 2)]
    q_pairs = [_split_heads_t(qa_ref[0, :, p * LANES:(p + 1) * LANES]) for p in range(ATT_HEADS // 2)]
    w_t = wi_ref[0].T
    w_rows = [w_t[h:h + 1, :] for h in range(IDX_HEADS)]

    s_bufs = (s_a, s_b)
    n_steps = (n_chunks + 1) // 2

    def logits_to(c, buf):
        start = pl.multiple_of(c * KEY_CHUNK, KEY_CHUNK)
        ki = ki_ref[0, pl.ds(start, KEY_CHUNK), :]
        for p in range(IDX_HEADS // 2):
            buf[p] = jnp.dot(ki, qi_pairs[p], preferred_element_type=jnp.float32)

    def score_stage(c, cur):
        logits_to(jnp.minimum(c + 1, n_kc - 1), s_bufs[1 - cur])
        buf = s_bufs[cur]
        acc = None
        for h in range(IDX_HEADS):
            lanes = slice((h % 2) * Q_BLOCK, (h % 2 + 1) * Q_BLOCK)
            term = w_rows[h] * jnp.maximum(buf[h // 2, :, lanes], 0.0)
            acc = term if acc is None else acc + term
        sc_ref[c] = jnp.where(c * KEY_CHUNK + k_off <= q_pos, acc, -jnp.inf)

    def score_step(j, carry):
        score_stage(2 * j, 0)
        score_stage(2 * j + 1, 1)
        return carry

    logits_to(0, s_a)
    lax.fori_loop(0, n_steps, score_step, 0)

    def count(pred):
        rows = COUNT_ROWS

        def add_chunk(c, acc):
            hit = jnp.where(pred(sc_ref[c]), 1.0, 0.0)
            for r in range(KEY_CHUNK // rows):
                acc = acc + hit[r * rows:(r + 1) * rows]
            return acc

        acc = lax.fori_loop(0, n_chunks // 2, lambda j, a: add_chunk(2 * j + 1, add_chunk(2 * j, a)),
                            jnp.zeros((rows, Q_BLOCK), jnp.float32))
        acc = lax.fori_loop(0, n_chunks % 2, lambda _, a: add_chunk(n_chunks - 1, a), acc)
        return jnp.sum(acc, axis=0, keepdims=True)

    @pl.when((i + 1) * Q_BLOCK <= topk)
    def _():
        bias_ref[0] = jnp.where(k_off <= q_pos, 0.0, MASK_VALUE)

    @pl.when((i + 1) * Q_BLOCK > topk)
    def _():
        def bit_step(b, carry):
            key, n_ge = carry
            cand = key | jnp.left_shift(jnp.int32(1), 31 - b)
            f = _key_to_float(cand)
            total = count(lambda s: s >= f)
            take = total >= topk
            return jnp.where(take, cand, key), jnp.where(take, total, n_ge)

        key, n_ge = lax.fori_loop(0, 32, bit_step, (jnp.zeros((1, Q_BLOCK), jnp.int32),
                                                     jnp.zeros((1, Q_BLOCK), jnp.float32)))
        thr = _key_to_float(key)
        surplus = n_ge - topk

        n_groups = KEY_CHUNK // SUBLANES
        sub = lax.broadcasted_iota(jnp.int32, (1, SUBLANES, Q_BLOCK), 1)
        keep = {k: jnp.where(sub < SUBLANES - k, 1.0, 0.0) for k in (1, 2, 4)}

        def body(j, later):
            c = n_chunks - 1 - j
            s = sc_ref[c]
            eq = s == thr
            cnt3 = jnp.where(eq, 1.0, 0.0).reshape(n_groups, SUBLANES, Q_BLOCK)
            for k in (1, 2, 4):
                cnt3 = cnt3 + pltpu.roll(cnt3, SUBLANES - k, 1) * keep[k]
            groups = [None] * n_groups
            for g in reversed(range(n_groups)):
                groups[g] = cnt3[g] + later
                later = later + cnt3[g, 0:1, :]
            from_here = jnp.concatenate(groups, axis=0)
            sel = (s > thr) | (eq & (from_here > surplus))
            bias_ref[c] = jnp.where(sel, 0.0, MASK_VALUE)
            return later

        lax.fori_loop(0, n_chunks, body, jnp.zeros((1, Q_BLOCK), jnp.float32))

    n_pairs = ATT_HEADS // 2
    half = ATT_HEAD_DIM

    pairs = range(n_pairs)
    p_bufs = (p_a, p_b)
    bias_ref[n_chunks] = jnp.full((KEY_CHUNK, Q_BLOCK), MASK_VALUE, jnp.float32)

    def scores_to(c, s_buf):
        start = pl.multiple_of(c * KEY_CHUNK, KEY_CHUNK)
        bias = bias_ref[c]
        bias2 = jnp.concatenate([bias, bias], axis=1)
        cmax = []
        for p in pairs:
            s = jnp.dot(ka_ref[0, pl.ds(start, KEY_CHUNK), p * LANES:(p + 1) * LANES], q_pairs[p],
                        preferred_element_type=jnp.float32) + bias2
            s_buf[p] = s
            cmax.append(jnp.max(s, axis=0, keepdims=True))
        return cmax

    ones_rows = jnp.ones((BF16_ROWS, KEY_CHUNK), jnp.bfloat16)

    def weighted_values(c, p_buf):
        return [jnp.dot(jnp.concatenate([vt_ref[0, c, p * LANES:(p + 1) * LANES, :], ones_rows], axis=0),
                        p_buf[p], preferred_element_type=jnp.float32) for p in pairs]

    def accumulate(l, alpha, o):
        l_new = []
        for p in pairs:
            acc_ref[p, 0] = alpha[p][:, :Q_BLOCK] * acc_ref[p, 0] + o[p][:half, :Q_BLOCK]
            acc_ref[p, 1] = alpha[p][:, Q_BLOCK:] * acc_ref[p, 1] + o[p][half:2 * half, Q_BLOCK:]
            l_new.append(alpha[p] * l[p] + o[p][2 * half:2 * half + 1, :])
        return l_new

    def softmax_to(s_buf, p_buf, cmax, m):
        m_new = [jnp.maximum(m[p], cmax[p]) for p in pairs]
        alpha = [jnp.exp2(m[p] - m_new[p]) for p in pairs]
        for p in pairs:
            p_buf[p] = jnp.exp2(s_buf[p] - m_new[p]).astype(jnp.bfloat16)
        return alpha, m_new

    def stage(c, cur, carry):
        cmax, alpha_prev, m, l = carry
        other = 1 - cur
        cmax_next = scores_to(jnp.minimum(c + 1, n_kc - 1), s_bufs[other])
        o_prev = weighted_values(jnp.maximum(c - 1, 0), p_bufs[other])
        alpha, m = softmax_to(s_bufs[cur], p_bufs[cur], cmax, m)
        return cmax_next, alpha, m, accumulate(l, alpha_prev, o_prev)

    def att_step(j, carry):
        return stage(2 * j + 1, 1, stage(2 * j, 0, carry))

    zeros_row = jnp.zeros((1, 2 * Q_BLOCK), jnp.float32)
    p_b[...] = jnp.zeros(p_b.shape, p_b.dtype)
    acc_ref[...] = jnp.zeros(acc_ref.shape, acc_ref.dtype)
    carry = (scores_to(0, s_a), [zeros_row + 1.0 for _ in pairs], [zeros_row + MASK_VALUE for _ in pairs],
             [zeros_row for _ in pairs])
    _, alpha_last, _, l = lax.fori_loop(0, n_steps, att_step, carry)
    l = accumulate(l, alpha_last, weighted_values(2 * n_steps - 1, p_b))
    for p in pairs:
        a0, a1 = acc_ref[p, 0], acc_ref[p, 1]
        psl = slice(p * LANES, (p + 1) * LANES)
        y_t = jnp.concatenate([a0 / l[p][:, :Q_BLOCK], a1 / l[p][:, Q_BLOCK:]], axis=0)
        out_ref[0, :, psl] = (y_t.T * ga_ref[0, :, psl].astype(jnp.float32)).astype(out_ref.dtype)


def _dsa(qi, wi, ki, qa, ka, vt, ga, topk):
    b, s, _ = qa.shape
    n_kc = s // KEY_CHUNK
    att_buf = (ATT_HEADS // 2, KEY_CHUNK, 2 * Q_BLOCK)
    qblk = lambda width: pl.BlockSpec((1, Q_BLOCK, width), lambda bi, i: (bi, i, 0))
    full = lambda width: pl.BlockSpec((1, s, width), lambda bi, i: (bi, 0, 0))
    return pl.pallas_call(
        functools.partial(_dsa_kernel, topk=topk),
        grid=(b, s // Q_BLOCK),
        in_specs=[qblk(IDX_WIDTH), qblk(LANES), full(LANES), qblk(ATT_WIDTH), full(ATT_WIDTH),
                  pl.BlockSpec((1, n_kc, ATT_WIDTH, KEY_CHUNK), lambda bi, i: (bi, 0, 0, 0)),
                  qblk(ATT_WIDTH)],
        out_specs=qblk(ATT_WIDTH),
        out_shape=jax.ShapeDtypeStruct((b, s, ATT_WIDTH), jnp.bfloat16),
        scratch_shapes=[pltpu.VMEM((n_kc, KEY_CHUNK, Q_BLOCK), jnp.float32),
                        pltpu.VMEM((n_kc + 1, KEY_CHUNK, Q_BLOCK), jnp.float32),
                        pltpu.VMEM(att_buf, jnp.float32), pltpu.VMEM(att_buf, jnp.float32),
                        pltpu.VMEM(att_buf, jnp.bfloat16), pltpu.VMEM(att_buf, jnp.bfloat16),
                        pltpu.VMEM((ATT_HEADS // 2, 2, ATT_HEAD_DIM, Q_BLOCK), jnp.float32)],
        compiler_params=pltpu.CompilerParams(
            dimension_semantics=("arbitrary", "arbitrary"), vmem_limit_bytes=VMEM_LIMIT_BYTES),
        name="sparse_attention",
    )(qi, wi, ki, qa, ka, vt, ga)


_TN = (((0,), (0,)), ((), ()))


def _retention_out_kernel(qr_ref, kr_ref, vr_ref, gr_ref, decay_ref, zeta_ref, xi_ref, grow_ref,
                          x_ref, ya_ref, wa_ref, wr_ref, gain_ref, out_ref, state_ref, yr_ref):
    @pl.when(pl.program_id(1) == 0)
    def _():
        state_ref[...] = jnp.zeros_like(state_ref)

    lane = lax.broadcasted_iota(jnp.int32, (RET_CHUNK, LANES), 1)
    for pair in range(RET_HEADS // 2):
        psl = slice(pair * LANES, (pair + 1) * LANES)
        state = state_ref[psl, :]
        for r in range(RET_STEP_CHUNKS):
            rows = slice(r * RET_CHUNK, (r + 1) * RET_CHUNK)
            q_pair = qr_ref[0, rows, psl].astype(jnp.float32)
            k_pair = kr_ref[0, rows, psl].astype(jnp.float32)
            state_b = state.astype(jnp.bfloat16)
            kv = jnp.zeros((LANES, RET_V_DIM), jnp.float32)
            for sub in range(2):
                h = 2 * pair + sub
                vsl = slice(h * RET_V_DIM, (h + 1) * RET_V_DIM)
                in_head = (lane >= sub * RET_QK_DIM) & (lane < (sub + 1) * RET_QK_DIM)
                q_h = jnp.where(in_head, q_pair, 0.0)
                k_h = jnp.where(in_head, k_pair, 0.0)
                v = vr_ref[0, rows, vsl]
                scores = lax.dot_general(q_h.astype(jnp.bfloat16), k_h.astype(jnp.bfloat16), _NT,
                                         preferred_element_type=jnp.float32) * decay_ref[h]
                inner = jnp.dot(scores.astype(jnp.bfloat16), v, preferred_element_type=jnp.float32)
                cross = jnp.dot((q_h * xi_ref[h]).astype(jnp.bfloat16), state_b,
                                preferred_element_type=jnp.float32)
                o = inner + cross
                o = o * lax.rsqrt(jnp.mean(o * o, axis=-1, keepdims=True) + EPS)
                yr_ref[rows, vsl] = (o * gr_ref[0, rows, vsl].astype(jnp.float32)).astype(yr_ref.dtype)
                kv = kv + lax.dot_general((k_h * zeta_ref[h]).astype(jnp.bfloat16), v, _TN,
                                          preferred_element_type=jnp.float32)
            state = state * grow_ref[psl, :] + kv
        state_ref[psl, :] = state

    h = (x_ref[0]
         + jnp.dot(ya_ref[0], wa_ref[...], preferred_element_type=jnp.float32)
         + jnp.dot(yr_ref[...], wr_ref[...], preferred_element_type=jnp.float32))
    ms = jnp.mean(h * h, axis=-1, keepdims=True)
    out_ref[0] = (h * lax.rsqrt(ms + EPS)) * gain_ref[...]


def _retention_out(qr, kr, vr, gr, tables, x, ya, wa, wr, gain):
    b, s, d = x.shape
    decay, zeta_b, xi_b, g_rows = tables
    rows = RET_CHUNK * RET_STEP_CHUNKS
    blk = lambda width: pl.BlockSpec((1, rows, width), lambda bi, i: (bi, i, 0))
    const3 = lambda a: pl.BlockSpec(a.shape, lambda bi, i: (0, 0, 0))
    const2 = lambda a: pl.BlockSpec(a.shape, lambda bi, i: (0, 0))
    return pl.pallas_call(
        _retention_out_kernel,
        grid=(b, s // rows),
        in_specs=[blk(RET_QK_WIDTH), blk(RET_QK_WIDTH), blk(RET_V_WIDTH), blk(RET_V_WIDTH),
                  const3(decay), const3(zeta_b), const3(xi_b), const2(g_rows),
                  blk(d), blk(ATT_WIDTH), const2(wa), const2(wr), const2(gain)],
        out_specs=blk(d),
        out_shape=jax.ShapeDtypeStruct((b, s, d), jnp.float32),
        scratch_shapes=[pltpu.VMEM((RET_HEADS * RET_QK_DIM, RET_V_DIM), jnp.float32),
                        pltpu.VMEM((rows, RET_V_WIDTH), jnp.bfloat16)],
        compiler_params=pltpu.CompilerParams(
            dimension_semantics=("arbitrary", "arbitrary"), vmem_limit_bytes=VMEM_LIMIT_BYTES),
        name="retention_out_proj",
    )(qr, kr, vr, gr, decay, zeta_b, xi_b, g_rows, x, ya, wa, wr, gain)


def _prep_weight_kernel(w_ref, out_ref):
    offs = np.concatenate([[0], np.cumsum(SPLITS)]).tolist()
    q_a, k_a, v_a, g_a, q_i, k_i, w_i, q_r, k_r, v_r, g_r = [
        slice(offs[j], offs[j + 1]) for j in range(len(SPLITS))]
    dst = 0
    for src_cols in (q_a, k_a, v_a, g_a, q_i, q_r, k_r, v_r, g_r, k_i, k_i, w_i):
        width = src_cols.stop - src_cols.start
        out_ref[:, dst:dst + width] = w_ref[:, src_cols].astype(out_ref.dtype)
        dst += width
    out_ref[:, dst:] = jnp.zeros((out_ref.shape[0], PROJ_WIDTH - dst), out_ref.dtype)


def _prep_weight(w_in):
    d, width = w_in.shape
    tr = 128
    return pl.pallas_call(
        _prep_weight_kernel,
        grid=(d // tr,),
        in_specs=[pl.BlockSpec((tr, width), lambda r: (r, 0))],
        out_specs=pl.BlockSpec((tr, PROJ_WIDTH), lambda r: (r, 0)),
        out_shape=jax.ShapeDtypeStruct((d, PROJ_WIDTH), jnp.bfloat16),
        compiler_params=pltpu.CompilerParams(
            dimension_semantics=("arbitrary",), vmem_limit_bytes=VMEM_LIMIT_BYTES),
        name="prep_weight",
    )(w_in)


def kernel(x, norm_gain, w_in, w_out, final_gain):
    b, s, d = x.shape
    depth = norm_gain.shape[0]
    assert d == D_MODEL and s % Q_BLOCK == 0 and w_in.shape[2] == sum(SPLITS)
    topk = min(TOPK_MAX, s // 4)
    rope_a = _rotary_tables(s, ATT_HEAD_DIM, ATT_ROPE_DIM, ROPE_THETA)
    rope_r = _rotary_tables(s, RET_QK_DIM, RET_QK_DIM, RET_THETA)
    ret_tables = _retention_tables()
    assert depth == 1, "the final norm is fused into the single layer's output projection"
    w = _prep_weight(w_in[0])
    qa, ka, vt, ga, qi, qr, kr, vr, gr, ki, wi = _in_proj(
        x, norm_gain[0][None, :], w, rope_a, rope_r)
    ya = _dsa(qi, wi, ki, qa, ka, vt, ga, topk)
    wo = w_out[0].astype(jnp.bfloat16)
    return _retention_out(qr, kr, vr, gr, ret_tables, x, ya, wo[:ATT_WIDTH], wo[ATT_WIDTH:],
                          final_gain[None, :])
```

```python
import functools

import jax
import jax.numpy as jnp
import numpy as np
from jax import lax
from jax.experimental import pallas as pl
from jax.experimental.pallas import tpu as pltpu

D_MODEL = 1024
ATT_HEADS = 8
ATT_HEAD_DIM = 64
ATT_WIDTH = ATT_HEADS * ATT_HEAD_DIM
ATT_ROPE_DIM = ATT_HEAD_DIM // 4
ROPE_THETA = 500000.0
IDX_HEADS = 4
IDX_DIM = 64
IDX_WIDTH = IDX_HEADS * IDX_DIM
IDX_ROPE_DIM = IDX_DIM // 4
TOPK_MAX = 256
Q_BLOCK = 256
RET_HEADS = 4
RET_QK_DIM = 64
RET_V_DIM = 128
RET_QK_WIDTH = RET_HEADS * RET_QK_DIM
RET_V_WIDTH = RET_HEADS * RET_V_DIM
RET_CHUNK = 128
RET_THETA = 10000.0
MIX_WIDTH = ATT_WIDTH + RET_V_WIDTH
SPLITS = (ATT_WIDTH, ATT_WIDTH, ATT_WIDTH, ATT_WIDTH, IDX_WIDTH, IDX_DIM, IDX_HEADS,
          RET_QK_WIDTH, RET_QK_WIDTH, RET_V_WIDTH, RET_V_WIDTH)
EPS = 1e-6

LANES = 128
SUBLANES = 8
BF16_ROWS = 16
MXU_COLS = 256
LOG2_E = float(np.log2(np.e))
NO_SURPLUS = 2.0 ** 30
MASK_VALUE = -1e30
VMEM_LIMIT_BYTES = 48 * 1024 * 1024

OFF_QA = 0
OFF_KA = OFF_QA + ATT_WIDTH
OFF_VA = OFF_KA + ATT_WIDTH
OFF_GA = OFF_VA + ATT_WIDTH
OFF_QI = OFF_GA + ATT_WIDTH
OFF_QR = OFF_QI + IDX_WIDTH
OFF_KR = OFF_QR + RET_QK_WIDTH
OFF_VR = OFF_KR + RET_QK_WIDTH
OFF_GR = OFF_VR + RET_V_WIDTH
OFF_KI = OFF_GR + RET_V_WIDTH
OFF_WI = OFF_KI + LANES
PROJ_WIDTH = OFF_WI + LANES
KEY_CHUNK = 256
IN_PROJ_ROWS = 512
RET_STEP_CHUNKS = 8
COUNT_ROWS = 32


def _rotary_tables(seq, head_dim, rot_dim, theta):
    half = rot_dim // 2
    inv = 1.0 / (theta ** (jnp.arange(half, dtype=jnp.float32) / half))
    ang = jnp.arange(seq).astype(jnp.float32)[:, None] * inv[None, :]
    cos = jnp.cos(ang)
    sin = jnp.sin(ang)
    pad = head_dim - rot_dim
    ones = jnp.ones((seq, pad), jnp.float32)
    zeros = jnp.zeros((seq, pad), jnp.float32)
    zh = jnp.zeros((seq, half), jnp.float32)
    c = jnp.concatenate([cos, cos, ones], axis=1)
    s_prev = jnp.concatenate([zh, sin, zeros], axis=1)
    s_next = jnp.concatenate([-sin, zh, zeros], axis=1)
    reps = LANES // head_dim
    return jnp.stack([jnp.tile(c, (1, reps)), jnp.tile(s_prev, (1, reps)),
                      jnp.tile(s_next, (1, reps))])


def _retention_tables():
    c = RET_CHUNK
    gamma = 1.0 - 2.0 ** (-5.0 - jnp.arange(RET_HEADS, dtype=jnp.float32))
    log_g = jnp.log(gamma)
    idx = jnp.arange(c, dtype=jnp.float32)
    diff = idx[:, None] - idx[None, :]
    decay = jnp.where(diff[None] >= 0,
                      jnp.exp(log_g[:, None, None] * jnp.maximum(diff, 0.0)[None]), 0.0)
    zeta = jnp.exp(log_g[:, None] * (c - 1.0 - idx)[None, :])
    xi = jnp.exp(log_g[:, None] * (idx + 1.0)[None, :])
    g_chunk = jnp.exp(log_g * c)
    zeta_b = jnp.broadcast_to(zeta[:, :, None], (RET_HEADS, c, LANES))
    xi_b = jnp.broadcast_to(xi[:, :, None], (RET_HEADS, c, LANES))
    g_rows = jnp.repeat(g_chunk, RET_QK_DIM)[:, None]
    g_rows = jnp.broadcast_to(g_rows, (RET_HEADS * RET_QK_DIM, RET_V_DIM))
    return decay, zeta_b, xi_b, g_rows


def _rope(z, tab_ref, half):
    return (z * tab_ref[0] + pltpu.roll(z, half, 1) * tab_ref[1]
            + pltpu.roll(z, LANES - half, 1) * tab_ref[2])


def _silu(g):
    return g * (1.0 / (1.0 + jnp.exp(-g)))


def _in_proj_kernel(x_ref, gain_ref, w_ref, rope_a_ref, rope_r_ref,
                    qa_ref, ka_ref, vt_ref, ga_ref, qi_ref, qr_ref, kr_ref, vr_ref, gr_ref,
                    ki_ref, wi_ref, u_ref):
    x = x_ref[0]
    ms = jnp.mean(x * x, axis=-1, keepdims=True)
    u_ref[...] = ((x * lax.rsqrt(ms + EPS)) * gain_ref[...]).astype(jnp.bfloat16)

    def proj(off, width):
        out = []
        for j in range(width // MXU_COLS):
            c0 = off + j * MXU_COLS
            z = jnp.dot(u_ref[...], w_ref[:, c0:c0 + MXU_COLS], preferred_element_type=jnp.float32)
            for h in range(MXU_COLS // LANES):
                lo = j * MXU_COLS + h * LANES
                out.append((slice(lo, lo + LANES), z[:, h * LANES:(h + 1) * LANES]))
        return out

    a_half = ATT_ROPE_DIM // 2
    r_half = RET_QK_DIM // 2
    att_scale = ATT_HEAD_DIM ** -0.5 * LOG2_E
    ret_scale = RET_QK_DIM ** -0.5
    for sl, z in proj(OFF_QA, ATT_WIDTH):
        qa_ref[0, :, sl] = (_rope(z, rope_a_ref, a_half) * att_scale).astype(qa_ref.dtype)
    for sl, z in proj(OFF_KA, ATT_WIDTH):
        ka_ref[0, :, sl] = _rope(z, rope_a_ref, a_half).astype(ka_ref.dtype)
    for sl, z in proj(OFF_VA, ATT_WIDTH):
        for kc in range(IN_PROJ_ROWS // KEY_CHUNK):
            vt_ref[0, kc, sl, :] = z[kc * KEY_CHUNK:(kc + 1) * KEY_CHUNK].T.astype(vt_ref.dtype)
    for sl, z in proj(OFF_GA, ATT_WIDTH):
        ga_ref[0, :, sl] = _silu(z).astype(ga_ref.dtype)
    for sl, z in proj(OFF_QI, IDX_WIDTH):
        qi_ref[0, :, sl] = _rope(z, rope_a_ref, a_half).astype(qi_ref.dtype)
    for sl, z in proj(OFF_QR, RET_QK_WIDTH):
        qr_ref[0, :, sl] = _rope(z, rope_r_ref, r_half).astype(qr_ref.dtype)
    for sl, z in proj(OFF_KR, RET_QK_WIDTH):
        kr_ref[0, :, sl] = (_rope(z, rope_r_ref, r_half) * ret_scale).astype(kr_ref.dtype)
    for sl, z in proj(OFF_VR, RET_V_WIDTH):
        vr_ref[0, :, sl] = z.astype(vr_ref.dtype)
    for sl, z in proj(OFF_GR, RET_V_WIDTH):
        gr_ref[0, :, sl] = _silu(z).astype(gr_ref.dtype)
    (_, ki), (_, wi) = proj(OFF_KI, 2 * LANES)
    ki_ref[0] = _rope(ki, rope_a_ref, a_half).astype(ki_ref.dtype)
    wi_ref[0] = wi * ((IDX_DIM ** -0.5) * (IDX_HEADS ** -0.5))


def _in_proj(x, gain, w, rope_a, rope_r):
    b, s, d = x.shape
    tm = IN_PROJ_ROWS
    row = lambda width: pl.BlockSpec((1, tm, width), lambda si, bi: (bi, si, 0))
    tab = pl.BlockSpec((3, tm, LANES), lambda si, bi: (0, si, 0))
    vt_spec = pl.BlockSpec((1, tm // KEY_CHUNK, ATT_WIDTH, KEY_CHUNK), lambda si, bi: (bi, si, 0, 0))
    bf = jnp.bfloat16
    out_shapes = [
        jax.ShapeDtypeStruct((b, s, ATT_WIDTH), bf),
        jax.ShapeDtypeStruct((b, s, ATT_WIDTH), bf),
        jax.ShapeDtypeStruct((b, s // KEY_CHUNK, ATT_WIDTH, KEY_CHUNK), bf),
        jax.ShapeDtypeStruct((b, s, ATT_WIDTH), bf),
        jax.ShapeDtypeStruct((b, s, IDX_WIDTH), bf),
        jax.ShapeDtypeStruct((b, s, RET_QK_WIDTH), bf),
        jax.ShapeDtypeStruct((b, s, RET_QK_WIDTH), bf),
        jax.ShapeDtypeStruct((b, s, RET_V_WIDTH), bf),
        jax.ShapeDtypeStruct((b, s, RET_V_WIDTH), bf),
        jax.ShapeDtypeStruct((b, s, LANES), bf),
        jax.ShapeDtypeStruct((b, s, LANES), jnp.float32),
    ]
    out_specs = [row(ATT_WIDTH), row(ATT_WIDTH), vt_spec, row(ATT_WIDTH),
                 row(IDX_WIDTH), row(RET_QK_WIDTH), row(RET_QK_WIDTH),
                 row(RET_V_WIDTH), row(RET_V_WIDTH), row(LANES), row(LANES)]
    return pl.pallas_call(
        _in_proj_kernel,
        grid=(s // tm, b),
        in_specs=[row(d),
                  pl.BlockSpec((1, d), lambda si, bi: (0, 0)),
                  pl.BlockSpec((d, PROJ_WIDTH), lambda si, bi: (0, 0)),
                  tab, tab],
        out_specs=out_specs,
        out_shape=out_shapes,
        scratch_shapes=[pltpu.VMEM((tm, d), jnp.bfloat16)],
        compiler_params=pltpu.CompilerParams(
            dimension_semantics=("arbitrary", "arbitrary"), vmem_limit_bytes=VMEM_LIMIT_BYTES),
        name="in_proj",
    )(x, gain, w, rope_a, rope_r)


_NT = (((1,), (1,)), ((), ()))


def _key_to_float(key):
    k = key ^ jnp.int32(-2 ** 31)
    bits = jnp.where(k >= 0, k, k ^ jnp.int32(2 ** 31 - 1))
    return lax.bitcast_convert_type(bits, jnp.float32)


def _split_heads_t(blk):
    blk_t = blk.astype(jnp.float32).T
    feat = lax.broadcasted_iota(jnp.int32, blk_t.shape, 0)
    zero = jnp.zeros_like(blk_t)
    both = jnp.concatenate([jnp.where(feat < LANES // 2, blk_t, zero),
                            jnp.where(feat >= LANES // 2, blk_t, zero)], axis=1)
    return both.astype(jnp.bfloat16)


def _dsa_kernel(qi_ref, wi_ref, ki_ref, qa_ref, ka_ref, vt_ref, ga_ref,
                out_ref, sc_ref, par_ref, s_a, s_b, p_a, p_b, acc_ref, *, topk):
    i = pl.program_id(1)
    n_kc = sc_ref.shape[0]
    n_chunks = ((i + 1) * Q_BLOCK + KEY_CHUNK - 1) // KEY_CHUNK
    q_pos = i * Q_BLOCK + lax.broadcasted_iota(jnp.int32, (1, Q_BLOCK), 1)
    k_off = lax.broadcasted_iota(jnp.int32, (KEY_CHUNK, 1), 0)

    qi = qi_ref[0]
    qi_pairs = [_split_heads_t(qi[:, p * LANES:(p + 1) * LANES]) for p in range(IDX_HEADS // 2)]
    q_pairs = [_split_heads_t(qa_ref[0, :, p * LANES:(p + 1) * LANES]) for p in range(ATT_HEADS // 2)]
    w_t = wi_ref[0].T
    w_rows = [w_t[h:h + 1, :] for h in range(IDX_HEADS)]

    s_bufs = (s_a, s_b)
    n_steps = (n_chunks + 1) // 2

    def logits_to(c, buf):
        start = pl.multiple_of(c * KEY_CHUNK, KEY_CHUNK)
        ki = ki_ref[0, pl.ds(start, KEY_CHUNK), :]
        for p in range(IDX_HEADS // 2):
            buf[p] = jnp.dot(ki, qi_pairs[p], preferred_element_type=jnp.float32)

    def score_stage(c, cur):
        logits_to(jnp.minimum(c + 1, n_kc - 1), s_bufs[1 - cur])
        buf = s_bufs[cur]
        acc = None
        for h in range(IDX_HEADS):
            lanes = slice((h % 2) * Q_BLOCK, (h % 2 + 1) * Q_BLOCK)
            term = w_rows[h] * jnp.maximum(buf[h // 2, :, lanes], 0.0)
            acc = term if acc is None else acc + term
        sc_ref[c] = jnp.where(c * KEY_CHUNK + k_off <= q_pos, acc, -jnp.inf)

    def score_step(j, carry):
        score_stage(2 * j, 0)
        score_stage(2 * j + 1, 1)
        return carry

    logits_to(0, s_a)
    lax.fori_loop(0, n_steps, score_step, 0)

    def count(pred):
        rows = COUNT_ROWS

        def add_chunk(c, acc):
            hit = jnp.where(pred(sc_ref[c]), 1.0, 0.0)
            for r in range(KEY_CHUNK // rows):
                acc = acc + hit[r * rows:(r + 1) * rows]
            return acc

        acc = lax.fori_loop(0, n_chunks // 2, lambda j, a: add_chunk(2 * j + 1, add_chunk(2 * j, a)),
                            jnp.zeros((rows, Q_BLOCK), jnp.float32))
        acc = lax.fori_loop(0, n_chunks % 2, lambda _, a: add_chunk(n_chunks - 1, a), acc)
        return jnp.sum(acc, axis=0, keepdims=True)

    @pl.when((i + 1) * Q_BLOCK <= topk)
    def _():
        par_ref[0:1, :] = jnp.full((1, Q_BLOCK), -jnp.inf, jnp.float32)
        par_ref[1:2, :] = jnp.full((1, Q_BLOCK), NO_SURPLUS, jnp.float32)

    @pl.when((i + 1) * Q_BLOCK > topk)
    def _():
        def bit_step(b, carry):
            key, n_ge = carry
            cand = key | jnp.left_shift(jnp.int32(1), 31 - b)
            f = _key_to_float(cand)
            total = count(lambda s: s >= f)
            take = total >= topk
            return jnp.where(take, cand, key), jnp.where(take, total, n_ge)

        key, n_ge = lax.fori_loop(0, 32, bit_step, (jnp.zeros((1, Q_BLOCK), jnp.int32),
                                                     jnp.zeros((1, Q_BLOCK), jnp.float32)))
        thr = _key_to_float(key)
        surplus = n_ge - topk

        par_ref[0:1, :] = thr
        par_ref[1:2, :] = surplus

    thr = par_ref[0:1, :]
    surplus = par_ref[1:2, :]
    n_groups = KEY_CHUNK // SUBLANES
    sub = lax.broadcasted_iota(jnp.int32, (1, SUBLANES, Q_BLOCK), 1)
    keep = {k: jnp.where(sub < SUBLANES - k, 1.0, 0.0) for k in (1, 2, 4)}

    def selection_bias(c, later):
        s = sc_ref[c]
        eq = s == thr
        cnt3 = jnp.where(eq, 1.0, 0.0).reshape(n_groups, SUBLANES, Q_BLOCK)
        for k in (1, 2, 4):
            cnt3 = cnt3 + pltpu.roll(cnt3, SUBLANES - k, 1) * keep[k]
        groups = [None] * n_groups
        for g in reversed(range(n_groups)):
            groups[g] = cnt3[g] + later
            later = later + cnt3[g, 0:1, :]
        from_here = jnp.concatenate(groups, axis=0)
        sel = (s > thr) | (eq & (from_here > surplus))
        return jnp.where(sel, 0.0, MASK_VALUE), later

    n_pairs = ATT_HEADS // 2
    half = ATT_HEAD_DIM

    pairs = range(n_pairs)
    p_bufs = (p_a, p_b)
    last = 2 * n_steps - 1

    def scores_to(c, s_buf, later):
        start = pl.multiple_of(c * KEY_CHUNK, KEY_CHUNK)
        bias, later = selection_bias(c, later)
        bias2 = jnp.concatenate([bias, bias], axis=1)
        cmax = []
        for p in pairs:
            s = jnp.dot(ka_ref[0, pl.ds(start, KEY_CHUNK), p * LANES:(p + 1) * LANES], q_pairs[p],
                        preferred_element_type=jnp.float32) + bias2
            s_buf[p] = s
            cmax.append(jnp.max(s, axis=0, keepdims=True))
        return cmax, later

    ones_rows = jnp.ones((BF16_ROWS, KEY_CHUNK), jnp.bfloat16)

    def weighted_values(c, p_buf):
        return [jnp.dot(jnp.concatenate([vt_ref[0, c, p * LANES:(p + 1) * LANES, :], ones_rows], axis=0),
                        p_buf[p], preferred_element_type=jnp.float32) for p in pairs]

    def accumulate(l, alpha, o):
        l_new = []
        for p in pairs:
            acc_ref[p, 0] = alpha[p][:, :Q_BLOCK] * acc_ref[p, 0] + o[p][:half, :Q_BLOCK]
            acc_ref[p, 1] = alpha[p][:, Q_BLOCK:] * acc_ref[p, 1] + o[p][half:2 * half, Q_BLOCK:]
            l_new.append(alpha[p] * l[p] + o[p][2 * half:2 * half + 1, :])
        return l_new

    def softmax_to(s_buf, p_buf, cmax, m):
        m_new = [jnp.maximum(m[p], cmax[p]) for p in pairs]
        alpha = [jnp.exp2(m[p] - m_new[p]) for p in pairs]
        for p in pairs:
            p_buf[p] = jnp.exp2(s_buf[p] - m_new[p]).astype(jnp.bfloat16)
        return alpha, m_new

    def stage(t, cur, carry):
        cmax, alpha_prev, m, l, later = carry
        c = last - t
        other = 1 - cur
        cmax_next, later = scores_to(jnp.maximum(c - 1, 0), s_bufs[other], later)
        o_prev = weighted_values(jnp.minimum(c + 1, last), p_bufs[other])
        alpha, m = softmax_to(s_bufs[cur], p_bufs[cur], cmax, m)
        return cmax_next, alpha, m, accumulate(l, alpha_prev, o_prev), later

    def att_step(j, carry):
        return stage(2 * j + 1, 1, stage(2 * j, 0, carry))

    zeros_row = jnp.zeros((1, 2 * Q_BLOCK), jnp.float32)
    p_b[...] = jnp.zeros(p_b.shape, p_b.dtype)
    acc_ref[...] = jnp.zeros(acc_ref.shape, acc_ref.dtype)
    cmax0, later0 = scores_to(last, s_a, jnp.zeros((1, Q_BLOCK), jnp.float32))
    carry = (cmax0, [zeros_row + 1.0 for _ in pairs], [zeros_row + MASK_VALUE for _ in pairs],
             [zeros_row for _ in pairs], later0)
    _, alpha_last, _, l, _ = lax.fori_loop(0, n_steps, att_step, carry)
    l = accumulate(l, alpha_last, weighted_values(0, p_b))
    for p in pairs:
        a0, a1 = acc_ref[p, 0], acc_ref[p, 1]
        psl = slice(p * LANES, (p + 1) * LANES)
        y_t = jnp.concatenate([a0 / l[p][:, :Q_BLOCK], a1 / l[p][:, Q_BLOCK:]], axis=0)
        out_ref[0, :, psl] = (y_t.T * ga_ref[0, :, psl].astype(jnp.float32)).astype(out_ref.dtype)


def _dsa(qi, wi, ki, qa, ka, vt, ga, topk):
    b, s, _ = qa.shape
    n_kc = s // KEY_CHUNK
    att_buf = (ATT_HEADS // 2, KEY_CHUNK, 2 * Q_BLOCK)
    qblk = lambda width: pl.BlockSpec((1, Q_BLOCK, width), lambda bi, i: (bi, i, 0))
    full = lambda width: pl.BlockSpec((1, s, width), lambda bi, i: (bi, 0, 0))
    return pl.pallas_call(
        functools.partial(_dsa_kernel, topk=topk),
        grid=(b, s // Q_BLOCK),
        in_specs=[qblk(IDX_WIDTH), qblk(LANES), full(LANES), qblk(ATT_WIDTH), full(ATT_WIDTH),
                  pl.BlockSpec((1, n_kc, ATT_WIDTH, KEY_CHUNK), lambda bi, i: (bi, 0, 0, 0)),
                  qblk(ATT_WIDTH)],
        out_specs=qblk(ATT_WIDTH),
        out_shape=jax.ShapeDtypeStruct((b, s, ATT_WIDTH), jnp.bfloat16),
        scratch_shapes=[pltpu.VMEM((n_kc, KEY_CHUNK, Q_BLOCK), jnp.float32),
                        pltpu.VMEM((SUBLANES, Q_BLOCK), jnp.float32),
                        pltpu.VMEM(att_buf, jnp.float32), pltpu.VMEM(att_buf, jnp.float32),
                        pltpu.VMEM(att_buf, jnp.bfloat16), pltpu.VMEM(att_buf, jnp.bfloat16),
                        pltpu.VMEM((ATT_HEADS // 2, 2, ATT_HEAD_DIM, Q_BLOCK), jnp.float32)],
        compiler_params=pltpu.CompilerParams(
            dimension_semantics=("arbitrary", "arbitrary"), vmem_limit_bytes=VMEM_LIMIT_BYTES),
        name="sparse_attention",
    )(qi, wi, ki, qa, ka, vt, ga)


_TN = (((0,), (0,)), ((), ()))


def _retention_out_kernel(qr_ref, kr_ref, vr_ref, gr_ref, decay_ref, zeta_ref, xi_ref, grow_ref,
                          x_ref, ya_ref, wa_ref, wr_ref, gain_ref, out_ref, state_ref, yr_ref):
    @pl.when(pl.program_id(1) == 0)
    def _():
        state_ref[...] = jnp.zeros_like(state_ref)

    lane = lax.broadcasted_iota(jnp.int32, (RET_CHUNK, LANES), 1)
    for pair in range(RET_HEADS // 2):
        psl = slice(pair * LANES, (pair + 1) * LANES)
        state = state_ref[psl, :]
        for r in range(RET_STEP_CHUNKS):
            rows = slice(r * RET_CHUNK, (r + 1) * RET_CHUNK)
            q_pair = qr_ref[0, rows, psl].astype(jnp.float32)
            k_pair = kr_ref[0, rows, psl].astype(jnp.float32)
            state_b = state.astype(jnp.bfloat16)
            kv = jnp.zeros((LANES, RET_V_DIM), jnp.float32)
            for sub in range(2):
                h = 2 * pair + sub
                vsl = slice(h * RET_V_DIM, (h + 1) * RET_V_DIM)
                in_head = (lane >= sub * RET_QK_DIM) & (lane < (sub + 1) * RET_QK_DIM)
                q_h = jnp.where(in_head, q_pair, 0.0)
                k_h = jnp.where(in_head, k_pair, 0.0)
                v = vr_ref[0, rows, vsl]
                scores = lax.dot_general(q_h.astype(jnp.bfloat16), k_h.astype(jnp.bfloat16), _NT,
                                         preferred_element_type=jnp.float32) * decay_ref[h]
                inner = jnp.dot(scores.astype(jnp.bfloat16), v, preferred_element_type=jnp.float32)
                cross = jnp.dot((q_h * xi_ref[h]).astype(jnp.bfloat16), state_b,
                                preferred_element_type=jnp.float32)
                o = inner + cross
                o = o * lax.rsqrt(jnp.mean(o * o, axis=-1, keepdims=True) + EPS)
                yr_ref[rows, vsl] = (o * gr_ref[0, rows, vsl].astype(jnp.float32)).astype(yr_ref.dtype)
                kv = kv + lax.dot_general((k_h * zeta_ref[h]).astype(jnp.bfloat16), v, _TN,
                                          preferred_element_type=jnp.float32)
            state = state * grow_ref[psl, :] + kv
        state_ref[psl, :] = state

    h = (x_ref[0]
         + jnp.dot(ya_ref[0], wa_ref[...], preferred_element_type=jnp.float32)
         + jnp.dot(yr_ref[...], wr_ref[...], preferred_element_type=jnp.float32))
    ms = jnp.mean(h * h, axis=-1, keepdims=True)
    out_ref[0] = (h * lax.rsqrt(ms + EPS)) * gain_ref[...]


def _retention_out(qr, kr, vr, gr, tables, x, ya, wa, wr, gain):
    b, s, d = x.shape
    decay, zeta_b, xi_b, g_rows = tables
    rows = RET_CHUNK * RET_STEP_CHUNKS
    blk = lambda width: pl.BlockSpec((1, rows, width), lambda bi, i: (bi, i, 0))
    const3 = lambda a: pl.BlockSpec(a.shape, lambda bi, i: (0, 0, 0))
    const2 = lambda a: pl.BlockSpec(a.shape, lambda bi, i: (0, 0))
    return pl.pallas_call(
        _retention_out_kernel,
        grid=(b, s // rows),
        in_specs=[blk(RET_QK_WIDTH), blk(RET_QK_WIDTH), blk(RET_V_WIDTH), blk(RET_V_WIDTH),
                  const3(decay), const3(zeta_b), const3(xi_b), const2(g_rows),
                  blk(d), blk(ATT_WIDTH), const2(wa), const2(wr), const2(gain)],
        out_specs=blk(d),
        out_shape=jax.ShapeDtypeStruct((b, s, d), jnp.float32),
        scratch_shapes=[pltpu.VMEM((RET_HEADS * RET_QK_DIM, RET_V_DIM), jnp.float32),
                        pltpu.VMEM((rows, RET_V_WIDTH), jnp.bfloat16)],
        compiler_params=pltpu.CompilerParams(
            dimension_semantics=("arbitrary", "arbitrary"), vmem_limit_bytes=VMEM_LIMIT_BYTES),
        name="retention_out_proj",
    )(qr, kr, vr, gr, decay, zeta_b, xi_b, g_rows, x, ya, wa, wr, gain)


def _prep_weight_kernel(w_ref, out_ref):
    offs = np.concatenate([[0], np.cumsum(SPLITS)]).tolist()
    q_a, k_a, v_a, g_a, q_i, k_i, w_i, q_r, k_r, v_r, g_r = [
        slice(offs[j], offs[j + 1]) for j in range(len(SPLITS))]
    dst = 0
    for src_cols in (q_a, k_a, v_a, g_a, q_i, q_r, k_r, v_r, g_r, k_i, k_i, w_i):
        width = src_cols.stop - src_cols.start
        out_ref[:, dst:dst + width] = w_ref[:, src_cols].astype(out_ref.dtype)
        dst += width
    out_ref[:, dst:] = jnp.zeros((out_ref.shape[0], PROJ_WIDTH - dst), out_ref.dtype)


def _prep_weight(w_in):
    d, width = w_in.shape
    tr = 128
    return pl.pallas_call(
        _prep_weight_kernel,
        grid=(d // tr,),
        in_specs=[pl.BlockSpec((tr, width), lambda r: (r, 0))],
        out_specs=pl.BlockSpec((tr, PROJ_WIDTH), lambda r: (r, 0)),
        out_shape=jax.ShapeDtypeStruct((d, PROJ_WIDTH), jnp.bfloat16),
        compiler_params=pltpu.CompilerParams(
            dimension_semantics=("arbitrary",), vmem_limit_bytes=VMEM_LIMIT_BYTES),
        name="prep_weight",
    )(w_in)


def kernel(x, norm_gain, w_in, w_out, final_gain):
    b, s, d = x.shape
    depth = norm_gain.shape[0]
    assert d == D_MODEL and s % Q_BLOCK == 0 and w_in.shape[2] == sum(SPLITS)
    topk = min(TOPK_MAX, s // 4)
    rope_a = _rotary_tables(s, ATT_HEAD_DIM, ATT_ROPE_DIM, ROPE_THETA)
    rope_r = _rotary_tables(s, RET_QK_DIM, RET_QK_DIM, RET_THETA)
    ret_tables = _retention_tables()
    assert depth == 1, "the final norm is fused into the single layer's output projection"
    w = _prep_weight(w_in[0])
    qa, ka, vt, ga, qi, qr, kr, vr, gr, ki, wi = _in_proj(
        x, norm_gain[0][None, :], w, rope_a, rope_r)
    ya = _dsa(qi, wi, ki, qa, ka, vt, ga, topk)
    wo = w_out[0].astype(jnp.bfloat16)
    return _retention_out(qr, kr, vr, gr, ret_tables, x, ya, wo[:ATT_WIDTH], wo[ATT_WIDTH:],
                          final_gain[None, :])
```
